```python
import math
import jax, jax.numpy as jnp
from jax import lax
import numpy as np

D_MODEL = 2048
BATCH = 2
SEQ = 4096
DEPTH = 2
DEC_BATCH = 32
DEC_SEQ = 4
PAST_LEN = 8192
PAGE_SIZE = 128

H_RET = 4
DH_RET = 128
RET_W = H_RET * DH_RET
H_FOX = 8
DH_FOX = 128
FOX_W = H_FOX * DH_FOX
SSM_W = 512
SSM_GROUP = 16
SSM_G = SSM_W // SSM_GROUP
SSM_P = 64
MIX_W = RET_W + FOX_W + SSM_W
IN_WIDTHS = [RET_W, RET_W, RET_W, RET_W, FOX_W, FOX_W, FOX_W, H_FOX, SSM_W]
IN_W = sum(IN_WIDTHS)
PEER_HEADS = 8
PEER_NKEYS = 128
PEER_N = PEER_NKEYS * PEER_NKEYS
PEER_DK = 128
PEER_TOPK = 16
PLE_DIM = 256
BLOCK = 128
ROPE_BASE = 10000.0
DEEPNORM_ALPHA = (2 * DEPTH) ** 0.25
DEEPNORM_BETA = (8 * DEPTH) ** -0.25
EPS = 1e-5

kernel_name = 'hymba_retnet_fox_s5_peer_step'

F32 = jnp.float32


def layer_norm(x, g, b):
    xf = x.astype(F32)
    mu = jnp.mean(xf, -1, keepdims=True)
    var = jnp.mean(jnp.square(xf - mu), -1, keepdims=True)
    return ((xf - mu) * lax.rsqrt(var + EPS) * g.astype(F32) + b.astype(F32)).astype(x.dtype)


def rms_norm(x, g):
    xf = x.astype(F32)
    return (xf * lax.rsqrt(jnp.mean(xf * xf, -1, keepdims=True) + EPS) * g.astype(F32)).astype(x.dtype)


def rotary(x, pos):
    d = x.shape[-1]
    inv = ROPE_BASE ** (-jnp.arange(0, d, 2, dtype=F32) / d)
    ang = pos.astype(F32)[:, None] * inv[None, :]
    cos = jnp.cos(ang)[None, :, None, :]
    sin = jnp.sin(ang)[None, :, None, :]
    x1, x2 = x[..., : d // 2], x[..., d // 2:]
    return jnp.concatenate([x1 * cos - x2 * sin, x1 * sin + x2 * cos], -1)


def retention_chunkwise(q, k, v, s0):
    b, L, h, _ = q.shape
    c = math.gcd(L, BLOCK)
    n = L // c
    lg = jnp.log1p(-jnp.exp2(-5.0 - jnp.arange(H_RET, dtype=F32)))
    idx = jnp.arange(c, dtype=F32)
    rel = idx[:, None] - idx[None, :]
    dmask = jnp.where(rel >= 0, jnp.exp(lg[:, None, None] * jnp.maximum(rel, 0.0)), 0.0)
    xi = jnp.exp(lg[:, None] * (idx + 1.0))
    zeta = jnp.exp(lg[:, None] * (c - 1.0 - idx))
    g_c = jnp.exp(lg * c)

    def chunks(t):
        return t.reshape(b, n, c, h, -1).transpose(1, 0, 3, 2, 4)

    def step(s, inp):
        qc, kc, vc = inp
        att = jnp.einsum('bhid,bhjd->bhij', qc, kc) * dmask
        o = jnp.einsum('bhij,bhje->bhie', att, vc) + jnp.einsum('bhid,bhde->bhie', qc * xi[..., None], s)
        s = g_c[:, None, None] * s + jnp.einsum('bhjd,bhje->bhde', kc * zeta[..., None], vc)
        return s, o

    s, o = lax.scan(step, s0, (chunks(q), chunks(k), chunks(v)))
    return o.transpose(1, 0, 3, 2, 4).reshape(b, L, h, -1), s


def fox_prompt(q, k, v, lf):
    b, L, h, d = q.shape
    bq = math.gcd(L, BLOCK)
    nb = L // bq
    scale = d ** -0.5
    c = jnp.cumsum(lf, axis=1).transpose(0, 2, 1)
    kpos = jnp.arange(L)

    def block(i):
        start = i * bq
        qb = lax.dynamic_slice_in_dim(q, start, bq, axis=1)
        cb = lax.dynamic_slice_in_dim(c, start, bq, axis=2)
        s = jnp.einsum('bqhd,bkhd->bhqk', qb, k).astype(F32) * scale
        s = s + cb[..., :, None] - c[..., None, :]
        qpos = start + jnp.arange(bq)
        s = jnp.where(kpos[None, :] <= qpos[:, None], s, -jnp.inf)
        p = jax.nn.softmax(s, axis=-1)
        return jnp.einsum('bhqk,bkhd->bqhd', p.astype(v.dtype), v)

    o = lax.map(block, jnp.arange(nb))
    return o.transpose(1, 0, 2, 3, 4).reshape(b, L, h, d)


def fox_sample(q, k, v, lf, k_past, v_past, lf_past):
    L = q.shape[1]
    P = k_past.shape[1]
    scale = q.shape[-1] ** -0.5
    cn = jnp.cumsum(lf, axis=1).transpose(0, 2, 1)
    lfp = lf_past.astype(F32)
    past_bias = (lax.cumsum(lfp, axis=1, reverse=True) - lfp).transpose(0, 2, 1)
    s_past = (jnp.einsum('bqhd,bkhd->bhqk', q, k_past).astype(F32) * scale
              + past_bias[:, :, None, :] + cn[..., :, None])
    s_new = jnp.einsum('bqhd,bkhd->bhqk', q, k).astype(F32) * scale + cn[..., :, None] - cn[..., None, :]
    s_new = jnp.where(jnp.tril(jnp.ones((L, L), bool)), s_new, -jnp.inf)
    p = jax.nn.softmax(jnp.concatenate([s_past, s_new], -1), axis=-1)
    return (jnp.einsum('bhqk,bkhd->bqhd', p[..., :P].astype(v_past.dtype), v_past)
            + jnp.einsum('bhqk,bkhd->bqhd', p[..., P:].astype(v.dtype), v))


def _ssm_combine(e1, e2):
    a1, b1 = e1
    a2, b2 = e2
    return a1 * a2, a2 * b1 + b2


def s5(u, h0, a_re, a_im, b_re, b_im, c_re, c_im, d_skip, log_dt):
    bsz, L, _ = u.shape
    lam = lax.complex(a_re.astype(F32), a_im.astype(F32))
    lam_dt = lam * jnp.exp(log_dt.astype(F32))[:, None]
    lam_bar = jnp.exp(lam_dt)
    b_bar = ((lam_bar - 1.0) / lam)[..., None] * lax.complex(b_re.astype(F32), b_im.astype(F32))
    c_mat = lax.complex(c_re.astype(F32), c_im.astype(F32))
    uf = u.astype(F32)
    ug = uf.reshape(bsz, L, SSM_G, SSM_GROUP).astype(jnp.complex64)
    bu = jnp.einsum('gpc,blgc->blgp', b_bar, ug)
    a = jnp.broadcast_to(lam_bar, bu.shape)
    _, hs = lax.associative_scan(_ssm_combine, (a, bu), axis=1)
    t = jnp.arange(1, L + 1, dtype=F32)
    hs = hs + jnp.exp(lam_dt[None] * t[:, None, None])[None] * h0[:, None]
    y = jnp.einsum('gcp,blgp->blgc', c_mat, hs).real.reshape(bsz, L, SSM_W) + d_skip.astype(F32) * uf
    return y, hs[:, -1]


def token_mixer(x, lw, pos, ret_s0, ssm_h0, fox_past):
    b, L, _ = x.shape
    z = x @ lw['w_in']
    rq, rk, rv, rg, fq, fk, fv, ff, su = jnp.split(z, np.cumsum(IN_WIDTHS)[:-1].tolist(), axis=-1)
    rq = rotary(rq.astype(F32).reshape(b, L, H_RET, DH_RET), pos)
    rk = rotary(rk.astype(F32).reshape(b, L, H_RET, DH_RET), pos) * DH_RET ** -0.5
    rv = rv.astype(F32).reshape(b, L, H_RET, DH_RET)
    ro, ret_s = retention_chunkwise(rq, rk, rv, ret_s0)
    mu = jnp.mean(ro, -1, keepdims=True)
    var = jnp.mean(jnp.square(ro - mu), -1, keepdims=True)
    ro = ((ro - mu) * lax.rsqrt(var + EPS)).reshape(b, L, RET_W)
    ro = ro * lw['ret_gn_g'].astype(F32) * jax.nn.silu(rg.astype(F32))
    fq = fq.reshape(b, L, H_FOX, DH_FOX)
    fk = fk.reshape(b, L, H_FOX, DH_FOX)
    fv = fv.reshape(b, L, H_FOX, DH_FOX)
    lf = jax.nn.log_sigmoid(ff.astype(F32) + lw['b_fox_f'].astype(F32))
    if fox_past is None:
        fo = fox_prompt(fq, fk, fv, lf)
    else:
        fo = fox_sample(fq, fk, fv, lf, *fox_past)
    fo = rms_norm(fo.reshape(b, L, FOX_W), lw['fox_norm_g'])
    sy, ssm_last = s5(su, ssm_h0, lw['ssm_a_re'], lw['ssm_a_im'], lw['ssm_b_re'], lw['ssm_b_im'],
                      lw['ssm_c_re'], lw['ssm_c_im'], lw['ssm_d'], lw['ssm_log_dt'])
    sy = jax.nn.gelu(sy)
    so = sy * jax.nn.sigmoid(sy @ lw['w_glu'].astype(F32) + lw['b_glu'].astype(F32))
    so = rms_norm(so, lw['ssm_norm_g'])
    mixed = jnp.concatenate([ro, fo.astype(F32), so], -1).astype(x.dtype)
    return mixed @ lw['w_out'], (fk, fv, lf, ret_s, ssm_last.real, ssm_last.imag)


def peer(x, w_q, sub_keys, u_tab, v_tab):
    b, L, d = x.shape
    T = b * L
    xt = x.reshape(T, d)
    q = (xt @ w_q).reshape(T, PEER_HEADS, 2, PEER_DK // 2)
    s = jnp.einsum('thzd,hznd->thzn', q, sub_keys).astype(F32)
    sv, si = lax.top_k(s, PEER_TOPK)
    cand = (sv[:, :, 0, :, None] + sv[:, :, 1, None, :]).reshape(T, PEER_HEADS, -1)
    cidx = (si[:, :, 0, :, None] * PEER_NKEYS + si[:, :, 1, None, :]).reshape(T, PEER_HEADS, -1)
    top, sel = lax.top_k(cand, PEER_TOPK)
    eidx = jnp.take_along_axis(cidx, sel, axis=-1).reshape(T, -1)
    gate = jax.nn.softmax(top, axis=-1).reshape(T, -1)
    tb = math.gcd(T, BLOCK)
    nb = T // tb

    def block(args):
        xb, ib, gb = args
        h = jax.nn.gelu(jnp.einsum('td,ted->te', xb, u_tab[ib]).astype(F32))
        return jnp.einsum('te,ted->td', (gb * h).astype(v_tab.dtype), v_tab[ib])

    out = lax.map(block, (xt.reshape(nb, tb, d), eidx.reshape(nb, tb, -1), gate.reshape(nb, tb, -1)))
    return out.reshape(b, L, d).astype(x.dtype)


def layer(x, p_l, lw, pos, ret_s0, ssm_h0, fox_past):
    mix, st = token_mixer(x, lw, pos, ret_s0, ssm_h0, fox_past)
    x = layer_norm(DEEPNORM_ALPHA * x + mix, lw['ln1_g'], lw['ln1_b'])
    x = layer_norm(DEEPNORM_ALPHA * x + peer(x, lw['peer_w_q'], lw['peer_sub_keys'], lw['peer_u'], lw['peer_v']),
                   lw['ln2_g'], lw['ln2_b'])
    g = jax.nn.sigmoid((x @ lw['w_pg'] + lw['b_pg']).astype(F32))
    x = (x.astype(F32) + g * (p_l @ lw['w_pe']).astype(F32)).astype(x.dtype)
    return x, st


def setup_inputs(seed: int = 0) -> dict:
    key = jax.random.key(seed)
    ks = iter(jax.random.split(key, 48))

    def nrm(shape, scale):
        return jax.random.normal(next(ks), shape, F32) * scale

    n_pages = PAST_LEN // PAGE_SIZE
    n_used = DEC_BATCH * n_pages
    n_pool = n_used + max(1, n_used // 4)
    page_table = jax.random.permutation(next(ks), n_pool)[:n_used].reshape(DEC_BATCH, n_pages).astype(jnp.int32)
    return {
        'x_prompt': nrm((BATCH, SEQ, D_MODEL), 1.0),
        'x_sample': nrm((DEC_BATCH, DEC_SEQ, D_MODEL), 1.0),
        'p_prompt': nrm((DEPTH, BATCH, SEQ, PLE_DIM), 1.0),
        'p_sample': nrm((DEPTH, DEC_BATCH, DEC_SEQ, PLE_DIM), 1.0),
        'cache_fox_k': nrm((DEPTH, n_pool, PAGE_SIZE, H_FOX, DH_FOX), 1.0),
        'cache_fox_v': nrm((DEPTH, n_pool, PAGE_SIZE, H_FOX, DH_FOX), 1.0),
        'cache_fox_lf': -jax.nn.softplus(-(3.0 + nrm((DEPTH, n_pool, PAGE_SIZE, H_FOX), 1.0))),
        'state_ret': nrm((DEPTH, DEC_BATCH, H_RET, DH_RET, DH_RET), 1.0),
        'state_ssm_re': nrm((DEPTH, DEC_BATCH, SSM_G, SSM_P), 0.1),
        'state_ssm_im': nrm((DEPTH, DEC_BATCH, SSM_G, SSM_P), 0.1),
        'page_table': page_table,
        'w_in': nrm((DEPTH, D_MODEL, IN_W), D_MODEL ** -0.5),
        'w_out': nrm((DEPTH, MIX_W, D_MODEL), DEEPNORM_BETA * MIX_W ** -0.5),
        'ret_gn_g': 1.0 + nrm((DEPTH, RET_W), 0.02),
        'b_fox_f': 2.0 + 2.0 * jax.random.uniform(next(ks), (DEPTH, H_FOX), F32),
        'fox_norm_g': 1.0 + nrm((DEPTH, FOX_W), 0.02),
        'ssm_a_re': -0.5 + nrm((DEPTH, SSM_G, SSM_P), 0.01),
        'ssm_a_im': jnp.pi * jnp.arange(SSM_P, dtype=F32) + nrm((DEPTH, SSM_G, SSM_P), 0.01),
        'ssm_b_re': nrm((DEPTH, SSM_G, SSM_P, SSM_GROUP), (2 * SSM_GROUP) ** -0.5),
        'ssm_b_im': nrm((DEPTH, SSM_G, SSM_P, SSM_GROUP), (2 * SSM_GROUP) ** -0.5),
        'ssm_c_re': nrm((DEPTH, SSM_G, SSM_GROUP, SSM_P), (2 * SSM_P) ** -0.5),
        'ssm_c_im': nrm((DEPTH, SSM_G, SSM_GROUP, SSM_P), (2 * SSM_P) ** -0.5),
        'ssm_d': nrm((DEPTH, SSM_W), 1.0),
        'ssm_log_dt': jax.random.uniform(next(ks), (DEPTH, SSM_G), F32, math.log(1e-3), math.log(1e-1)),
        'w_glu': nrm((DEPTH, SSM_W, SSM_W), SSM_W ** -0.5),
        'b_glu': nrm((DEPTH, SSM_W), 0.02),
        'ssm_norm_g': 1.0 + nrm((DEPTH, SSM_W), 0.02),
        'ln1_g': 1.0 + nrm((DEPTH, D_MODEL), 0.02),
        'ln1_b': nrm((DEPTH, D_MODEL), 0.02),
        'ln2_g': 1.0 + nrm((DEPTH, D_MODEL), 0.02),
        'ln2_b': nrm((DEPTH, D_MODEL), 0.02),
        'peer_w_q': nrm((DEPTH, D_MODEL, PEER_HEADS * PEER_DK), D_MODEL ** -0.5),
        'peer_sub_keys': nrm((DEPTH, PEER_HEADS, 2, PEER_NKEYS, PEER_DK // 2), (PEER_DK // 2) ** -0.5),
        'peer_u': nrm((DEPTH, PEER_N, D_MODEL), D_MODEL ** -0.5),
        'peer_v': nrm((DEPTH, PEER_N, D_MODEL), DEEPNORM_BETA),
        'w_pe': nrm((DEPTH, PLE_DIM, D_MODEL), 0.5 * PLE_DIM ** -0.5),
        'w_pg': nrm((DEPTH, D_MODEL, D_MODEL), D_MODEL ** -0.5),
        'b_pg': nrm((DEPTH, D_MODEL), 0.02),
    }


def reference(x_prompt, x_sample, p_prompt, p_sample, cache_fox_k, cache_fox_v, cache_fox_lf,
              state_ret, state_ssm_re, state_ssm_im, page_table,
              w_in, w_out, ret_gn_g, b_fox_f, fox_norm_g,
              ssm_a_re, ssm_a_im, ssm_b_re, ssm_b_im, ssm_c_re, ssm_c_im, ssm_d, ssm_log_dt,
              w_glu, b_glu, ssm_norm_g, ln1_g, ln1_b, ln2_g, ln2_b,
              peer_w_q, peer_sub_keys, peer_u, peer_v, w_pe, w_pg, b_pg):
    bp, lp, _ = x_prompt.shape
    bs, ls, _ = x_sample.shape
    past_len = page_table.shape[1] * cache_fox_k.shape[2]
    pos_p = jnp.arange(lp)
    pos_s = past_len + jnp.arange(ls)
    xp, xs = x_prompt, x_sample
    st_p_all, st_s_all = [], []
    for l in range(DEPTH):
        lw = dict(w_in=w_in[l], w_out=w_out[l], ret_gn_g=ret_gn_g[l], b_fox_f=b_fox_f[l], fox_norm_g=fox_norm_g[l],
                  ssm_a_re=ssm_a_re[l], ssm_a_im=ssm_a_im[l], ssm_b_re=ssm_b_re[l], ssm_b_im=ssm_b_im[l],
                  ssm_c_re=ssm_c_re[l], ssm_c_im=ssm_c_im[l], ssm_d=ssm_d[l], ssm_log_dt=ssm_log_dt[l],
                  w_glu=w_glu[l], b_glu=b_glu[l], ssm_norm_g=ssm_norm_g[l],
                  ln1_g=ln1_g[l], ln1_b=ln1_b[l], ln2_g=ln2_g[l], ln2_b=ln2_b[l],
                  peer_w_q=peer_w_q[l], peer_sub_keys=peer_sub_keys[l], peer_u=peer_u[l], peer_v=peer_v[l],
                  w_pe=w_pe[l], w_pg=w_pg[l], b_pg=b_pg[l])
        xp, st_p = layer(xp, p_prompt[l], lw, pos_p,
                         jnp.zeros((bp, H_RET, DH_RET, DH_RET), F32),
                         jnp.zeros((bp, SSM_G, SSM_P), jnp.complex64), None)
        k_past = cache_fox_k[l][page_table].reshape(bs, past_len, H_FOX, DH_FOX)
        v_past = cache_fox_v[l][page_table].reshape(bs, past_len, H_FOX, DH_FOX)
        lf_past = cache_fox_lf[l][page_table].reshape(bs, past_len, H_FOX)
        h0 = lax.complex(state_ssm_re[l].astype(F32), state_ssm_im[l].astype(F32))
        xs, st_s = layer(xs, p_sample[l], lw, pos_s, state_ret[l].astype(F32), h0, (k_past, v_past, lf_past))
        st_p_all.append(st_p)
        st_s_all.append(st_s)

    def stack(sts, i):
        return jnp.stack([s[i] for s in sts])

    new_k_prompt, new_v_prompt, new_lf_prompt = stack(st_p_all, 0), stack(st_p_all, 1), stack(st_p_all, 2)
    ret_state_prompt, ssm_re_prompt, ssm_im_prompt = stack(st_p_all, 3), stack(st_p_all, 4), stack(st_p_all, 5)
    new_k_sample, new_v_sample, new_lf_sample = stack(st_s_all, 0), stack(st_s_all, 1), stack(st_s_all, 2)
    ret_state_sample, ssm_re_sample, ssm_im_sample = stack(st_s_all, 3), stack(st_s_all, 4), stack(st_s_all, 5)
    return (xp, xs, new_k_prompt, new_v_prompt, new_lf_prompt, ret_state_prompt, ssm_re_prompt, ssm_im_prompt,
            new_k_sample, new_v_sample, new_lf_sample, ret_state_sample, ssm_re_sample, ssm_im_sample)
```

```python
import functools
import math

import jax
import jax.numpy as jnp
from jax import lax
from jax.experimental import pallas as pl
from jax.experimental.pallas import tpu as pltpu

F32 = jnp.float32
BF16 = jnp.bfloat16
HI = lax.Precision.HIGHEST

D_MODEL = 2048
DEPTH = 2
H_RET, DH_RET = 4, 128
RET_W = H_RET * DH_RET
H_FOX, DH_FOX = 8, 128
FOX_W = H_FOX * DH_FOX
SSM_W, SSM_GROUP = 512, 16
SSM_G = SSM_W // SSM_GROUP
SSM_P = 64
PEER_HEADS, PEER_NKEYS, PEER_DK, PEER_TOPK = 8, 128, 128, 16
PEER_N = PEER_NKEYS * PEER_NKEYS
PLE_DIM = 256
BLOCK = 128
ROPE_BASE = 10000.0
DEEPNORM_ALPHA = (2 * DEPTH) ** 0.25
EPS = 1e-5

Z_W = 4 * RET_W + 3 * FOX_W + SSM_W
Z_FQ, Z_FK, Z_FV, Z_SU = 4 * RET_W, 4 * RET_W + FOX_W, 4 * RET_W + 2 * FOX_W, 4 * RET_W + 3 * FOX_W

VMEM_LIMIT = 56 * 1024 * 1024
NT_DIMS = (((1,), (1,)), ((), ()))
TN_DIMS = (((0,), (0,)), ((), ()))


def _params(*sem):
    return pltpu.CompilerParams(dimension_semantics=sem, vmem_limit_bytes=VMEM_LIMIT)


def _gelu(x):
    return 0.5 * x * (1.0 + jnp.tanh(math.sqrt(2.0 / math.pi) * (x + 0.044715 * (x * x * x))))


def _sigmoid(x):
    return 1.0 / (1.0 + jnp.exp(-x))


def _log_sigmoid(x):
    return jnp.minimum(x, 0.0) - jnp.log1p(jnp.exp(-jnp.abs(x)))


def _layer_norm(y, g, b):
    mu = jnp.mean(y, -1, keepdims=True)
    var = jnp.mean(jnp.square(y - mu), -1, keepdims=True)
    return (y - mu) * lax.rsqrt(var + EPS) * g + b


def _rms_norm(y, g):
    return y * lax.rsqrt(jnp.mean(y * y, -1, keepdims=True) + EPS) * g


def _mm_kernel(x_ref, w_ref, o_ref):
    o_ref[...] = jnp.dot(x_ref[...].astype(BF16), w_ref[...], preferred_element_type=F32)


def _mm3_kernel(x_ref, wh_ref, wl_ref, o_ref):
    x = x_ref[...]
    xh = x.astype(BF16)
    xl = (x - xh.astype(F32)).astype(BF16)
    wh = wh_ref[...]
    acc = jnp.dot(xh, wh, preferred_element_type=F32)
    acc += jnp.dot(xl, wh, preferred_element_type=F32)
    acc += jnp.dot(xh, wl_ref[...], preferred_element_type=F32)
    o_ref[...] = acc


def _matmul(x, ws, tm, tn, name):
    t, k = x.shape
    n = ws[0].shape[1]
    kern = _mm_kernel if len(ws) == 1 else _mm3_kernel
    return pl.pallas_call(
        kern,
        out_shape=jax.ShapeDtypeStruct((t, n), F32),
        grid=(t // tm, n // tn),
        in_specs=[pl.BlockSpec((tm, k), lambda i, j: (i, 0))]
        + [pl.BlockSpec((k, tn), lambda i, j: (0, j)) for _ in ws],
        out_specs=pl.BlockSpec((tm, tn), lambda i, j: (i, j)),
        compiler_params=_params("parallel", "arbitrary"),
        name=name,
    )(x, *ws)


def _gate_kernel(x_ref, wc_ref, wr_ref, bc_ref, br_ref, lfc_ref, lfr_ref):
    x = x_ref[...]
    fc = jnp.dot(x, wc_ref[...], precision=HI, preferred_element_type=F32) + bc_ref[...]
    fr = lax.dot_general(wr_ref[...], x, NT_DIMS, precision=HI, preferred_element_type=F32) + br_ref[...]
    lfc_ref[...] = _log_sigmoid(fc)
    lfr_ref[...] = _log_sigmoid(fr)


def _forget_gates(x, w_ff, b_ff, tm):
    t, k = x.shape
    wc = jnp.zeros((k, 128), F32).at[:, :H_FOX].set(w_ff)
    bc = jnp.zeros((1, 128), F32).at[0, :H_FOX].set(b_ff)
    return pl.pallas_call(
        _gate_kernel,
        out_shape=(jax.ShapeDtypeStruct((t, 128), F32), jax.ShapeDtypeStruct((H_FOX, t), F32)),
        grid=(t // tm,),
        in_specs=[
            pl.BlockSpec((tm, k), lambda i: (i, 0)),
            pl.BlockSpec((k, 128), lambda i: (0, 0)),
            pl.BlockSpec((H_FOX, k), lambda i: (0, 0)),
            pl.BlockSpec((1, 128), lambda i: (0, 0)),
            pl.BlockSpec((H_FOX, 1), lambda i: (0, 0)),
        ],
        out_specs=(pl.BlockSpec((tm, 128), lambda i: (i, 0)), pl.BlockSpec((H_FOX, tm), lambda i: (0, i))),
        compiler_params=_params("parallel"),
        name="forget_gates",
    )(x, wc, w_ff.T, bc, b_ff.reshape(H_FOX, 1))


def _cumsum_kernel(lfc_ref, lfr_ref, cc_ref, cr_ref, carry_c, carry_r):
    @pl.when(pl.program_id(1) == 0)
    def _():
        carry_c[...] = jnp.zeros_like(carry_c)
        carry_r[...] = jnp.zeros_like(carry_r)

    n = lfc_ref.shape[0]
    r = lax.broadcasted_iota(jnp.int32, (n, n), 0)
    c = lax.broadcasted_iota(jnp.int32, (n, n), 1)
    lower = (r >= c).astype(F32)
    upper = (r <= c).astype(F32)
    cc = jnp.dot(lower, lfc_ref[...], precision=HI, preferred_element_type=F32) + carry_c[...]
    cr = jnp.dot(lfr_ref[...], upper, precision=HI, preferred_element_type=F32) + carry_r[...]
    cc_ref[...] = cc
    cr_ref[...] = cr
    carry_c[...] = cc[n - 1:n, :]
    carry_r[...] = cr[:, n - 1:n]


def _forget_cumsum(lfc, lfr, nb, seq):
    n = 128
    nc = seq // n
    return pl.pallas_call(
        _cumsum_kernel,
        out_shape=(jax.ShapeDtypeStruct((nb * seq, 128), F32), jax.ShapeDtypeStruct((H_FOX, nb * seq), F32)),
        grid=(nb, nc),
        in_specs=[
            pl.BlockSpec((n, 128), lambda b, i: (b * nc + i, 0)),
            pl.BlockSpec((H_FOX, n), lambda b, i: (0, b * nc + i)),
        ],
        out_specs=(
            pl.BlockSpec((n, 128), lambda b, i: (b * nc + i, 0)),
            pl.BlockSpec((H_FOX, n), lambda b, i: (0, b * nc + i)),
        ),
        scratch_shapes=[pltpu.VMEM((1, 128), F32), pltpu.VMEM((H_FOX, 1), F32)],
        compiler_params=_params("parallel", "arbitrary"),
        name="forget_cumsum",
    )(lfc, lfr)


def _fox_prompt_kernel(q_ref, k_ref, v_ref, cc_ref, cr_ref, g_ref, o_ref, m_sc, l_sc, acc_sc, *, scale):
    qi = pl.program_id(1)
    ki = pl.program_id(2)
    bq = q_ref.shape[0]
    bk = k_ref.shape[0]

    @pl.when(ki == 0)
    def _():
        m_sc[...] = jnp.full_like(m_sc, -jnp.inf)
        l_sc[...] = jnp.zeros_like(l_sc)
        acc_sc[...] = jnp.zeros_like(acc_sc)

    @pl.when(ki <= qi)
    def _():
        rows = qi * bq + lax.broadcasted_iota(jnp.int32, (bq, bk), 0)
        cols = ki * bk + lax.broadcasted_iota(jnp.int32, (bq, bk), 1)
        causal = cols <= rows
        for h in range(H_FOX):
            sl = slice(h * DH_FOX, (h + 1) * DH_FOX)
            q = q_ref[:, sl].astype(BF16)
            k = k_ref[:, sl].astype(BF16)
            s = lax.dot_general(q, k, NT_DIMS, preferred_element_type=F32) * scale
            s = s + (cc_ref[:, h:h + 1] - cr_ref[h:h + 1, :])
            s = jnp.where(causal, s, -jnp.inf)
            m_prev = m_sc[h]
            m_new = jnp.maximum(m_prev, jnp.max(s, axis=1, keepdims=True))
            alpha = jnp.exp(m_prev - m_new)
            p = jnp.exp(s - m_new)
            l_sc[h] = alpha * l_sc[h] + jnp.sum(p, axis=1, keepdims=True)
            acc_sc[:, sl] = alpha * acc_sc[:, sl] + jnp.dot(
                p.astype(BF16), v_ref[:, sl].astype(BF16), preferred_element_type=F32)
            m_sc[h] = m_new

    @pl.when(ki == qi)
    def _():
        for h in range(H_FOX):
            sl = slice(h * DH_FOX, (h + 1) * DH_FOX)
            acc_sc[:, sl] = acc_sc[:, sl] / l_sc[h]
        o_ref[...] = _rms_norm(acc_sc[...], g_ref[...])


def _fox_prompt(z, cc, cr, norm_g, nb, seq, t_all, blk):
    nq = seq // blk
    cq, ck, cv = Z_FQ // FOX_W, Z_FK // FOX_W, Z_FV // FOX_W
    kern = functools.partial(_fox_prompt_kernel, scale=DH_FOX ** -0.5)
    return pl.pallas_call(
        kern,
        out_shape=jax.ShapeDtypeStruct((t_all, FOX_W), F32),
        grid=(nb, nq, nq),
        in_specs=[
            pl.BlockSpec((blk, FOX_W), lambda b, i, j: (b * nq + i, cq)),
            pl.BlockSpec((blk, FOX_W), lambda b, i, j: (b * nq + jnp.minimum(i, j), ck)),
            pl.BlockSpec((blk, FOX_W), lambda b, i, j: (b * nq + jnp.minimum(i, j), cv)),
            pl.BlockSpec((blk, 128), lambda b, i, j: (b * nq + i, 0)),
            pl.BlockSpec((H_FOX, blk), lambda b, i, j: (0, b * nq + jnp.minimum(i, j))),
            pl.BlockSpec((1, FOX_W), lambda b, i, j: (0, 0)),
        ],
        out_specs=pl.BlockSpec((blk, FOX_W), lambda b, i, j: (b * nq + i, 0)),
        scratch_shapes=[
            pltpu.VMEM((H_FOX, blk, 1), F32),
            pltpu.VMEM((H_FOX, blk, 1), F32),
            pltpu.VMEM((blk, FOX_W), F32),
        ],
        compiler_params=_params("parallel", "parallel", "arbitrary"),
        name="fox_prompt",
    )(z, z, z, cc, cr, norm_g.reshape(1, FOX_W))


def _fox_sample_kernel(pt_ref, q_ref, kn_ref, vn_ref, lfn_ref, lfnt_ref, kp_ref, vp_ref, lfp_ref, g_ref,
                       o_ref, m_sc, l_sc, acc_sc, carry_sc, *, scale, n_new):
    del pt_ref
    step = pl.program_id(1)
    nq = q_ref.shape[0]
    pg = kp_ref.shape[0]

    @pl.when(step == 0)
    def _():
        m_sc[...] = jnp.full_like(m_sc, -jnp.inf)
        l_sc[...] = jnp.zeros_like(l_sc)
        acc_sc[...] = jnp.zeros_like(acc_sc)
        carry_sc[...] = jnp.zeros_like(carry_sc)

    lfn = lfn_ref[...]
    cn = lfn
    for sh in range(1, n_new):
        cn = cn + jnp.concatenate([jnp.zeros((sh, H_FOX), F32), lfn[:nq - sh]], axis=0)

    eye = (lax.broadcasted_iota(jnp.int32, (H_FOX, H_FOX), 0)
           == lax.broadcasted_iota(jnp.int32, (H_FOX, H_FOX), 1)).astype(F32)
    lfp_t = lax.dot_general(eye, lfp_ref[...], NT_DIMS, precision=HI, preferred_element_type=F32)
    later = (lax.broadcasted_iota(jnp.int32, (pg, pg), 0)
             > lax.broadcasted_iota(jnp.int32, (pg, pg), 1)).astype(F32)
    bias = jnp.dot(lfp_t, later, precision=HI, preferred_element_type=F32) + carry_sc[...]
    carry_sc[...] = carry_sc[...] + jnp.sum(lfp_t, axis=1, keepdims=True)

    for h in range(H_FOX):
        sl = slice(h * DH_FOX, (h + 1) * DH_FOX)
        q = q_ref[:, sl]
        s = lax.dot_general(q, kp_ref[:, h, :], NT_DIMS, preferred_element_type=F32) * scale
        s = s + bias[h:h + 1, :] + cn[:, h:h + 1]
        m_prev = m_sc[h]
        m_new = jnp.maximum(m_prev, jnp.max(s, axis=1, keepdims=True))
        alpha = jnp.exp(m_prev - m_new)
        p = jnp.exp(s - m_new)
        l_sc[h] = alpha * l_sc[h] + jnp.sum(p, axis=1, keepdims=True)
        acc_sc[:, sl] = alpha * acc_sc[:, sl] + jnp.dot(p, vp_ref[:, h, :], preferred_element_type=F32)
        m_sc[h] = m_new

    @pl.when(step == pl.num_programs(1) - 1)
    def _():
        lfnt = lfnt_ref[...]
        cnt = lfnt
        for sh in range(1, n_new):
            cnt = cnt + jnp.concatenate([jnp.zeros((H_FOX, sh), F32), lfnt[:, :nq - sh]], axis=1)
        rows = lax.broadcasted_iota(jnp.int32, (nq, nq), 0)
        cols = lax.broadcasted_iota(jnp.int32, (nq, nq), 1)
        causal = cols <= rows
        for h in range(H_FOX):
            sl = slice(h * DH_FOX, (h + 1) * DH_FOX)
            q = q_ref[:, sl]
            s = lax.dot_general(q, kn_ref[:, sl], NT_DIMS, preferred_element_type=F32) * scale
            s = s + cn[:, h:h + 1] - cnt[h:h + 1, :]
            s = jnp.where(causal, s, -jnp.inf)
            m_prev = m_sc[h]
            m_new = jnp.maximum(m_prev, jnp.max(s, axis=1, keepdims=True))
            alpha = jnp.exp(m_prev - m_new)
            p = jnp.exp(s - m_new)
            l_new = alpha * l_sc[h] + jnp.sum(p, axis=1, keepdims=True)
            acc = alpha * acc_sc[:, sl] + jnp.dot(p, vn_ref[:, sl], preferred_element_type=F32)
            acc_sc[:, sl] = acc / l_new
        o_ref[...] = _rms_norm(acc_sc[...], g_ref[...])


def _fox_sample(zs, lfn, lfnt, cache_k, cache_v, cache_lf, page_table, layer, norm_g, n_new):
    ns, r, _ = zs.shape
    n_pages = page_table.shape[1]
    pg = cache_k.shape[2]
    cq, ck, cv = Z_FQ // FOX_W, Z_FK // FOX_W, Z_FV // FOX_W
    kern = functools.partial(_fox_sample_kernel, scale=DH_FOX ** -0.5, n_new=n_new)

    def page(b, s, pt):
        return pt[b * n_pages + (n_pages - 1 - s)]

    grid_spec = pltpu.PrefetchScalarGridSpec(
        num_scalar_prefetch=1,
        grid=(ns, n_pages),
        in_specs=[
            pl.BlockSpec((None, r, FOX_W), lambda b, s, pt: (b, 0, cq)),
            pl.BlockSpec((None, r, FOX_W), lambda b, s, pt: (b, 0, ck)),
            pl.BlockSpec((None, r, FOX_W), lambda b, s, pt: (b, 0, cv)),
            pl.BlockSpec((None, r, H_FOX), lambda b, s, pt: (b, 0, 0)),
            pl.BlockSpec((None, H_FOX, r), lambda b, s, pt: (b, 0, 0)),
            pl.BlockSpec((None, None, pg, H_FOX, DH_FOX), lambda b, s, pt: (layer, page(b, s, pt), 0, 0, 0)),
            pl.BlockSpec((None, None, pg, H_FOX, DH_FOX), lambda b, s, pt: (layer, page(b, s, pt), 0, 0, 0)),
            pl.BlockSpec((None, None, pg, H_FOX), lambda b, s, pt: (layer, page(b, s, pt), 0, 0)),
            pl.BlockSpec((1, FOX_W), lambda b, s, pt: (0, 0)),
        ],
        out_specs=pl.BlockSpec((None, r, FOX_W), lambda b, s, pt: (b, 0, 0)),
        scratch_shapes=[
            pltpu.VMEM((H_FOX, r, 1), F32),
            pltpu.VMEM((H_FOX, r, 1), F32),
            pltpu.VMEM((r, FOX_W), F32),
            pltpu.VMEM((H_FOX, 1), F32),
        ],
    )
    return pl.pallas_call(
        kern,
        out_shape=jax.ShapeDtypeStruct((ns, r, FOX_W), F32),
        grid_spec=grid_spec,
        compiler_params=_params("parallel", "arbitrary"),
        name="fox_sample",
    )(page_table.reshape(-1), zs, zs, zs, lfn, lfnt, cache_k, cache_v, cache_lf, norm_g.reshape(1, FOX_W))


def _ret_kernel(*refs, has_s0):
    if has_s0:
        (q_ref, k_ref, v_ref, gt_ref, cos_ref, sin_ref, dm_ref, xi_ref, zt_ref, gc_ref, gn_ref, s0_ref,
         o_ref, s_ref) = refs
    else:
        (q_ref, k_ref, v_ref, gt_ref, cos_ref, sin_ref, dm_ref, xi_ref, zt_ref, gc_ref, gn_ref,
         o_ref, s_ref) = refs

    @pl.when(pl.program_id(2) == 0)
    def _():
        if has_s0:
            s_ref[...] = s0_ref[...]
        else:
            s_ref[...] = jnp.zeros_like(s_ref)

    cos = cos_ref[...]
    sin = sin_ref[...]

    def rot(x):
        return x * cos + pltpu.roll(x, DH_RET // 2, 1) * sin

    q = rot(q_ref[...])
    k = rot(k_ref[...]) * DH_RET ** -0.5
    v = v_ref[...]
    s = s_ref[...]
    att = lax.dot_general(q, k, NT_DIMS, preferred_element_type=F32) * dm_ref[...]
    o = jnp.dot(att, v, preferred_element_type=F32) + jnp.dot(q * xi_ref[...], s, preferred_element_type=F32)
    s_ref[...] = gc_ref[...] * s + lax.dot_general(k * zt_ref[...], v, TN_DIMS, preferred_element_type=F32)
    mu = jnp.mean(o, -1, keepdims=True)
    var = jnp.mean(jnp.square(o - mu), -1, keepdims=True)
    on = (o - mu) * lax.rsqrt(var + EPS)
    gt = gt_ref[...]
    o_ref[...] = on * gn_ref[...] * (gt * _sigmoid(gt))


def _ret_consts(c, c_eff):
    lg = jnp.log1p(-jnp.exp2(-5.0 - jnp.arange(H_RET, dtype=F32)))
    idx = jnp.arange(c, dtype=F32)
    rel = idx[:, None] - idx[None, :]
    dmask = jnp.where(rel >= 0, jnp.exp(lg[:, None, None] * jnp.maximum(rel, 0.0)), 0.0)
    xi = jnp.exp(lg[:, None] * (idx + 1.0))[..., None]
    zeta = jnp.exp(lg[:, None] * (c_eff - 1.0 - idx))[..., None]
    g_c = jnp.broadcast_to(jnp.exp(lg * c_eff)[:, None, None], (H_RET, 1, DH_RET))
    return dmask, xi, zeta, g_c


def _rope_tables(pos):
    inv = ROPE_BASE ** (-jnp.arange(0, DH_RET, 2, dtype=F32) / DH_RET)
    ang = pos.astype(F32)[:, None] * inv[None, :]
    cos, sin = jnp.cos(ang), jnp.sin(ang)
    return jnp.concatenate([cos, cos], -1), jnp.concatenate([-sin, sin], -1)


def _retention_prompt(z, gn_g, nb, seq, t_all):
    c = BLOCK
    nc = seq // c
    dmask, xi, zeta, g_c = _ret_consts(c, c)
    cos, sin = _rope_tables(jnp.arange(seq))

    def zspec(col0):
        return pl.BlockSpec((c, DH_RET), lambda b, h, i: (b * nc + i, col0 + h))

    hspec = lambda shape: pl.BlockSpec((None,) + shape, lambda b, h, i: (h, 0, 0))
    return pl.pallas_call(
        functools.partial(_ret_kernel, has_s0=False),
        out_shape=(jax.ShapeDtypeStruct((t_all, RET_W), F32),
                   jax.ShapeDtypeStruct((nb, H_RET, DH_RET, DH_RET), F32)),
        grid=(nb, H_RET, nc),
        in_specs=[zspec(0), zspec(H_RET), zspec(2 * H_RET), zspec(3 * H_RET),
                  pl.BlockSpec((c, DH_RET), lambda b, h, i: (i, 0)),
                  pl.BlockSpec((c, DH_RET), lambda b, h, i: (i, 0)),
                  hspec((c, c)), hspec((c, 1)), hspec((c, 1)), hspec((1, DH_RET)),
                  pl.BlockSpec((1, DH_RET), lambda b, h, i: (0, h))],
        out_specs=(pl.BlockSpec((c, DH_RET), lambda b, h, i: (b * nc + i, h)),
                   pl.BlockSpec((None, None, DH_RET, DH_RET), lambda b, h, i: (b, h, 0, 0))),
        compiler_params=_params("parallel", "parallel", "arbitrary"),
        name="retention_prompt",
    )(z, z, z, z, cos, sin, dmask, xi, zeta, g_c, gn_g.reshape(1, RET_W))


def _retention_sample(zs, state_ret, layer, gn_g, pos0, n_new):
    ns, r, _ = zs.shape
    dmask, xi, zeta, g_c = _ret_consts(r, n_new)
    cos, sin = _rope_tables(pos0 + jnp.arange(r))

    def zspec(col0):
        return pl.BlockSpec((None, r, DH_RET), lambda b, h, i: (b, 0, col0 + h))

    hspec = lambda shape: pl.BlockSpec((None,) + shape, lambda b, h, i: (h, 0, 0))
    return pl.pallas_call(
        functools.partial(_ret_kernel, has_s0=True),
        out_shape=(jax.ShapeDtypeStruct((ns, r, RET_W), F32),
                   jax.ShapeDtypeStruct((ns, H_RET, DH_RET, DH_RET), F32)),
        grid=(ns, H_RET, 1),
        in_specs=[zspec(0), zspec(H_RET), zspec(2 * H_RET), zspec(3 * H_RET),
                  pl.BlockSpec((r, DH_RET), lambda b, h, i: (0, 0)),
                  pl.BlockSpec((r, DH_RET), lambda b, h, i: (0, 0)),
                  hspec((r, r)), hspec((r, 1)), hspec((r, 1)), hspec((1, DH_RET)),
                  pl.BlockSpec((1, DH_RET), lambda b, h, i: (0, h)),
                  pl.BlockSpec((None, None, None, DH_RET, DH_RET), lambda b, h, i: (layer, b, h, 0, 0))],
        out_specs=(pl.BlockSpec((None, r, DH_RET), lambda b, h, i: (b, 0, h)),
                   pl.BlockSpec((None, None, DH_RET, DH_RET), lambda b, h, i: (b, h, 0, 0))),
        compiler_params=_params("parallel", "parallel", "arbitrary"),
        name="retention_sample",
    )(zs, zs, zs, zs, cos, sin, dmask, xi, zeta, g_c, gn_g.reshape(1, RET_W), state_ret)


def _s5_mats(a_re, a_im, b_re, b_im, c_re, c_im, d_skip, log_dt, c, c_eff):
    lam = lax.complex(a_re, a_im)
    lam_dt = lam * jnp.exp(log_dt)[:, None]
    lam_bar = jnp.exp(lam_dt)
    b_bar = ((lam_bar - 1.0) / lam)[..., None] * lax.complex(b_re, b_im)
    c_mat = lax.complex(c_re, c_im)
    g, p, cg = b_bar.shape

    def power(n):
        return jnp.exp(lam_dt[None] * jnp.asarray(n, F32)[:, None, None])

    lags = jnp.real(jnp.einsum('gop,jgp,gpi->jgoi', c_mat, power(jnp.arange(c)), b_bar, precision=HI))
    sig = jnp.arange(c)[:, None]
    tau = jnp.arange(c)[None, :]
    m = jnp.where((tau >= sig)[:, :, None, None, None], lags[jnp.maximum(tau - sig, 0)], 0.0)
    m = m.transpose(2, 0, 4, 1, 3)
    m = m + (jnp.eye(c)[None, :, None, :, None] * jnp.eye(cg)[None, None, :, None, :]
             * d_skip.reshape(g, 1, cg, 1, 1))
    m = m.reshape(g, c * cg, c * cg)
    w = power(c_eff - 1.0 - jnp.arange(c, dtype=F32))[:, :, :, None] * b_bar[None]
    w = jnp.where((jnp.arange(c) < c_eff)[:, None, None, None], w, 0.0)
    w = w.transpose(1, 0, 3, 2).reshape(g, c * cg, p)
    v = power(jnp.arange(c, dtype=F32) + 1.0)[:, :, None, :] * c_mat[None]
    v = v.transpose(1, 3, 0, 2).reshape(g, p, c * cg)
    a = power(jnp.asarray([c_eff], F32))[0]
    return (m.astype(F32), jnp.real(w), jnp.imag(w), jnp.real(v), -jnp.imag(v), jnp.real(a), jnp.imag(a))


def _s5_prompt_kernel(u_ref, m_ref, wre_ref, wim_ref, vre_ref, vim_ref, are_ref, aim_ref,
                      y_ref, hre_ref, him_ref, ere_sc, eim_sc, hsre_sc, hsim_sc):
    ns, nc, _ = u_ref.shape
    ere = jnp.dot(u_ref[0], wre_ref[0], precision=HI, preferred_element_type=F32)
    eim = jnp.dot(u_ref[0], wim_ref[0], precision=HI, preferred_element_type=F32)
    for s in range(1, ns):
        ere += jnp.dot(u_ref[s], wre_ref[s], precision=HI, preferred_element_type=F32)
        eim += jnp.dot(u_ref[s], wim_ref[s], precision=HI, preferred_element_type=F32)
    ere_sc[...] = ere
    eim_sc[...] = eim
    ar = are_ref[...]
    ai = aim_ref[...]

    def body(k, carry):
        hr, hi = carry
        hsre_sc[pl.ds(k, 1), :] = hr
        hsim_sc[pl.ds(k, 1), :] = hi
        er = ere_sc[pl.ds(k, 1), :]
        ei = eim_sc[pl.ds(k, 1), :]
        return ar * hr - ai * hi + er, ar * hi + ai * hr + ei

    zero = jnp.zeros((1, 2 * SSM_P), F32)
    hr, hi = lax.fori_loop(0, nc, body, (zero, zero))
    hre_ref[...] = hr
    him_ref[...] = hi
    hsre = hsre_sc[...]
    hsim = hsim_sc[...]
    for s in range(ns):
        y_ref[s] = (jnp.dot(u_ref[s], m_ref[...], precision=HI, preferred_element_type=F32)
                    + jnp.dot(hsre, vre_ref[s], precision=HI, preferred_element_type=F32)
                    + jnp.dot(hsim, vim_ref[s], precision=HI, preferred_element_type=F32))


def _s5_prompt(su, mats, nb, seq):
    assert nb == 2
    c = SSM_GROUP
    nc = seq // c
    cw = c * SSM_GROUP
    m, w_re, w_im, v_re, v_im, a_re, a_im = mats
    u = su.reshape(nb, nc, c, SSM_G, SSM_GROUP).transpose(3, 0, 1, 2, 4).reshape(SSM_G, nb, nc, cw)

    def lanes(w):
        return jnp.stack([jnp.pad(w, ((0, 0), (0, 0), (SSM_P * s, SSM_P * (nb - 1 - s)))) for s in range(nb)], 1)

    def rows(v):
        return jnp.stack([jnp.pad(v, ((0, 0), (SSM_P * s, SSM_P * (nb - 1 - s)), (0, 0))) for s in range(nb)], 1)

    gspec = lambda shape: pl.BlockSpec((None,) + shape, lambda g: (g,) + (0,) * len(shape))
    y, hre, him = pl.pallas_call(
        _s5_prompt_kernel,
        out_shape=(jax.ShapeDtypeStruct((SSM_G, nb, nc, cw), F32),
                   jax.ShapeDtypeStruct((SSM_G, 1, nb * SSM_P), F32),
                   jax.ShapeDtypeStruct((SSM_G, 1, nb * SSM_P), F32)),
        grid=(SSM_G,),
        in_specs=[gspec((nb, nc, cw)), gspec((cw, cw)),
                  gspec((nb, cw, nb * SSM_P)), gspec((nb, cw, nb * SSM_P)),
                  gspec((nb, nb * SSM_P, cw)), gspec((nb, nb * SSM_P, cw)),
                  gspec((1, nb * SSM_P)), gspec((1, nb * SSM_P))],
        out_specs=(gspec((nb, nc, cw)), gspec((1, nb * SSM_P)), gspec((1, nb * SSM_P))),
        scratch_shapes=[pltpu.VMEM((nc, nb * SSM_P), F32) for _ in range(4)],
        compiler_params=_params("parallel"),
        name="s5_prompt",
    )(u, m, lanes(w_re), lanes(w_im), rows(v_re), rows(v_im),
      jnp.tile(a_re, (1, nb))[:, None, :], jnp.tile(a_im, (1, nb))[:, None, :])
    y = y.reshape(SSM_G, nb, nc, c, SSM_GROUP).transpose(1, 2, 3, 0, 4).reshape(nb * seq, SSM_W)
    unpack = lambda h: h.reshape(SSM_G, nb, SSM_P).transpose(1, 0, 2)
    return y, unpack(hre), unpack(him)


def _s5_sample_kernel(u_ref, h0re_ref, h0im_ref, m_ref, wre_ref, wim_ref, vre_ref, vim_ref, are_ref, aim_ref,
                      y_ref, hre_ref, him_ref):
    u = u_ref[...]
    hr = h0re_ref[...]
    hi = h0im_ref[...]
    ar = are_ref[...]
    ai = aim_ref[...]
    hre_ref[...] = ar * hr - ai * hi + jnp.dot(u, wre_ref[...], precision=HI, preferred_element_type=F32)
    him_ref[...] = ar * hi + ai * hr + jnp.dot(u, wim_ref[...], precision=HI, preferred_element_type=F32)
    y_ref[...] = (jnp.dot(u, m_ref[...], precision=HI, preferred_element_type=F32)
                  + jnp.dot(hr, vre_ref[...], precision=HI, preferred_element_type=F32)
                  + jnp.dot(hi, vim_ref[...], precision=HI, preferred_element_type=F32))


def _s5_sample(su, h0_re, h0_im, mats, n_new):
    ns, r, _ = su.shape
    cw = r * SSM_GROUP
    m, w_re, w_im, v_re, v_im, a_re, a_im = mats
    u = su.reshape(ns, r, SSM_G, SSM_GROUP).transpose(2, 0, 1, 3).reshape(SSM_G, ns, cw)
    gspec = lambda shape: pl.BlockSpec((None,) + shape, lambda g: (g,) + (0,) * len(shape))
    y, hre, him = pl.pallas_call(
        _s5_sample_kernel,
        out_shape=(jax.ShapeDtypeStruct((SSM_G, ns, cw), F32),
                   jax.ShapeDtypeStruct((SSM_G, ns, SSM_P), F32),
                   jax.ShapeDtypeStruct((SSM_G, ns, SSM_P), F32)),
        grid=(SSM_G,),
        in_specs=[gspec((ns, cw)), gspec((ns, SSM_P)), gspec((ns, SSM_P)), gspec((cw, cw)),
                  gspec((cw, SSM_P)), gspec((cw, SSM_P)), gspec((SSM_P, cw)), gspec((SSM_P, cw)),
                  gspec((1, SSM_P)), gspec((1, SSM_P))],
        out_specs=(gspec((ns, cw)), gspec((ns, SSM_P)), gspec((ns, SSM_P))),
        compiler_params=_params("parallel"),
        name="s5_sample",
    )(u, h0_re.transpose(1, 0, 2), h0_im.transpose(1, 0, 2), m, w_re, w_im, v_re, v_im,
      a_re[:, None, :], a_im[:, None, :])
    y = y.reshape(SSM_G, ns, r, SSM_GROUP)[:, :, :n_new].transpose(1, 2, 0, 3).reshape(ns * n_new, SSM_W)
    return y, hre.transpose(1, 0, 2), him.transpose(1, 0, 2)


def _glu_kernel(y_ref, w_ref, b_ref, g_ref, o_ref):
    sy = _gelu(y_ref[...])
    gate = _sigmoid(jnp.dot(sy.astype(BF16), w_ref[...], preferred_element_type=F32) + b_ref[...])
    o_ref[...] = _rms_norm(sy * gate, g_ref[...])


def _s5_glu(y, w_glu, b_glu, norm_g, tm):
    t = y.shape[0]
    row = lambda: pl.BlockSpec((1, SSM_W), lambda i: (0, 0))
    return pl.pallas_call(
        _glu_kernel,
        out_shape=jax.ShapeDtypeStruct((t, SSM_W), F32),
        grid=(t // tm,),
        in_specs=[pl.BlockSpec((tm, SSM_W), lambda i: (i, 0)),
                  pl.BlockSpec((SSM_W, SSM_W), lambda i: (0, 0)), row(), row()],
        out_specs=pl.BlockSpec((tm, SSM_W), lambda i: (i, 0)),
        compiler_params=_params("parallel"),
        name="s5_glu",
    )(y, w_glu.astype(BF16), b_glu.reshape(1, SSM_W), norm_g.reshape(1, SSM_W))


def _outproj_kernel(ro_ref, fo_ref, so_ref, x_ref, wr_ref, wf_ref, ws_ref, g_ref, b_ref, o_ref):
    mix = jnp.dot(ro_ref[...].astype(BF16), wr_ref[...], preferred_element_type=F32)
    mix += jnp.dot(fo_ref[...].astype(BF16), wf_ref[...], preferred_element_type=F32)
    mix += jnp.dot(so_ref[...].astype(BF16), ws_ref[...], preferred_element_type=F32)
    o_ref[...] = _layer_norm(DEEPNORM_ALPHA * x_ref[...] + mix, g_ref[...], b_ref[...])


def _outproj_ln(ro, fo, so, x, w_out, ln_g, ln_b, tm):
    t = x.shape[0]
    w = w_out.astype(BF16)
    tile = lambda n: pl.BlockSpec((tm, n), lambda i: (i, 0))
    full = lambda r, n: pl.BlockSpec((r, n), lambda i: (0, 0))
    return pl.pallas_call(
        _outproj_kernel,
        out_shape=jax.ShapeDtypeStruct((t, D_MODEL), F32),
        grid=(t // tm,),
        in_specs=[tile(RET_W), tile(FOX_W), tile(SSM_W), tile(D_MODEL),
                  full(RET_W, D_MODEL), full(FOX_W, D_MODEL), full(SSM_W, D_MODEL),
                  full(1, D_MODEL), full(1, D_MODEL)],
        out_specs=tile(D_MODEL),
        compiler_params=_params("parallel"),
        name="outproj_ln",
    )(ro, fo, so, x, w[:RET_W], w[RET_W:RET_W + FOX_W], w[RET_W + FOX_W:],
      ln_g.reshape(1, D_MODEL), ln_b.reshape(1, D_MODEL))


def _topk_rows(s_ref, val_ref, idx_ref, k):
    n = s_ref.shape[0]
    iota = lax.broadcasted_iota(jnp.int32, s_ref.shape, 0)

    def body(it, _):
        s = s_ref[...]
        m = jnp.max(s, axis=0, keepdims=True)
        idx = jnp.min(jnp.where(s == m, iota, n), axis=0, keepdims=True)
        val_ref[pl.ds(it, 1), :] = m
        idx_ref[pl.ds(it, 1), :] = idx
        s_ref[...] = jnp.where(iota == idx, -jnp.inf, s)
        return 0

    lax.fori_loop(0, k, body, 0)


def _peer_topk_kernel(q_ref, sk_ref, i_ref, j_ref, gate_ref, s0_sc, s1_sc, cand_sc, v0_sc, v1_sc, x0_sc, x1_sc,
                      top_sc, sel_sc):
    kk = PEER_TOPK
    s = lax.dot_general(sk_ref[...], q_ref[...], NT_DIMS, precision=HI, preferred_element_type=F32)
    s0_sc[...] = s[:PEER_NKEYS]
    s1_sc[...] = s[PEER_NKEYS:]
    _topk_rows(s0_sc, v0_sc, x0_sc, kk)
    _topk_rows(s1_sc, v1_sc, x1_sc, kk)
    v1 = v1_sc[...]
    for a in range(kk):
        cand_sc[a * kk:(a + 1) * kk, :] = v0_sc[a:a + 1, :] + v1
    _topk_rows(cand_sc, top_sc, sel_sc, kk)
    sel = sel_sc[...]
    a_idx = sel // kk
    b_idx = sel - a_idx * kk
    ii = jnp.zeros_like(sel)
    jj = jnp.zeros_like(sel)
    for a in range(kk):
        ii = jnp.where(a_idx == a, x0_sc[a:a + 1, :], ii)
        jj = jnp.where(b_idx == a, x1_sc[a:a + 1, :], jj)
    top = top_sc[...]
    e = jnp.exp(top - top[0:1, :])
    i_ref[...] = ii
    j_ref[...] = jj
    gate_ref[...] = e / jnp.sum(e, axis=0, keepdims=True)


def _peer_topk(q, sub_keys, tm):
    t = q.shape[0]
    half = PEER_DK // 2
    sk = jnp.zeros((PEER_HEADS, 2 * PEER_NKEYS, PEER_DK), F32)
    sk = sk.at[:, :PEER_NKEYS, :half].set(sub_keys[:, 0]).at[:, PEER_NKEYS:, half:].set(sub_keys[:, 1])
    kk = PEER_TOPK
    slot = lambda: pl.BlockSpec((kk, tm), lambda i, h: (h, i))
    return pl.pallas_call(
        _peer_topk_kernel,
        out_shape=(jax.ShapeDtypeStruct((PEER_HEADS * kk, t), jnp.int32),
                   jax.ShapeDtypeStruct((PEER_HEADS * kk, t), jnp.int32),
                   jax.ShapeDtypeStruct((PEER_HEADS * kk, t), F32)),
        grid=(t // tm, PEER_HEADS),
        in_specs=[pl.BlockSpec((tm, PEER_DK), lambda i, h: (i, h)),
                  pl.BlockSpec((None, 2 * PEER_NKEYS, PEER_DK), lambda i, h: (h, 0, 0))],
        out_specs=(slot(), slot(), slot()),
        scratch_shapes=[pltpu.VMEM((PEER_NKEYS, tm), F32), pltpu.VMEM((PEER_NKEYS, tm), F32),
                        pltpu.VMEM((kk * kk, tm), F32),
                        pltpu.VMEM((kk, tm), F32), pltpu.VMEM((kk, tm), F32),
                        pltpu.VMEM((kk, tm), jnp.int32), pltpu.VMEM((kk, tm), jnp.int32),
                        pltpu.VMEM((kk, tm), F32), pltpu.VMEM((kk, tm), jnp.int32)],
        compiler_params=_params("parallel", "arbitrary"),
        name="peer_topk",
    )(q, sk)


G_ROW_STRIDE = PEER_NKEYS + 8


def _peer_gates_kernel(i_ref, j_ref, gate_ref, o_ref, g_sc):
    tb, ns = i_ref.shape
    nk = PEER_NKEYS
    io = lax.broadcasted_iota(jnp.int32, (tb, nk, ns), 1)
    pt = jnp.where(i_ref[...][:, None, :] == io, gate_ref[...][:, None, :], 0.0).astype(BF16)
    qt = jnp.where(j_ref[...][:, None, :] == io, 1.0, 0.0).astype(BF16)
    g_sc[:, :nk, :] = jnp.einsum('tik,tjk->tij', pt, qt, preferred_element_type=F32)
    for i in range(nk):
        o_ref[:, i * nk:(i + 1) * nk] = g_sc[:, i, :].astype(BF16)


def _peer_gates(ii, jj, gate, tb):
    t = ii.shape[0]
    ns = PEER_HEADS * PEER_TOPK
    tok = lambda: pl.BlockSpec((tb, ns), lambda i: (i, 0))
    return pl.pallas_call(
        _peer_gates_kernel,
        out_shape=jax.ShapeDtypeStruct((t, PEER_N), BF16),
        grid=(t // tb,),
        in_specs=[tok(), tok(), tok()],
        out_specs=pl.BlockSpec((tb, PEER_N), lambda i: (i, 0)),
        scratch_shapes=[pltpu.VMEM((tb, G_ROW_STRIDE, PEER_NKEYS), F32)],
        compiler_params=_params("parallel"),
        name="peer_gates",
    )(ii, jj, gate)


def _peer_experts_kernel(x_ref, g_ref, u_ref, v_ref, lg_ref, lb_ref, o_ref, xb_sc, acc_sc):
    e = pl.program_id(1)

    @pl.when(e == 0)
    def _():
        xb_sc[...] = x_ref[...].astype(BF16)
        acc_sc[...] = jnp.zeros_like(acc_sc)

    h = lax.dot_general(xb_sc[...], u_ref[...], NT_DIMS, preferred_element_type=F32)
    w = (g_ref[...].astype(F32) * _gelu(h)).astype(BF16)
    acc_sc[...] += jnp.dot(w, v_ref[...], preferred_element_type=F32)

    @pl.when(e == pl.num_programs(1) - 1)
    def _():
        o_ref[...] = _layer_norm(DEEPNORM_ALPHA * x_ref[...] + acc_sc[...], lg_ref[...], lb_ref[...])


def _peer_experts_ln(x, g, u_tab, v_tab, ln_g, ln_b, tm, nb):
    t = x.shape[0]
    return pl.pallas_call(
        _peer_experts_kernel,
        out_shape=jax.ShapeDtypeStruct((t, D_MODEL), F32),
        grid=(t // tm, PEER_N // nb),
        in_specs=[pl.BlockSpec((tm, D_MODEL), lambda i, e: (i, 0)),
                  pl.BlockSpec((tm, nb), lambda i, e: (i, e)),
                  pl.BlockSpec((nb, D_MODEL), lambda i, e: (e, 0)),
                  pl.BlockSpec((nb, D_MODEL), lambda i, e: (e, 0)),
                  pl.BlockSpec((1, D_MODEL), lambda i, e: (0, 0)),
                  pl.BlockSpec((1, D_MODEL), lambda i, e: (0, 0))],
        out_specs=pl.BlockSpec((tm, D_MODEL), lambda i, e: (i, 0)),
        scratch_shapes=[pltpu.VMEM((tm, D_MODEL), BF16), pltpu.VMEM((tm, D_MODEL), F32)],
        compiler_params=_params("parallel", "arbitrary"),
        name="peer_experts_ln",
    )(x, g, u_tab, v_tab, ln_g.reshape(1, D_MODEL), ln_b.reshape(1, D_MODEL))


def _ple_kernel(x_ref, xn_ref, p_ref, wg_ref, bg_ref, we_ref, o_ref):
    gate = _sigmoid(jnp.dot(x_ref[...].astype(BF16), wg_ref[...], preferred_element_type=F32) + bg_ref[...])
    emb = jnp.dot(p_ref[...].astype(BF16), we_ref[...], preferred_element_type=F32)
    o_ref[...] = xn_ref[...] + gate * emb


def _ple(x, p, w_pg, b_pg, w_pe, tm, tn):
    t = x.shape[0]
    return pl.pallas_call(
        _ple_kernel,
        out_shape=jax.ShapeDtypeStruct((t, D_MODEL), F32),
        grid=(t // tm, D_MODEL // tn),
        in_specs=[pl.BlockSpec((tm, D_MODEL), lambda i, j: (i, 0)),
                  pl.BlockSpec((tm, tn), lambda i, j: (i, j)),
                  pl.BlockSpec((tm, PLE_DIM), lambda i, j: (i, 0)),
                  pl.BlockSpec((D_MODEL, tn), lambda i, j: (0, j)),
                  pl.BlockSpec((1, tn), lambda i, j: (0, j)),
                  pl.BlockSpec((PLE_DIM, tn), lambda i, j: (0, j))],
        out_specs=pl.BlockSpec((tm, tn), lambda i, j: (i, j)),
        compiler_params=_params("parallel", "arbitrary"),
        name="ple_gate",
    )(x, x, p, w_pg.astype(BF16), b_pg.reshape(1, D_MODEL), w_pe.astype(BF16))


def _split_bf16(w):
    hi = w.astype(BF16)
    return hi, (w - hi.astype(F32)).astype(BF16)


def _layer(x, p, lw, caches, page_table, layer, nb, seq, ns, n_new, tiles):
    t_all = x.shape[0]
    tp = nb * seq
    r = 8
    cache_k, cache_v, cache_lf, state_ret, state_re, state_im = caches
    past_len = page_table.shape[1] * cache_k.shape[2]

    widths = [RET_W] * 4 + [FOX_W] * 3 + [H_FOX, SSM_W]
    offs = [0]
    for wd in widths:
        offs.append(offs[-1] + wd)
    w_in = lw['w_in']
    w_pack = jnp.concatenate([w_in[:, :offs[7]], w_in[:, offs[8]:]], axis=1).astype(BF16)
    w_ff = w_in[:, offs[7]:offs[8]]

    z = _matmul(x, (w_pack,), tiles['tm'], 512, "in_proj")
    lfc, lfr = _forget_gates(x, w_ff, lw['b_fox_f'], tiles['tm'])
    zs = jnp.pad(z[tp:].reshape(ns, n_new, Z_W), ((0, 0), (0, r - n_new), (0, 0)))

    ro, ret_p = _retention_prompt(z, lw['ret_gn_g'], nb, seq, t_all)
    ro_s, ret_s = _retention_sample(zs, state_ret, layer, lw['ret_gn_g'], past_len, n_new)
    ro = ro.at[tp:].set(ro_s[:, :n_new].reshape(ns * n_new, RET_W))

    cc, cr = _forget_cumsum(lfc, lfr, nb, seq)
    fo = _fox_prompt(z, cc, cr, lw['fox_norm_g'], nb, seq, t_all, tiles['attn'])
    lf_s = lfc[tp:, :H_FOX].reshape(ns, n_new, H_FOX)
    lfn = jnp.pad(lf_s, ((0, 0), (0, r - n_new), (0, 0)))
    fo_s = _fox_sample(zs, lfn, lfn.transpose(0, 2, 1), cache_k, cache_v, cache_lf, page_table, layer,
                       lw['fox_norm_g'], n_new)
    fo = fo.at[tp:].set(fo_s[:, :n_new].reshape(ns * n_new, FOX_W))

    ssm = (lw['ssm_a_re'], lw['ssm_a_im'], lw['ssm_b_re'], lw['ssm_b_im'], lw['ssm_c_re'], lw['ssm_c_im'],
           lw['ssm_d'], lw['ssm_log_dt'])
    y_p, hre_p, him_p = _s5_prompt(z[:tp, Z_SU:], _s5_mats(*ssm, SSM_GROUP, SSM_GROUP), nb, seq)
    y_s, hre_s, him_s = _s5_sample(zs[:, :, Z_SU:], state_re[layer], state_im[layer], _s5_mats(*ssm, r, n_new), n_new)
    so = _s5_glu(jnp.concatenate([y_p, y_s], 0), lw['w_glu'], lw['b_glu'], lw['ssm_norm_g'], tiles['tm'])

    x1 = _outproj_ln(ro, fo, so, x, lw['w_out'], lw['ln1_g'], lw['ln1_b'], tiles['tm_ln'])

    q = _matmul(x1, _split_bf16(lw['peer_w_q']), tiles['tm'], 512, "peer_query")
    ii, jj, gate = _peer_topk(q, lw['peer_sub_keys'], tiles['tm'])
    g = _peer_gates(ii.T, jj.T, gate.T, 128)
    x2 = _peer_experts_ln(x1, g, lw['peer_u'].astype(BF16), lw['peer_v'].astype(BF16),
                          lw['ln2_g'], lw['ln2_b'], tiles['tm'], 512)

    x3 = _ple(x2, p, lw['w_pg'], lw['b_pg'], lw['w_pe'], tiles['tm'], 512)

    st_p = (z[:tp, Z_FK:Z_FK + FOX_W].reshape(nb, seq, H_FOX, DH_FOX),
            z[:tp, Z_FV:Z_FV + FOX_W].reshape(nb, seq, H_FOX, DH_FOX),
            lfc[:tp, :H_FOX].reshape(nb, seq, H_FOX), ret_p, hre_p, him_p)
    st_s = (z[tp:, Z_FK:Z_FK + FOX_W].reshape(ns, n_new, H_FOX, DH_FOX),
            z[tp:, Z_FV:Z_FV + FOX_W].reshape(ns, n_new, H_FOX, DH_FOX),
            lf_s, ret_s, hre_s, him_s)
    return x3, st_p, st_s


def kernel(x_prompt, x_sample, p_prompt, p_sample, cache_fox_k, cache_fox_v, cache_fox_lf, state_ret, state_ssm_re, state_ssm_im, page_table, w_in, w_out, ret_gn_g, b_fox_f, fox_norm_g, ssm_a_re, ssm_a_im, ssm_b_re, ssm_b_im, ssm_c_re, ssm_c_im, ssm_d, ssm_log_dt, w_glu, b_glu, ssm_norm_g, ln1_g, ln1_b, ln2_g, ln2_b, peer_w_q, peer_sub_keys, peer_u, peer_v, w_pe, w_pg, b_pg):
    nb, seq, d = x_prompt.shape
    ns, n_new, _ = x_sample.shape
    tp, ts = nb * seq, ns * n_new
    t_all = tp + ts
    tm = 640 if t_all % 640 == 0 else 128
    tiles = dict(tm=tm, tm_ln=tm // 2 if tm == 640 else tm, attn=256 if seq % 256 == 0 else 128)
    x = jnp.concatenate([x_prompt.reshape(tp, d), x_sample.reshape(ts, d)], 0)
    caches = (cache_fox_k, cache_fox_v, cache_fox_lf, state_ret, state_ssm_re, state_ssm_im)
    weights = dict(w_in=w_in, w_out=w_out, ret_gn_g=ret_gn_g, b_fox_f=b_fox_f, fox_norm_g=fox_norm_g,
                   ssm_a_re=ssm_a_re, ssm_a_im=ssm_a_im, ssm_b_re=ssm_b_re, ssm_b_im=ssm_b_im,
                   ssm_c_re=ssm_c_re, ssm_c_im=ssm_c_im, ssm_d=ssm_d, ssm_log_dt=ssm_log_dt,
                   w_glu=w_glu, b_glu=b_glu, ssm_norm_g=ssm_norm_g, ln1_g=ln1_g, ln1_b=ln1_b,
                   ln2_g=ln2_g, ln2_b=ln2_b, peer_w_q=peer_w_q, peer_sub_keys=peer_sub_keys,
                   peer_u=peer_u, peer_v=peer_v, w_pe=w_pe, w_pg=w_pg, b_pg=b_pg)
    sts_p, sts_s = [], []
    for l in range(DEPTH):
        lw = {k: v[l] for k, v in weights.items()}
        p = jnp.concatenate([p_prompt[l].reshape(tp, PLE_DIM), p_sample[l].reshape(ts, PLE_DIM)], 0)
        x, st_p, st_s = _layer(x, p, lw, caches, page_table, l, nb, seq, ns, n_new, tiles)
        sts_p.append(st_p)
        sts_s.append(st_s)
    stack = lambda sts, i: jnp.stack([s[i] for s in sts])
    return ((x[:tp].reshape(nb, seq, d), x[tp:].reshape(ns, n_new, d))
            + tuple(stack(sts_p, i) for i in range(6)) + tuple(stack(sts_s, i) for i in range(6)))
```

```python
import functools
import math

import jax
import jax.numpy as jnp
from jax import lax
from jax.experimental import pallas as pl
from jax.experimental.pallas import tpu as pltpu

F32 = jnp.float32
BF16 = jnp.bfloat16
HI = lax.Precision.HIGHEST

D_MODEL = 2048
DEPTH = 2
H_RET, DH_RET = 4, 128
RET_W = H_RET * DH_RET
H_FOX, DH_FOX = 8, 128
FOX_W = H_FOX * DH_FOX
SSM_W, SSM_GROUP = 512, 16
SSM_G = SSM_W // SSM_GROUP
SSM_P = 64
PEER_HEADS, PEER_NKEYS, PEER_DK, PEER_TOPK = 8, 128, 128, 16
PEER_N = PEER_NKEYS * PEER_NKEYS
PLE_DIM = 256
BLOCK = 128
ROPE_BASE = 10000.0
DEEPNORM_ALPHA = (2 * DEPTH) ** 0.25
EPS = 1e-5

Z_W = 4 * RET_W + 3 * FOX_W + SSM_W
Z_FQ, Z_FK, Z_FV, Z_SU = 4 * RET_W, 4 * RET_W + FOX_W, 4 * RET_W + 2 * FOX_W, 4 * RET_W + 3 * FOX_W

VMEM_LIMIT = 56 * 1024 * 1024
NT_DIMS = (((1,), (1,)), ((), ()))
TN_DIMS = (((0,), (0,)), ((), ()))


def _params(*sem):
    return pltpu.CompilerParams(dimension_semantics=sem, vmem_limit_bytes=VMEM_LIMIT)


def _gelu(x):
    return 0.5 * x * (1.0 + jnp.tanh(math.sqrt(2.0 / math.pi) * (x + 0.044715 * (x * x * x))))


def _sigmoid(x):
    return 1.0 / (1.0 + jnp.exp(-x))


def _log_sigmoid(x):
    return jnp.minimum(x, 0.0) - jnp.log1p(jnp.exp(-jnp.abs(x)))


def _layer_norm(y, g, b):
    mu = jnp.mean(y, -1, keepdims=True)
    var = jnp.mean(jnp.square(y - mu), -1, keepdims=True)
    return (y - mu) * lax.rsqrt(var + EPS) * g + b


def _rms_norm(y, g):
    return y * lax.rsqrt(jnp.mean(y * y, -1, keepdims=True) + EPS) * g


def _mm_kernel(x_ref, w_ref, o_ref):
    o_ref[...] = jnp.dot(x_ref[...].astype(BF16), w_ref[...], preferred_element_type=F32)


def _mm3_kernel(x_ref, wh_ref, wl_ref, o_ref):
    x = x_ref[...]
    xh = x.astype(BF16)
    xl = (x - xh.astype(F32)).astype(BF16)
    wh = wh_ref[...]
    acc = jnp.dot(xh, wh, preferred_element_type=F32)
    acc += jnp.dot(xl, wh, preferred_element_type=F32)
    acc += jnp.dot(xh, wl_ref[...], preferred_element_type=F32)
    o_ref[...] = acc


def _matmul(x, ws, tm, tn, name):
    t, k = x.shape
    n = ws[0].shape[1]
    kern = _mm_kernel if len(ws) == 1 else _mm3_kernel
    return pl.pallas_call(
        kern,
        out_shape=jax.ShapeDtypeStruct((t, n), F32),
        grid=(t // tm, n // tn),
        in_specs=[pl.BlockSpec((tm, k), lambda i, j: (i, 0))]
        + [pl.BlockSpec((k, tn), lambda i, j: (0, j)) for _ in ws],
        out_specs=pl.BlockSpec((tm, tn), lambda i, j: (i, j)),
        compiler_params=_params("parallel", "arbitrary"),
        name=name,
    )(x, *ws)


def _gate_kernel(x_ref, w_ref, b_ref, lf_ref):
    f = lax.dot_general(w_ref[...], x_ref[...], NT_DIMS, precision=HI, preferred_element_type=F32)
    lf_ref[...] = _log_sigmoid(f + b_ref[...])


def _forget_gates(x, w_ff, b_ff, tm):
    t, k = x.shape
    return pl.pallas_call(
        _gate_kernel,
        out_shape=jax.ShapeDtypeStruct((H_FOX, t), F32),
        grid=(t // tm,),
        in_specs=[
            pl.BlockSpec((tm, k), lambda i: (i, 0)),
            pl.BlockSpec((H_FOX, k), lambda i: (0, 0)),
            pl.BlockSpec((H_FOX, 1), lambda i: (0, 0)),
        ],
        out_specs=pl.BlockSpec((H_FOX, tm), lambda i: (0, i)),
        compiler_params=_params("parallel"),
        name="forget_gates",
    )(x, w_ff.T, b_ff.reshape(H_FOX, 1))


def _cumsum_kernel(lf_ref, c_ref, carry):
    @pl.when(pl.program_id(1) == 0)
    def _():
        carry[...] = jnp.zeros_like(carry)

    n = lf_ref.shape[1]
    r = lax.broadcasted_iota(jnp.int32, (n, n), 0)
    c = lax.broadcasted_iota(jnp.int32, (n, n), 1)
    upper = (r <= c).astype(F32)
    cs = jnp.dot(lf_ref[...], upper, precision=HI, preferred_element_type=F32) + carry[...]
    c_ref[...] = cs
    carry[...] = cs[:, n - 1:n]


def _forget_cumsum(lf, nb, seq):
    n = 256 if seq % 256 == 0 else 128
    nc = seq // n
    return pl.pallas_call(
        _cumsum_kernel,
        out_shape=jax.ShapeDtypeStruct((H_FOX, nb * seq), F32),
        grid=(nb, nc),
        in_specs=[pl.BlockSpec((H_FOX, n), lambda b, i: (0, b * nc + i))],
        out_specs=pl.BlockSpec((H_FOX, n), lambda b, i: (0, b * nc + i)),
        scratch_shapes=[pltpu.VMEM((H_FOX, 1), F32)],
        compiler_params=_params("parallel", "arbitrary"),
        name="forget_cumsum",
    )(lf)


def _fox_prompt_kernel(q_ref, k_ref, v_ref, c_ref, g_ref, o_ref, m_sc, l_sc, acc_sc, *, scale):
    qi = pl.program_id(1)
    ki = pl.program_id(2)
    bq = q_ref.shape[0]
    bk = k_ref.shape[0]

    @pl.when(ki == 0)
    def _():
        m_sc[...] = jnp.full_like(m_sc, -jnp.inf)
        l_sc[...] = jnp.zeros_like(l_sc)
        acc_sc[...] = jnp.zeros_like(acc_sc)

    def attend(diagonal):
        if diagonal:
            causal = (lax.broadcasted_iota(jnp.int32, (bq, bk), 1)
                      <= lax.broadcasted_iota(jnp.int32, (bq, bk), 0))
        for h in range(H_FOX):
            sl = slice(h * DH_FOX, (h + 1) * DH_FOX)
            q = q_ref[:, sl].astype(BF16)
            k = k_ref[:, sl].astype(BF16)
            s = lax.dot_general(q, k, NT_DIMS, preferred_element_type=F32) * scale - c_ref[h:h + 1, :]
            if diagonal:
                s = jnp.where(causal, s, -jnp.inf)
            m_prev = m_sc[h]
            m_new = jnp.maximum(m_prev, jnp.max(s, axis=1, keepdims=True))
            alpha = jnp.exp(m_prev - m_new)
            p = jnp.exp(s - pltpu.repeat(m_new, bk // 128, axis=1))
            l_sc[h] = alpha * l_sc[h] + jnp.sum(p, axis=1, keepdims=True)
            acc_sc[:, sl] = alpha * acc_sc[:, sl] + jnp.dot(
                p.astype(BF16), v_ref[:, sl].astype(BF16), preferred_element_type=F32)
            m_sc[h] = m_new

    @pl.when(ki < qi)
    def _():
        attend(False)

    @pl.when(ki == qi)
    def _():
        attend(True)
        for h in range(H_FOX):
            sl = slice(h * DH_FOX, (h + 1) * DH_FOX)
            acc_sc[:, sl] = acc_sc[:, sl] / l_sc[h]
        o_ref[...] = _rms_norm(acc_sc[...], g_ref[...])


def _fox_prompt(z, c, norm_g, nb, seq, t_all, blk):
    nq = seq // blk
    cq, ck, cv = Z_FQ // FOX_W, Z_FK // FOX_W, Z_FV // FOX_W
    kern = functools.partial(_fox_prompt_kernel, scale=DH_FOX ** -0.5)
    return pl.pallas_call(
        kern,
        out_shape=jax.ShapeDtypeStruct((t_all, FOX_W), F32),
        grid=(nb, nq, nq),
        in_specs=[
            pl.BlockSpec((blk, FOX_W), lambda b, i, j: (b * nq + i, cq)),
            pl.BlockSpec((blk, FOX_W), lambda b, i, j: (b * nq + jnp.minimum(i, j), ck)),
            pl.BlockSpec((blk, FOX_W), lambda b, i, j: (b * nq + jnp.minimum(i, j), cv)),
            pl.BlockSpec((H_FOX, blk), lambda b, i, j: (0, b * nq + jnp.minimum(i, j))),
            pl.BlockSpec((1, FOX_W), lambda b, i, j: (0, 0)),
        ],
        out_specs=pl.BlockSpec((blk, FOX_W), lambda b, i, j: (b * nq + i, 0)),
        scratch_shapes=[
            pltpu.VMEM((H_FOX, blk, 128), F32),
            pltpu.VMEM((H_FOX, blk, 128), F32),
            pltpu.VMEM((blk, FOX_W), F32),
        ],
        compiler_params=_params("parallel", "parallel", "arbitrary"),
        name="fox_prompt",
    )(z, z, z, c, norm_g.reshape(1, FOX_W))


def _lf_prep_kernel(lf_ref, o_ref, x_sc):
    pb, pg, nh = lf_ref.shape
    for p in range(pb):
        x_sc[:, p * nh:(p + 1) * nh] = lf_ref[p]
    r = lax.broadcasted_iota(jnp.int32, (pg + 8, pg), 0)
    c = lax.broadcasted_iota(jnp.int32, (pg + 8, pg), 1)
    later_or_all = jnp.where(r >= pg, 1.0, jnp.where(c > r, 1.0, 0.0))
    o_ref[...] = jnp.dot(later_or_all, x_sc[...], precision=HI, preferred_element_type=F32)


def _lf_prep(cache_lf):
    depth, n_pool, pg, nh = cache_lf.shape
    pb = max(d for d in range(1, 17) if n_pool % d == 0)
    nblk = n_pool // pb
    out = pl.pallas_call(
        _lf_prep_kernel,
        out_shape=jax.ShapeDtypeStruct((depth, nblk, pg + 8, pb * nh), F32),
        grid=(depth, nblk),
        in_specs=[pl.BlockSpec((None, pb, pg, nh), lambda l, i: (l, i, 0, 0))],
        out_specs=pl.BlockSpec((None, None, pg + 8, pb * nh), lambda l, i: (l, i, 0, 0)),
        scratch_shapes=[pltpu.VMEM((pg, pb * nh), F32)],
        compiler_params=_params("parallel", "parallel"),
        name="lf_prep",
    )(cache_lf)
    suffix = out[:, :, :pg].reshape(depth, nblk, pg, pb, nh).transpose(0, 1, 3, 2, 4)
    suffix = suffix.reshape(depth, n_pool, 1, pg * nh)
    total = jnp.tile(out[:, :, pg].reshape(depth, n_pool, 1, nh), (1, 1, 1, pg))
    return jnp.concatenate([suffix, total], axis=2)


def _fox_sample_kernel(pt_ref, q_ref, kn_ref, vn_ref, lfn_ref, g_ref, *rest, scale, npg):
    del pt_ref
    k_refs, v_refs, x_refs = rest[:npg], rest[npg:2 * npg], rest[2 * npg:3 * npg]
    o_ref, m_sc, l_sc, acc_sc, carry_sc = rest[3 * npg:]
    step = pl.program_id(1)
    nr = q_ref.shape[0]
    head_mask = H_FOX - 1
    head_bits = H_FOX.bit_length() - 1

    @pl.when(step == 0)
    def _():
        m_sc[...] = jnp.full_like(m_sc, -jnp.inf)
        l_sc[...] = jnp.zeros_like(l_sc)
        acc_sc[...] = jnp.zeros_like(acc_sc)
        carry_sc[...] = jnp.zeros_like(carry_sc)

    def equal_heads(ncol):
        rows = lax.broadcasted_iota(jnp.int32, (nr, ncol), 0)
        cols = lax.broadcasted_iota(jnp.int32, (nr, ncol), 1)
        return rows, cols, (rows & head_mask) == (cols & head_mask)

    q = q_ref[...].astype(BF16)
    _, _, same_head = equal_heads(k_refs[0].shape[0])
    carry = carry_sc[...]
    m_prev = m_sc[...]
    m_new = m_prev
    scores = []
    for i in range(npg):
        x = x_refs[i][...]
        bias = x[0:1] + carry
        carry = carry + x[1:2]
        s = lax.dot_general(q, k_refs[i][...].astype(BF16), NT_DIMS, preferred_element_type=F32) * scale + bias
        s = jnp.where(same_head, s, -jnp.inf)
        m_new = jnp.maximum(m_new, jnp.max(s, axis=1, keepdims=True))
        scores.append(s)
    carry_sc[...] = carry
    alpha = jnp.exp(m_prev - m_new)
    l_new = alpha * l_sc[...]
    acc = alpha * acc_sc[...]
    for i in range(npg):
        p = jnp.exp(scores[i] - m_new)
        l_new += jnp.sum(p, axis=1, keepdims=True)
        acc += jnp.dot(p.astype(BF16), v_refs[i][...].astype(BF16), preferred_element_type=F32)
    m_sc[...] = m_new
    l_sc[...] = l_new
    acc_sc[...] = acc

    @pl.when(step == pl.num_programs(1) - 1)
    def _():
        nc = kn_ref.shape[0]
        rows, cols, same = equal_heads(nc)
        causal = (cols >> head_bits) <= (rows >> head_bits)
        r2 = lax.broadcasted_iota(jnp.int32, (nc, nc), 0)
        c2 = lax.broadcasted_iota(jnp.int32, (nc, nc), 1)
        prefix = jnp.where((r2 & head_mask) == (c2 & head_mask), jnp.where(r2 <= c2, 1.0, 0.0), 0.0)
        cn = jnp.dot(jnp.broadcast_to(lfn_ref[...], (8, nc)), prefix, precision=HI,
                     preferred_element_type=F32)[0:1]
        s = lax.dot_general(q, kn_ref[...].astype(BF16), NT_DIMS, preferred_element_type=F32) * scale - cn
        s = jnp.where(same, jnp.where(causal, s, -jnp.inf), -jnp.inf)
        m_fin = jnp.maximum(m_new, jnp.max(s, axis=1, keepdims=True))
        a_fin = jnp.exp(m_new - m_fin)
        p = jnp.exp(s - m_fin)
        l_fin = a_fin * l_new + jnp.sum(p, axis=1, keepdims=True)
        out = (a_fin * acc + jnp.dot(p.astype(BF16), vn_ref[...].astype(BF16),
                                     preferred_element_type=F32)) / l_fin
        r3 = lax.broadcasted_iota(jnp.int32, (nr, nr), 0)
        c3 = lax.broadcasted_iota(jnp.int32, (nr, nr), 1)
        same_token = jnp.where((r3 >> head_bits) == (c3 >> head_bits), 1.0, 0.0)
        ssq = jnp.sum(jnp.dot(same_token, out * out, precision=HI, preferred_element_type=F32),
                      axis=1, keepdims=True)
        o_ref[...] = out * lax.rsqrt(ssq / (H_FOX * DH_FOX) + EPS) * g_ref[...]


def _fox_sample(q, kn, vn, lfn, cache_k, cache_v, lf_sums, page_table, layer, norm_g):
    ns, nr, _ = q.shape
    nc = kn.shape[1]
    n_pages = page_table.shape[1]
    cols = lf_sums.shape[3]
    npg = max(d for d in range(1, 9) if n_pages % d == 0)
    kern = functools.partial(_fox_sample_kernel, scale=DH_FOX ** -0.5, npg=npg)
    g = jnp.tile(norm_g.reshape(H_FOX, DH_FOX), (nr // H_FOX, 1))

    def paged(shape, i):
        def index(b, s, pt):
            return (layer, pt[b * n_pages + (n_pages - 1 - (s * npg + i))], 0, 0)
        return pl.BlockSpec((None, None) + shape, index)

    grid_spec = pltpu.PrefetchScalarGridSpec(
        num_scalar_prefetch=1,
        grid=(ns, n_pages // npg),
        in_specs=[
            pl.BlockSpec((None, nr, DH_FOX), lambda b, s, pt: (b, 0, 0)),
            pl.BlockSpec((None, nc, DH_FOX), lambda b, s, pt: (b, 0, 0)),
            pl.BlockSpec((None, nc, DH_FOX), lambda b, s, pt: (b, 0, 0)),
            pl.BlockSpec((None, 1, nc), lambda b, s, pt: (b, 0, 0)),
            pl.BlockSpec((nr, DH_FOX), lambda b, s, pt: (0, 0)),
        ] + [paged((cols, DH_FOX), i) for i in range(npg)] * 2
          + [paged((2, cols), i) for i in range(npg)],
        out_specs=pl.BlockSpec((None, nr, DH_FOX), lambda b, s, pt: (b, 0, 0)),
        scratch_shapes=[
            pltpu.VMEM((nr, 1), F32),
            pltpu.VMEM((nr, 1), F32),
            pltpu.VMEM((nr, DH_FOX), F32),
            pltpu.VMEM((1, cols), F32),
        ],
    )
    return pl.pallas_call(
        kern,
        out_shape=jax.ShapeDtypeStruct((ns, nr, DH_FOX), F32),
        grid_spec=grid_spec,
        compiler_params=_params("parallel", "arbitrary"),
        name="fox_sample",
    )(page_table.reshape(-1), q, kn, vn, lfn, g,
      *([cache_k] * npg), *([cache_v] * npg), *([lf_sums] * npg))


def _ret_kernel(*refs, has_s0):
    if has_s0:
        (q_ref, k_ref, v_ref, gt_ref, cos_ref, sin_ref, dm_ref, xi_ref, zt_ref, gc_ref, gn_ref, s0_ref,
         o_ref, s_ref) = refs
    else:
        (q_ref, k_ref, v_ref, gt_ref, cos_ref, sin_ref, dm_ref, xi_ref, zt_ref, gc_ref, gn_ref,
         o_ref, s_ref) = refs

    @pl.when(pl.program_id(2) == 0)
    def _():
        if has_s0:
            s_ref[...] = s0_ref[...]
        else:
            s_ref[...] = jnp.zeros_like(s_ref)

    cos = cos_ref[...]
    sin = sin_ref[...]

    def rot(x):
        return x * cos + pltpu.roll(x, DH_RET // 2, 1) * sin

    q = rot(q_ref[...])
    k = rot(k_ref[...]) * DH_RET ** -0.5
    v = v_ref[...]
    s = s_ref[...]
    att = lax.dot_general(q, k, NT_DIMS, preferred_element_type=F32) * dm_ref[...]
    o = jnp.dot(att, v, preferred_element_type=F32) + jnp.dot(q * xi_ref[...], s, preferred_element_type=F32)
    s_ref[...] = gc_ref[...] * s + lax.dot_general(k * zt_ref[...], v, TN_DIMS, preferred_element_type=F32)
    mu = jnp.mean(o, -1, keepdims=True)
    var = jnp.mean(jnp.square(o - mu), -1, keepdims=True)
    on = (o - mu) * lax.rsqrt(var + EPS)
    gt = gt_ref[...]
    o_ref[...] = on * gn_ref[...] * (gt * _sigmoid(gt))


def _ret_consts(c, c_eff):
    lg = jnp.log1p(-jnp.exp2(-5.0 - jnp.arange(H_RET, dtype=F32)))
    idx = jnp.arange(c, dtype=F32)
    rel = idx[:, None] - idx[None, :]
    dmask = jnp.where(rel >= 0, jnp.exp(lg[:, None, None] * jnp.maximum(rel, 0.0)), 0.0)
    xi = jnp.exp(lg[:, None] * (idx + 1.0))[..., None]
    zeta = jnp.exp(lg[:, None] * (c_eff - 1.0 - idx))[..., None]
    g_c = jnp.broadcast_to(jnp.exp(lg * c_eff)[:, None, None], (H_RET, 1, DH_RET))
    return dmask, xi, zeta, g_c


def _rope_tables(pos):
    inv = ROPE_BASE ** (-jnp.arange(0, DH_RET, 2, dtype=F32) / DH_RET)
    ang = pos.astype(F32)[:, None] * inv[None, :]
    cos, sin = jnp.cos(ang), jnp.sin(ang)
    return jnp.concatenate([cos, cos], -1), jnp.concatenate([-sin, sin], -1)


def _retention_prompt(z, gn_g, nb, seq, t_all):
    c = BLOCK
    nc = seq // c
    dmask, xi, zeta, g_c = _ret_consts(c, c)
    cos, sin = _rope_tables(jnp.arange(seq))

    def zspec(col0):
        return pl.BlockSpec((c, DH_RET), lambda b, h, i: (b * nc + i, col0 + h))

    hspec = lambda shape: pl.BlockSpec((None,) + shape, lambda b, h, i: (h, 0, 0))
    return pl.pallas_call(
        functools.partial(_ret_kernel, has_s0=False),
        out_shape=(jax.ShapeDtypeStruct((t_all, RET_W), F32),
                   jax.ShapeDtypeStruct((nb, H_RET, DH_RET, DH_RET), F32)),
        grid=(nb, H_RET, nc),
        in_specs=[zspec(0), zspec(H_RET), zspec(2 * H_RET), zspec(3 * H_RET),
                  pl.BlockSpec((c, DH_RET), lambda b, h, i: (i, 0)),
                  pl.BlockSpec((c, DH_RET), lambda b, h, i: (i, 0)),
                  hspec((c, c)), hspec((c, 1)), hspec((c, 1)), hspec((1, DH_RET)),
                  pl.BlockSpec((1, DH_RET), lambda b, h, i: (0, h))],
        out_specs=(pl.BlockSpec((c, DH_RET), lambda b, h, i: (b * nc + i, h)),
                   pl.BlockSpec((None, None, DH_RET, DH_RET), lambda b, h, i: (b, h, 0, 0))),
        compiler_params=_params("parallel", "parallel", "arbitrary"),
        name="retention_prompt",
    )(z, z, z, z, cos, sin, dmask, xi, zeta, g_c, gn_g.reshape(1, RET_W))


def _retention_sample(zs, state_ret, layer, gn_g, pos0, n_new):
    ns, r, _ = zs.shape
    dmask, xi, zeta, g_c = _ret_consts(r, n_new)
    cos, sin = _rope_tables(pos0 + jnp.arange(r))

    def zspec(col0):
        return pl.BlockSpec((None, r, DH_RET), lambda b, h, i: (b, 0, col0 + h))

    hspec = lambda shape: pl.BlockSpec((None,) + shape, lambda b, h, i: (h, 0, 0))
    return pl.pallas_call(
        functools.partial(_ret_kernel, has_s0=True),
        out_shape=(jax.ShapeDtypeStruct((ns, r, RET_W), F32),
                   jax.ShapeDtypeStruct((ns, H_RET, DH_RET, DH_RET), F32)),
        grid=(ns, H_RET, 1),
        in_specs=[zspec(0), zspec(H_RET), zspec(2 * H_RET), zspec(3 * H_RET),
                  pl.BlockSpec((r, DH_RET), lambda b, h, i: (0, 0)),
                  pl.BlockSpec((r, DH_RET), lambda b, h, i: (0, 0)),
                  hspec((r, r)), hspec((r, 1)), hspec((r, 1)), hspec((1, DH_RET)),
                  pl.BlockSpec((1, DH_RET), lambda b, h, i: (0, h)),
                  pl.BlockSpec((None, None, None, DH_RET, DH_RET), lambda b, h, i: (layer, b, h, 0, 0))],
        out_specs=(pl.BlockSpec((None, r, DH_RET), lambda b, h, i: (b, 0, h)),
                   pl.BlockSpec((None, None, DH_RET, DH_RET), lambda b, h, i: (b, h, 0, 0))),
        compiler_params=_params("parallel", "parallel", "arbitrary"),
        name="retention_sample",
    )(zs, zs, zs, zs, cos, sin, dmask, xi, zeta, g_c, gn_g.reshape(1, RET_W), state_ret)


def _s5_mats(a_re, a_im, b_re, b_im, c_re, c_im, d_skip, log_dt, c, c_eff):
    lam = lax.complex(a_re, a_im)
    lam_dt = lam * jnp.exp(log_dt)[:, None]
    lam_bar = jnp.exp(lam_dt)
    b_bar = ((lam_bar - 1.0) / lam)[..., None] * lax.complex(b_re, b_im)
    c_mat = lax.complex(c_re, c_im)
    g, p, cg = b_bar.shape

    def power(n):
        return jnp.exp(lam_dt[None] * jnp.asarray(n, F32)[:, None, None])

    lags = jnp.real(jnp.einsum('gop,jgp,gpi->jgoi', c_mat, power(jnp.arange(c)), b_bar, precision=HI))
    sig = jnp.arange(c)[:, None]
    tau = jnp.arange(c)[None, :]
    m = jnp.where((tau >= sig)[:, :, None, None, None], lags[jnp.maximum(tau - sig, 0)], 0.0)
    m = m.transpose(2, 0, 4, 1, 3)
    m = m + (jnp.eye(c)[None, :, None, :, None] * jnp.eye(cg)[None, None, :, None, :]
             * d_skip.reshape(g, 1, cg, 1, 1))
    m = m.reshape(g, c * cg, c * cg)
    w = power(c_eff - 1.0 - jnp.arange(c, dtype=F32))[:, :, :, None] * b_bar[None]
    w = jnp.where((jnp.arange(c) < c_eff)[:, None, None, None], w, 0.0)
    w = w.transpose(1, 0, 3, 2).reshape(g, c * cg, p)
    v = power(jnp.arange(c, dtype=F32) + 1.0)[:, :, None, :] * c_mat[None]
    v = v.transpose(1, 3, 0, 2).reshape(g, p, c * cg)
    a = power(jnp.asarray([c_eff], F32))[0]
    return (m.astype(F32), jnp.real(w), jnp.imag(w), jnp.real(v), -jnp.imag(v), jnp.real(a), jnp.imag(a))


def _s5_prompt_kernel(u_ref, m_ref, wre_ref, wim_ref, vre_ref, vim_ref, are_ref, aim_ref,
                      y_ref, hre_ref, him_ref, ere_sc, eim_sc, hsre_sc, hsim_sc):
    ns, nc, _ = u_ref.shape
    ere = jnp.dot(u_ref[0], wre_ref[0], precision=HI, preferred_element_type=F32)
    eim = jnp.dot(u_ref[0], wim_ref[0], precision=HI, preferred_element_type=F32)
    for s in range(1, ns):
        ere += jnp.dot(u_ref[s], wre_ref[s], precision=HI, preferred_element_type=F32)
        eim += jnp.dot(u_ref[s], wim_ref[s], precision=HI, preferred_element_type=F32)
    ere_sc[...] = ere
    eim_sc[...] = eim
    ar = are_ref[...]
    ai = aim_ref[...]

    def body(k, carry):
        hr, hi = carry
        hsre_sc[pl.ds(k, 1), :] = hr
        hsim_sc[pl.ds(k, 1), :] = hi
        er = ere_sc[pl.ds(k, 1), :]
        ei = eim_sc[pl.ds(k, 1), :]
        return ar * hr - ai * hi + er, ar * hi + ai * hr + ei

    zero = jnp.zeros((1, 2 * SSM_P), F32)
    hr, hi = lax.fori_loop(0, nc, body, (zero, zero))
    hre_ref[...] = hr
    him_ref[...] = hi
    hsre = hsre_sc[...]
    hsim = hsim_sc[...]
    for s in range(ns):
        y_ref[s] = (jnp.dot(u_ref[s], m_ref[...], precision=HI, preferred_element_type=F32)
                    + jnp.dot(hsre, vre_ref[s], precision=HI, preferred_element_type=F32)
                    + jnp.dot(hsim, vim_ref[s], precision=HI, preferred_element_type=F32))


def _s5_prompt(su, mats, nb, seq):
    assert nb == 2
    c = SSM_GROUP
    nc = seq // c
    cw = c * SSM_GROUP
    m, w_re, w_im, v_re, v_im, a_re, a_im = mats
    u = su.reshape(nb, nc, c, SSM_G, SSM_GROUP).transpose(3, 0, 1, 2, 4).reshape(SSM_G, nb, nc, cw)

    def lanes(w):
        return jnp.stack([jnp.pad(w, ((0, 0), (0, 0), (SSM_P * s, SSM_P * (nb - 1 - s)))) for s in range(nb)], 1)

    def rows(v):
        return jnp.stack([jnp.pad(v, ((0, 0), (SSM_P * s, SSM_P * (nb - 1 - s)), (0, 0))) for s in range(nb)], 1)

    gspec = lambda shape: pl.BlockSpec((None,) + shape, lambda g: (g,) + (0,) * len(shape))
    y, hre, him = pl.pallas_call(
        _s5_prompt_kernel,
        out_shape=(jax.ShapeDtypeStruct((SSM_G, nb, nc, cw), F32),
                   jax.ShapeDtypeStruct((SSM_G, 1, nb * SSM_P), F32),
                   jax.ShapeDtypeStruct((SSM_G, 1, nb * SSM_P), F32)),
        grid=(SSM_G,),
        in_specs=[gspec((nb, nc, cw)), gspec((cw, cw)),
                  gspec((nb, cw, nb * SSM_P)), gspec((nb, cw, nb * SSM_P)),
                  gspec((nb, nb * SSM_P, cw)), gspec((nb, nb * SSM_P, cw)),
                  gspec((1, nb * SSM_P)), gspec((1, nb * SSM_P))],
        out_specs=(gspec((nb, nc, cw)), gspec((1, nb * SSM_P)), gspec((1, nb * SSM_P))),
        scratch_shapes=[pltpu.VMEM((nc, nb * SSM_P), F32) for _ in range(4)],
        compiler_params=_params("parallel"),
        name="s5_prompt",
    )(u, m, lanes(w_re), lanes(w_im), rows(v_re), rows(v_im),
      jnp.tile(a_re, (1, nb))[:, None, :], jnp.tile(a_im, (1, nb))[:, None, :])
    y = y.reshape(SSM_G, nb, nc, c, SSM_GROUP).transpose(1, 2, 3, 0, 4).reshape(nb * seq, SSM_W)
    unpack = lambda h: h.reshape(SSM_G, nb, SSM_P).transpose(1, 0, 2)
    return y, unpack(hre), unpack(him)


def _s5_sample_kernel(u_ref, h0re_ref, h0im_ref, m_ref, wre_ref, wim_ref, vre_ref, vim_ref, are_ref, aim_ref,
                      y_ref, hre_ref, him_ref):
    u = u_ref[...]
    hr = h0re_ref[...]
    hi = h0im_ref[...]
    ar = are_ref[...]
    ai = aim_ref[...]
    hre_ref[...] = ar * hr - ai * hi + jnp.dot(u, wre_ref[...], precision=HI, preferred_element_type=F32)
    him_ref[...] = ar * hi + ai * hr + jnp.dot(u, wim_ref[...], precision=HI, preferred_element_type=F32)
    y_ref[...] = (jnp.dot(u, m_ref[...], precision=HI, preferred_element_type=F32)
                  + jnp.dot(hr, vre_ref[...], precision=HI, preferred_element_type=F32)
                  + jnp.dot(hi, vim_ref[...], precision=HI, preferred_element_type=F32))


def _s5_sample(su, h0_re, h0_im, mats, n_new):
    ns, r, _ = su.shape
    cw = r * SSM_GROUP
    m, w_re, w_im, v_re, v_im, a_re, a_im = mats
    u = su.reshape(ns, r, SSM_G, SSM_GROUP).transpose(2, 0, 1, 3).reshape(SSM_G, ns, cw)
    gspec = lambda shape: pl.BlockSpec((None,) + shape, lambda g: (g,) + (0,) * len(shape))
    y, hre, him = pl.pallas_call(
        _s5_sample_kernel,
        out_shape=(jax.ShapeDtypeStruct((SSM_G, ns, cw), F32),
                   jax.ShapeDtypeStruct((SSM_G, ns, SSM_P), F32),
                   jax.ShapeDtypeStruct((SSM_G, ns, SSM_P), F32)),
        grid=(SSM_G,),
        in_specs=[gspec((ns, cw)), gspec((ns, SSM_P)), gspec((ns, SSM_P)), gspec((cw, cw)),
                  gspec((cw, SSM_P)), gspec((cw, SSM_P)), gspec((SSM_P, cw)), gspec((SSM_P, cw)),
                  gspec((1, SSM_P)), gspec((1, SSM_P))],
        out_specs=(gspec((ns, cw)), gspec((ns, SSM_P)), gspec((ns, SSM_P))),
        compiler_params=_params("parallel"),
        name="s5_sample",
    )(u, h0_re.transpose(1, 0, 2), h0_im.transpose(1, 0, 2), m, w_re, w_im, v_re, v_im,
      a_re[:, None, :], a_im[:, None, :])
    y = y.reshape(SSM_G, ns, r, SSM_GROUP)[:, :, :n_new].transpose(1, 2, 0, 3).reshape(ns * n_new, SSM_W)
    return y, hre.transpose(1, 0, 2), him.transpose(1, 0, 2)


def _glu_kernel(y_ref, w_ref, b_ref, g_ref, o_ref):
    sy = _gelu(y_ref[...])
    gate = _sigmoid(jnp.dot(sy.astype(BF16), w_ref[...], preferred_element_type=F32) + b_ref[...])
    o_ref[...] = _rms_norm(sy * gate, g_ref[...])


def _s5_glu(y, w_glu, b_glu, norm_g, tm):
    t = y.shape[0]
    row = lambda: pl.BlockSpec((1, SSM_W), lambda i: (0, 0))
    return pl.pallas_call(
        _glu_kernel,
        out_shape=jax.ShapeDtypeStruct((t, SSM_W), F32),
        grid=(t // tm,),
        in_specs=[pl.BlockSpec((tm, SSM_W), lambda i: (i, 0)),
                  pl.BlockSpec((SSM_W, SSM_W), lambda i: (0, 0)), row(), row()],
        out_specs=pl.BlockSpec((tm, SSM_W), lambda i: (i, 0)),
        compiler_params=_params("parallel"),
        name="s5_glu",
    )(y, w_glu.astype(BF16), b_glu.reshape(1, SSM_W), norm_g.reshape(1, SSM_W))


def _outproj_kernel(ro_ref, fo_ref, so_ref, x_ref, wr_ref, wf_ref, ws_ref, g_ref, b_ref, o_ref):
    mix = jnp.dot(ro_ref[...].astype(BF16), wr_ref[...], preferred_element_type=F32)
    mix += jnp.dot(fo_ref[...].astype(BF16), wf_ref[...], preferred_element_type=F32)
    mix += jnp.dot(so_ref[...].astype(BF16), ws_ref[...], preferred_element_type=F32)
    o_ref[...] = _layer_norm(DEEPNORM_ALPHA * x_ref[...] + mix, g_ref[...], b_ref[...])


def _outproj_ln(ro, fo, so, x, w_out, ln_g, ln_b, tm):
    t = x.shape[0]
    w = w_out.astype(BF16)
    tile = lambda n: pl.BlockSpec((tm, n), lambda i: (i, 0))
    full = lambda r, n: pl.BlockSpec((r, n), lambda i: (0, 0))
    return pl.pallas_call(
        _outproj_kernel,
        out_shape=jax.ShapeDtypeStruct((t, D_MODEL), F32),
        grid=(t // tm,),
        in_specs=[tile(RET_W), tile(FOX_W), tile(SSM_W), tile(D_MODEL),
                  full(RET_W, D_MODEL), full(FOX_W, D_MODEL), full(SSM_W, D_MODEL),
                  full(1, D_MODEL), full(1, D_MODEL)],
        out_specs=tile(D_MODEL),
        compiler_params=_params("parallel"),
        name="outproj_ln",
    )(ro, fo, so, x, w[:RET_W], w[RET_W:RET_W + FOX_W], w[RET_W + FOX_W:],
      ln_g.reshape(1, D_MODEL), ln_b.reshape(1, D_MODEL))


def _topk_rows(s_ref, val_ref, idx_ref, k):
    n = s_ref.shape[0]
    iota = lax.broadcasted_iota(jnp.int32, s_ref.shape, 0)

    def body(it, _):
        s = s_ref[...]
        m = jnp.max(s, axis=0, keepdims=True)
        idx = jnp.min(jnp.where(s == m, iota, n), axis=0, keepdims=True)
        val_ref[pl.ds(it, 1), :] = m
        idx_ref[pl.ds(it, 1), :] = idx
        s_ref[...] = jnp.where(iota == idx, -jnp.inf, s)
        return 0

    lax.fori_loop(0, k, body, 0)


def _peer_topk_kernel(q_ref, sk_ref, i_ref, j_ref, gate_ref, s0_sc, s1_sc, cand_sc, v0_sc, v1_sc, x0_sc, x1_sc,
                      top_sc, sel_sc):
    kk = PEER_TOPK
    s = lax.dot_general(sk_ref[...], q_ref[...], NT_DIMS, precision=HI, preferred_element_type=F32)
    s0_sc[...] = s[:PEER_NKEYS]
    s1_sc[...] = s[PEER_NKEYS:]
    _topk_rows(s0_sc, v0_sc, x0_sc, kk)
    _topk_rows(s1_sc, v1_sc, x1_sc, kk)
    v1 = v1_sc[...]
    for a in range(kk):
        cand_sc[a * kk:(a + 1) * kk, :] = v0_sc[a:a + 1, :] + v1
    _topk_rows(cand_sc, top_sc, sel_sc, kk)
    sel = sel_sc[...]
    a_idx = sel // kk
    b_idx = sel - a_idx * kk
    ii = jnp.zeros_like(sel)
    jj = jnp.zeros_like(sel)
    for a in range(kk):
        ii = jnp.where(a_idx == a, x0_sc[a:a + 1, :], ii)
        jj = jnp.where(b_idx == a, x1_sc[a:a + 1, :], jj)
    top = top_sc[...]
    e = jnp.exp(top - top[0:1, :])
    i_ref[...] = ii
    j_ref[...] = jj
    gate_ref[...] = e / jnp.sum(e, axis=0, keepdims=True)


def _peer_topk(q, sub_keys, tm):
    t = q.shape[0]
    half = PEER_DK // 2
    sk = jnp.zeros((PEER_HEADS, 2 * PEER_NKEYS, PEER_DK), F32)
    sk = sk.at[:, :PEER_NKEYS, :half].set(sub_keys[:, 0]).at[:, PEER_NKEYS:, half:].set(sub_keys[:, 1])
    kk = PEER_TOPK
    slot = lambda: pl.BlockSpec((kk, tm), lambda i, h: (h, i))
    return pl.pallas_call(
        _peer_topk_kernel,
        out_shape=(jax.ShapeDtypeStruct((PEER_HEADS * kk, t), jnp.int32),
                   jax.ShapeDtypeStruct((PEER_HEADS * kk, t), jnp.int32),
                   jax.ShapeDtypeStruct((PEER_HEADS * kk, t), F32)),
        grid=(t // tm, PEER_HEADS),
        in_specs=[pl.BlockSpec((tm, PEER_DK), lambda i, h: (i, h)),
                  pl.BlockSpec((None, 2 * PEER_NKEYS, PEER_DK), lambda i, h: (h, 0, 0))],
        out_specs=(slot(), slot(), slot()),
        scratch_shapes=[pltpu.VMEM((PEER_NKEYS, tm), F32), pltpu.VMEM((PEER_NKEYS, tm), F32),
                        pltpu.VMEM((kk * kk, tm), F32),
                        pltpu.VMEM((kk, tm), F32), pltpu.VMEM((kk, tm), F32),
                        pltpu.VMEM((kk, tm), jnp.int32), pltpu.VMEM((kk, tm), jnp.int32),
                        pltpu.VMEM((kk, tm), F32), pltpu.VMEM((kk, tm), jnp.int32)],
        compiler_params=_params("parallel", "arbitrary"),
        name="peer_topk",
    )(q, sk)


G_ROW_STRIDE = PEER_NKEYS + 8
G_UNROLL = 8


def _peer_gates_kernel(i_ref, j_ref, gate_ref, o_ref, g_sc):
    tb, ns = i_ref.shape
    nk = PEER_NKEYS
    io = lax.broadcasted_iota(jnp.int32, (nk, ns), 0)

    def body(tt, carry):
        for u in range(G_UNROLL):
            t = tt * G_UNROLL + u
            irow = i_ref[pl.ds(t, 1), :]
            jrow = j_ref[pl.ds(t, 1), :]
            grow = gate_ref[pl.ds(t, 1), :]
            pt = jnp.where(io == irow, grow, 0.0).astype(BF16)
            qt = jnp.where(io == jrow, 1.0, 0.0).astype(BF16)
            g = lax.dot_general(pt, qt, NT_DIMS, preferred_element_type=F32)
            g_sc[pl.ds(pl.multiple_of(t * G_ROW_STRIDE, 8), nk), :] = g
        return carry

    lax.fori_loop(0, tb // G_UNROLL, body, 0)
    for i in range(nk):
        o_ref[:, i * nk:(i + 1) * nk] = g_sc[pl.ds(i, tb, stride=G_ROW_STRIDE), :].astype(BF16)


def _peer_gates(ii, jj, gate, tb):
    t = ii.shape[0]
    ns = PEER_HEADS * PEER_TOPK
    tok = lambda: pl.BlockSpec((tb, ns), lambda i: (i, 0))
    return pl.pallas_call(
        _peer_gates_kernel,
        out_shape=jax.ShapeDtypeStruct((t, PEER_N), BF16),
        grid=(t // tb,),
        in_specs=[tok(), tok(), tok()],
        out_specs=pl.BlockSpec((tb, PEER_N), lambda i: (i, 0)),
        scratch_shapes=[pltpu.VMEM((tb * G_ROW_STRIDE, PEER_NKEYS), F32)],
        compiler_params=_params("parallel"),
        name="peer_gates",
    )(ii, jj, gate)


def _peer_experts_kernel(x_ref, g_ref, u_ref, v_ref, lg_ref, lb_ref, o_ref, xb_sc, acc_sc):
    e = pl.program_id(1)

    @pl.when(e == 0)
    def _():
        xb_sc[...] = x_ref[...].astype(BF16)
        acc_sc[...] = jnp.zeros_like(acc_sc)

    h = lax.dot_general(xb_sc[...], u_ref[...], NT_DIMS, preferred_element_type=F32)
    w = (g_ref[...].astype(F32) * _gelu(h)).astype(BF16)
    acc_sc[...] += jnp.dot(w, v_ref[...], preferred_element_type=F32)

    @pl.when(e == pl.num_programs(1) - 1)
    def _():
        o_ref[...] = _layer_norm(DEEPNORM_ALPHA * x_ref[...] + acc_sc[...], lg_ref[...], lb_ref[...])


def _peer_experts_ln(x, g, u_tab, v_tab, ln_g, ln_b, tm, nb):
    t = x.shape[0]
    return pl.pallas_call(
        _peer_experts_kernel,
        out_shape=jax.ShapeDtypeStruct((t, D_MODEL), F32),
        grid=(t // tm, PEER_N // nb),
        in_specs=[pl.BlockSpec((tm, D_MODEL), lambda i, e: (i, 0)),
                  pl.BlockSpec((tm, nb), lambda i, e: (i, e)),
                  pl.BlockSpec((nb, D_MODEL), lambda i, e: (e, 0)),
                  pl.BlockSpec((nb, D_MODEL), lambda i, e: (e, 0)),
                  pl.BlockSpec((1, D_MODEL), lambda i, e: (0, 0)),
                  pl.BlockSpec((1, D_MODEL), lambda i, e: (0, 0))],
        out_specs=pl.BlockSpec((tm, D_MODEL), lambda i, e: (i, 0)),
        scratch_shapes=[pltpu.VMEM((tm, D_MODEL), BF16), pltpu.VMEM((tm, D_MODEL), F32)],
        compiler_params=_params("parallel", "arbitrary"),
        name="peer_experts_ln",
    )(x, g, u_tab, v_tab, ln_g.reshape(1, D_MODEL), ln_b.reshape(1, D_MODEL))


def _ple_kernel(x_ref, xn_ref, p_ref, wg_ref, bg_ref, we_ref, o_ref):
    gate = _sigmoid(jnp.dot(x_ref[...].astype(BF16), wg_ref[...], preferred_element_type=F32) + bg_ref[...])
    emb = jnp.dot(p_ref[...].astype(BF16), we_ref[...], preferred_element_type=F32)
    o_ref[...] = xn_ref[...] + gate * emb


def _ple(x, p, w_pg, b_pg, w_pe, tm, tn):
    t = x.shape[0]
    return pl.pallas_call(
        _ple_kernel,
        out_shape=jax.ShapeDtypeStruct((t, D_MODEL), F32),
        grid=(t // tm, D_MODEL // tn),
        in_specs=[pl.BlockSpec((tm, D_MODEL), lambda i, j: (i, 0)),
                  pl.BlockSpec((tm, tn), lambda i, j: (i, j)),
                  pl.BlockSpec((tm, PLE_DIM), lambda i, j: (i, 0)),
                  pl.BlockSpec((D_MODEL, tn), lambda i, j: (0, j)),
                  pl.BlockSpec((1, tn), lambda i, j: (0, j)),
                  pl.BlockSpec((PLE_DIM, tn), lambda i, j: (0, j))],
        out_specs=pl.BlockSpec((tm, tn), lambda i, j: (i, j)),
        compiler_params=_params("parallel", "arbitrary"),
        name="ple_gate",
    )(x, x, p, w_pg.astype(BF16), b_pg.reshape(1, D_MODEL), w_pe.astype(BF16))


def _split_bf16(w):
    hi = w.astype(BF16)
    return hi, (w - hi.astype(F32)).astype(BF16)


def _layer(x, p, lw, caches, page_table, layer, nb, seq, ns, n_new, tiles):
    t_all = x.shape[0]
    tp = nb * seq
    r = 8
    cache_k, cache_v, lf_sums, state_ret, state_re, state_im = caches
    past_len = page_table.shape[1] * (lf_sums.shape[3] // H_FOX)

    widths = [RET_W] * 4 + [FOX_W] * 3 + [H_FOX, SSM_W]
    offs = [0]
    for wd in widths:
        offs.append(offs[-1] + wd)
    w_in = lw['w_in']
    w_pack = jnp.concatenate([w_in[:, :offs[7]], w_in[:, offs[8]:]], axis=1).astype(BF16)
    w_ff = w_in[:, offs[7]:offs[8]]

    z = _matmul(x, (w_pack,), tiles['tm'], 512, "in_proj")
    lf = _forget_gates(x, w_ff, lw['b_fox_f'], tiles['tm'])
    zs = jnp.pad(z[tp:].reshape(ns, n_new, Z_W), ((0, 0), (0, r - n_new), (0, 0)))

    ro, ret_p = _retention_prompt(z, lw['ret_gn_g'], nb, seq, t_all)
    ro_s, ret_s = _retention_sample(zs, state_ret, layer, lw['ret_gn_g'], past_len, n_new)
    ro = ro.at[tp:].set(ro_s[:, :n_new].reshape(ns * n_new, RET_W))

    fo = _fox_prompt(z, _forget_cumsum(lf, nb, seq), lw['fox_norm_g'], nb, seq, t_all, tiles['attn'])
    lf_s = lf[:, tp:].T.reshape(ns, n_new, H_FOX)
    lfn = jnp.pad(lf_s, ((0, 0), (0, r - n_new), (0, 0))).reshape(ns, 1, r * H_FOX)
    fo_s = _fox_sample(z[tp:, Z_FQ:Z_FQ + FOX_W].reshape(ns, n_new * H_FOX, DH_FOX),
                       zs[:, :, Z_FK:Z_FK + FOX_W].reshape(ns, r * H_FOX, DH_FOX),
                       zs[:, :, Z_FV:Z_FV + FOX_W].reshape(ns, r * H_FOX, DH_FOX),
                       lfn, cache_k, cache_v, lf_sums, page_table, layer, lw['fox_norm_g'])
    fo = fo.at[tp:].set(fo_s.reshape(ns * n_new, FOX_W))

    ssm = (lw['ssm_a_re'], lw['ssm_a_im'], lw['ssm_b_re'], lw['ssm_b_im'], lw['ssm_c_re'], lw['ssm_c_im'],
           lw['ssm_d'], lw['ssm_log_dt'])
    y_p, hre_p, him_p = _s5_prompt(z[:tp, Z_SU:], _s5_mats(*ssm, SSM_GROUP, SSM_GROUP), nb, seq)
    y_s, hre_s, him_s = _s5_sample(zs[:, :, Z_SU:], state_re[layer], state_im[layer], _s5_mats(*ssm, r, n_new), n_new)
    so = _s5_glu(jnp.concatenate([y_p, y_s], 0), lw['w_glu'], lw['b_glu'], lw['ssm_norm_g'], tiles['tm'])

    x1 = _outproj_ln(ro, fo, so, x, lw['w_out'], lw['ln1_g'], lw['ln1_b'], tiles['tm_ln'])

    q = _matmul(x1, _split_bf16(lw['peer_w_q']), tiles['tm'], 512, "peer_query")
    ii, jj, gate = _peer_topk(q, lw['peer_sub_keys'], tiles['tm'])
    g = _peer_gates(ii.T, jj.T, gate.T, 128)
    x2 = _peer_experts_ln(x1, g, lw['peer_u'].astype(BF16), lw['peer_v'].astype(BF16),
                          lw['ln2_g'], lw['ln2_b'], tiles['tm'], 512)

    x3 = _ple(x2, p, lw['w_pg'], lw['b_pg'], lw['w_pe'], tiles['tm'], 512)

    st_p = (z[:tp, Z_FK:Z_FK + FOX_W].reshape(nb, seq, H_FOX, DH_FOX),
            z[:tp, Z_FV:Z_FV + FOX_W].reshape(nb, seq, H_FOX, DH_FOX),
            lf[:, :tp].T.reshape(nb, seq, H_FOX), ret_p, hre_p, him_p)
    st_s = (z[tp:, Z_FK:Z_FK + FOX_W].reshape(ns, n_new, H_FOX, DH_FOX),
            z[tp:, Z_FV:Z_FV + FOX_W].reshape(ns, n_new, H_FOX, DH_FOX),
            lf_s, ret_s, hre_s, him_s)
    return x3, st_p, st_s


def kernel(x_prompt, x_sample, p_prompt, p_sample, cache_fox_k, cache_fox_v, cache_fox_lf, state_ret, state_ssm_re, state_ssm_im, page_table, w_in, w_out, ret_gn_g, b_fox_f, fox_norm_g, ssm_a_re, ssm_a_im, ssm_b_re, ssm_b_im, ssm_c_re, ssm_c_im, ssm_d, ssm_log_dt, w_glu, b_glu, ssm_norm_g, ln1_g, ln1_b, ln2_g, ln2_b, peer_w_q, peer_sub_keys, peer_u, peer_v, w_pe, w_pg, b_pg):
    nb, seq, d = x_prompt.shape
    ns, n_new, _ = x_sample.shape
    tp, ts = nb * seq, ns * n_new
    t_all = tp + ts
    tm = 640 if t_all % 640 == 0 else 128
    tiles = dict(tm=tm, tm_ln=tm // 2 if tm == 640 else tm, attn=256 if seq % 256 == 0 else 128)
    x = jnp.concatenate([x_prompt.reshape(tp, d), x_sample.reshape(ts, d)], 0)
    depth, n_pool, pg, nh, dh = cache_fox_k.shape
    caches = (cache_fox_k.reshape(depth, n_pool, pg * nh, dh), cache_fox_v.reshape(depth, n_pool, pg * nh, dh),
              _lf_prep(cache_fox_lf), state_ret, state_ssm_re, state_ssm_im)
    weights = dict(w_in=w_in, w_out=w_out, ret_gn_g=ret_gn_g, b_fox_f=b_fox_f, fox_norm_g=fox_norm_g,
                   ssm_a_re=ssm_a_re, ssm_a_im=ssm_a_im, ssm_b_re=ssm_b_re, ssm_b_im=ssm_b_im,
                   ssm_c_re=ssm_c_re, ssm_c_im=ssm_c_im, ssm_d=ssm_d, ssm_log_dt=ssm_log_dt,
                   w_glu=w_glu, b_glu=b_glu, ssm_norm_g=ssm_norm_g, ln1_g=ln1_g, ln1_b=ln1_b,
                   ln2_g=ln2_g, ln2_b=ln2_b, peer_w_q=peer_w_q, peer_sub_keys=peer_sub_keys,
                   peer_u=peer_u, peer_v=peer_v, w_pe=w_pe, w_pg=w_pg, b_pg=b_pg)
    sts_p, sts_s = [], []
    for l in range(DEPTH):
        lw = {k: v[l] for k, v in weights.items()}
        p = jnp.concatenate([p_prompt[l].reshape(tp, PLE_DIM), p_sample[l].reshape(ts, PLE_DIM)], 0)
        x, st_p, st_s = _layer(x, p, lw, caches, page_table, l, nb, seq, ns, n_new, tiles)
        sts_p.append(st_p)
        sts_s.append(st_s)
    stack = lambda sts, i: jnp.stack([s[i] for s in sts])
    return ((x[:tp].reshape(nb, seq, d), x[tp:].reshape(ns, n_new, d))
            + tuple(stack(sts_p, i) for i in range(6)) + tuple(stack(sts_s, i) for i in range(6)))
```

```python
import functools
import math

import jax
import jax.numpy as jnp
from jax import lax
from jax.experimental import pallas as pl
from jax.experimental.pallas import tpu as pltpu

F32 = jnp.float32
BF16 = jnp.bfloat16
HI = lax.Precision.HIGHEST

D_MODEL = 2048
DEPTH = 2
H_RET, DH_RET = 4, 128
RET_W = H_RET * DH_RET
H_FOX, DH_FOX = 8, 128
FOX_W = H_FOX * DH_FOX
SSM_W, SSM_GROUP = 512, 16
SSM_G = SSM_W // SSM_GROUP
SSM_P = 64
PEER_HEADS, PEER_NKEYS, PEER_DK, PEER_TOPK = 8, 128, 128, 16
PEER_N = PEER_NKEYS * PEER_NKEYS
PLE_DIM = 256
BLOCK = 128
ROPE_BASE = 10000.0
DEEPNORM_ALPHA = (2 * DEPTH) ** 0.25
EPS = 1e-5

Z_W = 4 * RET_W + 3 * FOX_W + SSM_W
Z_FQ, Z_FK, Z_FV, Z_SU = 4 * RET_W, 4 * RET_W + FOX_W, 4 * RET_W + 2 * FOX_W, 4 * RET_W + 3 * FOX_W

VMEM_LIMIT = 56 * 1024 * 1024
NT_DIMS = (((1,), (1,)), ((), ()))
TN_DIMS = (((0,), (0,)), ((), ()))


def _params(*sem):
    return pltpu.CompilerParams(dimension_semantics=sem, vmem_limit_bytes=VMEM_LIMIT)


def _gelu(x):
    return 0.5 * x * (1.0 + jnp.tanh(math.sqrt(2.0 / math.pi) * (x + 0.044715 * (x * x * x))))


def _sigmoid(x):
    return 1.0 / (1.0 + jnp.exp(-x))


def _log_sigmoid(x):
    return jnp.minimum(x, 0.0) - jnp.log1p(jnp.exp(-jnp.abs(x)))


def _layer_norm(y, g, b):
    mu = jnp.mean(y, -1, keepdims=True)
    var = jnp.mean(jnp.square(y - mu), -1, keepdims=True)
    return (y - mu) * lax.rsqrt(var + EPS) * g + b


def _rms_norm(y, g):
    return y * lax.rsqrt(jnp.mean(y * y, -1, keepdims=True) + EPS) * g


def _mm_kernel(x_ref, w_ref, o_ref):
    o_ref[...] = jnp.dot(x_ref[...].astype(BF16), w_ref[...], preferred_element_type=F32)


def _mm3_kernel(x_ref, wh_ref, wl_ref, o_ref):
    x = x_ref[...]
    xh = x.astype(BF16)
    xl = (x - xh.astype(F32)).astype(BF16)
    wh = wh_ref[...]
    acc = jnp.dot(xh, wh, preferred_element_type=F32)
    acc += jnp.dot(xl, wh, preferred_element_type=F32)
    acc += jnp.dot(xh, wl_ref[...], preferred_element_type=F32)
    o_ref[...] = acc


def _matmul(x, ws, tm, tn, name):
    t, k = x.shape
    n = ws[0].shape[1]
    kern = _mm_kernel if len(ws) == 1 else _mm3_kernel
    return pl.pallas_call(
        kern,
        out_shape=jax.ShapeDtypeStruct((t, n), F32),
        grid=(t // tm, n // tn),
        in_specs=[pl.BlockSpec((tm, k), lambda i, j: (i, 0))]
        + [pl.BlockSpec((k, tn), lambda i, j: (0, j)) for _ in ws],
        out_specs=pl.BlockSpec((tm, tn), lambda i, j: (i, j)),
        compiler_params=_params("parallel", "arbitrary"),
        name=name,
    )(x, *ws)


def _gate_kernel(x_ref, w_ref, b_ref, lf_ref):
    f = lax.dot_general(w_ref[...], x_ref[...], NT_DIMS, precision=HI, preferred_element_type=F32)
    lf_ref[...] = _log_sigmoid(f + b_ref[...])


def _forget_gates(x, w_ff, b_ff, tm):
    t, k = x.shape
    return pl.pallas_call(
        _gate_kernel,
        out_shape=jax.ShapeDtypeStruct((H_FOX, t), F32),
        grid=(t // tm,),
        in_specs=[
            pl.BlockSpec((tm, k), lambda i: (i, 0)),
            pl.BlockSpec((H_FOX, k), lambda i: (0, 0)),
            pl.BlockSpec((H_FOX, 1), lambda i: (0, 0)),
        ],
        out_specs=pl.BlockSpec((H_FOX, tm), lambda i: (0, i)),
        compiler_params=_params("parallel"),
        name="forget_gates",
    )(x, w_ff.T, b_ff.reshape(H_FOX, 1))


def _cumsum_kernel(lf_ref, c_ref, carry):
    @pl.when(pl.program_id(1) == 0)
    def _():
        carry[...] = jnp.zeros_like(carry)

    n = lf_ref.shape[1]
    r = lax.broadcasted_iota(jnp.int32, (n, n), 0)
    c = lax.broadcasted_iota(jnp.int32, (n, n), 1)
    upper = (r <= c).astype(F32)
    cs = jnp.dot(lf_ref[...], upper, precision=HI, preferred_element_type=F32) + carry[...]
    c_ref[...] = cs
    carry[...] = cs[:, n - 1:n]


def _forget_cumsum(lf, nb, seq):
    n = 256 if seq % 256 == 0 else 128
    nc = seq // n
    return pl.pallas_call(
        _cumsum_kernel,
        out_shape=jax.ShapeDtypeStruct((H_FOX, nb * seq), F32),
        grid=(nb, nc),
        in_specs=[pl.BlockSpec((H_FOX, n), lambda b, i: (0, b * nc + i))],
        out_specs=pl.BlockSpec((H_FOX, n), lambda b, i: (0, b * nc + i)),
        scratch_shapes=[pltpu.VMEM((H_FOX, 1), F32)],
        compiler_params=_params("parallel", "arbitrary"),
        name="forget_cumsum",
    )(lf)


def _fox_prompt_kernel(q_ref, k_ref, v_ref, c_ref, g_ref, o_ref, m_sc, l_sc, acc_sc, *, scale):
    qi = pl.program_id(1)
    ki = pl.program_id(2)
    bq = q_ref.shape[0]
    bk = k_ref.shape[0]

    @pl.when(ki == 0)
    def _():
        m_sc[...] = jnp.full_like(m_sc, -jnp.inf)
        l_sc[...] = jnp.zeros_like(l_sc)
        acc_sc[...] = jnp.zeros_like(acc_sc)

    def attend(diagonal):
        if diagonal:
            causal = (lax.broadcasted_iota(jnp.int32, (bq, bk), 1)
                      <= lax.broadcasted_iota(jnp.int32, (bq, bk), 0))
        for h in range(H_FOX):
            sl = slice(h * DH_FOX, (h + 1) * DH_FOX)
            q = q_ref[:, sl].astype(BF16)
            k = k_ref[:, sl].astype(BF16)
            s = lax.dot_general(q, k, NT_DIMS, preferred_element_type=F32) * scale - c_ref[h:h + 1, :]
            if diagonal:
                s = jnp.where(causal, s, -jnp.inf)
            m_prev = m_sc[h]
            m_new = jnp.maximum(m_prev, jnp.max(s, axis=1, keepdims=True))
            alpha = jnp.exp(m_prev - m_new)
            p = jnp.exp(s - jnp.tile(m_new, (1, bk // 128)))
            l_sc[h] = alpha * l_sc[h] + jnp.sum(p, axis=1, keepdims=True)
            acc_sc[:, sl] = alpha * acc_sc[:, sl] + jnp.dot(
                p.astype(BF16), v_ref[:, sl].astype(BF16), preferred_element_type=F32)
            m_sc[h] = m_new

    @pl.when(ki < qi)
    def _():
        attend(False)

    @pl.when(ki == qi)
    def _():
        attend(True)
        for h in range(H_FOX):
            sl = slice(h * DH_FOX, (h + 1) * DH_FOX)
            acc_sc[:, sl] = acc_sc[:, sl] / l_sc[h]
        o_ref[...] = _rms_norm(acc_sc[...], g_ref[...])


def _fox_prompt(z, c, norm_g, nb, seq, t_all, blk):
    nq = seq // blk
    cq, ck, cv = Z_FQ // FOX_W, Z_FK // FOX_W, Z_FV // FOX_W
    kern = functools.partial(_fox_prompt_kernel, scale=DH_FOX ** -0.5)
    return pl.pallas_call(
        kern,
        out_shape=jax.ShapeDtypeStruct((t_all, FOX_W), F32),
        grid=(nb, nq, nq),
        in_specs=[
            pl.BlockSpec((blk, FOX_W), lambda b, i, j: (b * nq + i, cq)),
            pl.BlockSpec((blk, FOX_W), lambda b, i, j: (b * nq + jnp.minimum(i, j), ck)),
            pl.BlockSpec((blk, FOX_W), lambda b, i, j: (b * nq + jnp.minimum(i, j), cv)),
            pl.BlockSpec((H_FOX, blk), lambda b, i, j: (0, b * nq + jnp.minimum(i, j))),
            pl.BlockSpec((1, FOX_W), lambda b, i, j: (0, 0)),
        ],
        out_specs=pl.BlockSpec((blk, FOX_W), lambda b, i, j: (b * nq + i, 0)),
        scratch_shapes=[
            pltpu.VMEM((H_FOX, blk, 128), F32),
            pltpu.VMEM((H_FOX, blk, 128), F32),
            pltpu.VMEM((blk, FOX_W), F32),
        ],
        compiler_params=_params("parallel", "parallel", "arbitrary"),
        name="fox_prompt",
    )(z, z, z, c, norm_g.reshape(1, FOX_W))


def _lf_prep_kernel(lf_ref, w_ref, o_ref):
    x = lf_ref[...]
    p1 = x.astype(BF16)
    r1 = x - p1.astype(F32)
    p2 = r1.astype(BF16)
    p3 = (r1 - p2.astype(F32)).astype(BF16)
    w = w_ref[...]
    acc = jnp.dot(p3, w, preferred_element_type=F32)
    acc += jnp.dot(p2, w, preferred_element_type=F32)
    acc += jnp.dot(p1, w, preferred_element_type=F32)
    o_ref[...] = acc


def _lf_prep(cache_lf):
    depth, n_pool, pg, nh = cache_lf.shape
    c = pg * nh
    rows = depth * n_pool
    tm = max(d for d in (256, 128, 64, 32, 16, 8) if rows % d == 0)
    src = jnp.arange(c)[:, None]
    dst = jnp.arange(c)[None, :]
    same_head = (src % nh) == (dst % nh)
    w = jnp.concatenate([same_head & (src // nh > dst // nh), same_head], axis=1).astype(BF16)
    return pl.pallas_call(
        _lf_prep_kernel,
        out_shape=jax.ShapeDtypeStruct((rows, 2 * c), F32),
        grid=(rows // tm,),
        in_specs=[pl.BlockSpec((tm, c), lambda i: (i, 0)), pl.BlockSpec((c, 2 * c), lambda i: (0, 0))],
        out_specs=pl.BlockSpec((tm, 2 * c), lambda i: (i, 0)),
        compiler_params=_params("parallel"),
        name="lf_prep",
    )(cache_lf.reshape(rows, c), w)


def _fox_sample_kernel(pt_ref, q_ref, kn_ref, vn_ref, lfn_ref, g_ref, *rest, scale, npg, page_of):
    k_refs, v_refs, x_refs = rest[:npg], rest[npg:2 * npg], rest[2 * npg:3 * npg]
    o_ref, m_sc, l_sc, acc_sc, carry_sc = rest[3 * npg:]
    step = pl.program_id(1)
    nr = q_ref.shape[0]
    ncol = k_refs[0].shape[0]
    head_mask = H_FOX - 1
    head_bits = H_FOX.bit_length() - 1

    @pl.when(step == 0)
    def _():
        m_sc[...] = jnp.full_like(m_sc, -jnp.inf)
        l_sc[...] = jnp.zeros_like(l_sc)
        acc_sc[...] = jnp.zeros_like(acc_sc)
        carry_sc[...] = jnp.zeros_like(carry_sc)

    def equal_heads(ncol):
        rows = lax.broadcasted_iota(jnp.int32, (nr, ncol), 0)
        cols = lax.broadcasted_iota(jnp.int32, (nr, ncol), 1)
        return rows, cols, (rows & head_mask) == (cols & head_mask)

    q = q_ref[...].astype(BF16)
    _, _, same_head = equal_heads(ncol)
    carry = carry_sc[...]
    m_prev = m_sc[...]
    m_new = m_prev
    scores = []
    for i in range(npg):
        x = x_refs[i][pl.ds(page_of(pl.program_id(0), step, i, pt_ref) % 8, 1), :]
        bias = x[:, :ncol] + carry
        carry = carry + x[:, ncol:]
        s = lax.dot_general(q, k_refs[i][...].astype(BF16), NT_DIMS, preferred_element_type=F32) * scale + bias
        s = jnp.where(same_head, s, -jnp.inf)
        m_new = jnp.maximum(m_new, jnp.max(s, axis=1, keepdims=True))
        scores.append(s)
    carry_sc[...] = carry
    alpha = jnp.exp(m_prev - m_new)
    l_new = alpha * l_sc[...]
    acc = alpha * acc_sc[...]
    for i in range(npg):
        p = jnp.exp(scores[i] - m_new)
        l_new += jnp.sum(p, axis=1, keepdims=True)
        acc += jnp.dot(p.astype(BF16), v_refs[i][...].astype(BF16), preferred_element_type=F32)
    m_sc[...] = m_new
    l_sc[...] = l_new
    acc_sc[...] = acc

    @pl.when(step == pl.num_programs(1) - 1)
    def _():
        nc = kn_ref.shape[0]
        rows, cols, same = equal_heads(nc)
        causal = (cols >> head_bits) <= (rows >> head_bits)
        r2 = lax.broadcasted_iota(jnp.int32, (nc, nc), 0)
        c2 = lax.broadcasted_iota(jnp.int32, (nc, nc), 1)
        prefix = jnp.where((r2 & head_mask) == (c2 & head_mask), jnp.where(r2 <= c2, 1.0, 0.0), 0.0)
        cn = jnp.dot(jnp.broadcast_to(lfn_ref[...], (8, nc)), prefix, precision=HI,
                     preferred_element_type=F32)[0:1]
        s = lax.dot_general(q, kn_ref[...].astype(BF16), NT_DIMS, preferred_element_type=F32) * scale - cn
        s = jnp.where(same, jnp.where(causal, s, -jnp.inf), -jnp.inf)
        m_fin = jnp.maximum(m_new, jnp.max(s, axis=1, keepdims=True))
        a_fin = jnp.exp(m_new - m_fin)
        p = jnp.exp(s - m_fin)
        l_fin = a_fin * l_new + jnp.sum(p, axis=1, keepdims=True)
        out = (a_fin * acc + jnp.dot(p.astype(BF16), vn_ref[...].astype(BF16),
                                     preferred_element_type=F32)) / l_fin
        r3 = lax.broadcasted_iota(jnp.int32, (nr, nr), 0)
        c3 = lax.broadcasted_iota(jnp.int32, (nr, nr), 1)
        same_token = jnp.where((r3 >> head_bits) == (c3 >> head_bits), 1.0, 0.0)
        ssq = jnp.sum(jnp.dot(same_token, out * out, precision=HI, preferred_element_type=F32),
                      axis=1, keepdims=True)
        o_ref[...] = out * lax.rsqrt(ssq / (H_FOX * DH_FOX) + EPS) * g_ref[...]


def _fox_sample(q, kn, vn, lfn, cache_k, cache_v, lf_sums, page_table, layer, norm_g):
    ns, nr, _ = q.shape
    nc = kn.shape[1]
    n_pages = page_table.shape[1]
    n_pool, cols = cache_k.shape[1], cache_k.shape[2]
    assert n_pool % 8 == 0 and lf_sums.shape == (cache_k.shape[0] * n_pool, 2 * cols)
    npg = max(d for d in range(1, 9) if n_pages % d == 0)
    g = jnp.tile(norm_g.reshape(H_FOX, DH_FOX), (nr // H_FOX, 1))

    def page_of(b, s, i, pt):
        return pt[b * n_pages + (n_pages - 1 - (s * npg + i))]

    kern = functools.partial(_fox_sample_kernel, scale=DH_FOX ** -0.5, npg=npg, page_of=page_of)

    def paged(i):
        return pl.BlockSpec((None, None, cols, DH_FOX), lambda b, s, pt: (layer, page_of(b, s, i, pt), 0, 0))

    def paged_sums(i):
        return pl.BlockSpec((8, 2 * cols), lambda b, s, pt: ((layer * n_pool + page_of(b, s, i, pt)) // 8, 0))

    grid_spec = pltpu.PrefetchScalarGridSpec(
        num_scalar_prefetch=1,
        grid=(ns, n_pages // npg),
        in_specs=[
            pl.BlockSpec((None, nr, DH_FOX), lambda b, s, pt: (b, 0, 0)),
            pl.BlockSpec((None, nc, DH_FOX), lambda b, s, pt: (b, 0, 0)),
            pl.BlockSpec((None, nc, DH_FOX), lambda b, s, pt: (b, 0, 0)),
            pl.BlockSpec((None, 1, nc), lambda b, s, pt: (b, 0, 0)),
            pl.BlockSpec((nr, DH_FOX), lambda b, s, pt: (0, 0)),
        ] + [paged(i) for i in range(npg)] * 2 + [paged_sums(i) for i in range(npg)],
        out_specs=pl.BlockSpec((None, nr, DH_FOX), lambda b, s, pt: (b, 0, 0)),
        scratch_shapes=[
            pltpu.VMEM((nr, 1), F32),
            pltpu.VMEM((nr, 1), F32),
            pltpu.VMEM((nr, DH_FOX), F32),
            pltpu.VMEM((1, cols), F32),
        ],
    )
    return pl.pallas_call(
        kern,
        out_shape=jax.ShapeDtypeStruct((ns, nr, DH_FOX), F32),
        grid_spec=grid_spec,
        compiler_params=_params("parallel", "arbitrary"),
        name="fox_sample",
    )(page_table.reshape(-1), q, kn, vn, lfn, g,
      *([cache_k] * npg), *([cache_v] * npg), *([lf_sums] * npg))


def _ret_kernel(*refs, has_s0):
    if has_s0:
        (q_ref, k_ref, v_ref, gt_ref, cos_ref, sin_ref, dm_ref, xi_ref, zt_ref, gc_ref, gn_ref, s0_ref,
         o_ref, s_ref) = refs
    else:
        (q_ref, k_ref, v_ref, gt_ref, cos_ref, sin_ref, dm_ref, xi_ref, zt_ref, gc_ref, gn_ref,
         o_ref, s_ref) = refs

    @pl.when(pl.program_id(2) == 0)
    def _():
        if has_s0:
            s_ref[...] = s0_ref[...]
        else:
            s_ref[...] = jnp.zeros_like(s_ref)

    cos = cos_ref[...]
    sin = sin_ref[...]

    def rot(x):
        return x * cos + pltpu.roll(x, DH_RET // 2, 1) * sin

    q = rot(q_ref[...])
    k = rot(k_ref[...]) * DH_RET ** -0.5
    v = v_ref[...]
    s = s_ref[...]
    att = lax.dot_general(q, k, NT_DIMS, preferred_element_type=F32) * dm_ref[...]
    o = jnp.dot(att, v, preferred_element_type=F32) + jnp.dot(q * xi_ref[...], s, preferred_element_type=F32)
    s_ref[...] = gc_ref[...] * s + lax.dot_general(k * zt_ref[...], v, TN_DIMS, preferred_element_type=F32)
    mu = jnp.mean(o, -1, keepdims=True)
    var = jnp.mean(jnp.square(o - mu), -1, keepdims=True)
    on = (o - mu) * lax.rsqrt(var + EPS)
    gt = gt_ref[...]
    o_ref[...] = on * gn_ref[...] * (gt * _sigmoid(gt))


def _ret_consts(c, c_eff):
    lg = jnp.log1p(-jnp.exp2(-5.0 - jnp.arange(H_RET, dtype=F32)))
    idx = jnp.arange(c, dtype=F32)
    rel = idx[:, None] - idx[None, :]
    dmask = jnp.where(rel >= 0, jnp.exp(lg[:, None, None] * jnp.maximum(rel, 0.0)), 0.0)
    xi = jnp.exp(lg[:, None] * (idx + 1.0))[..., None]
    zeta = jnp.exp(lg[:, None] * (c_eff - 1.0 - idx))[..., None]
    g_c = jnp.broadcast_to(jnp.exp(lg * c_eff)[:, None, None], (H_RET, 1, DH_RET))
    return dmask, xi, zeta, g_c


def _rope_tables(pos):
    inv = ROPE_BASE ** (-jnp.arange(0, DH_RET, 2, dtype=F32) / DH_RET)
    ang = pos.astype(F32)[:, None] * inv[None, :]
    cos, sin = jnp.cos(ang), jnp.sin(ang)
    return jnp.concatenate([cos, cos], -1), jnp.concatenate([-sin, sin], -1)


def _retention_prompt(z, gn_g, nb, seq, t_all):
    c = BLOCK
    nc = seq // c
    dmask, xi, zeta, g_c = _ret_consts(c, c)
    cos, sin = _rope_tables(jnp.arange(seq))

    def zspec(col0):
        return pl.BlockSpec((c, DH_RET), lambda b, h, i: (b * nc + i, col0 + h))

    hspec = lambda shape: pl.BlockSpec((None,) + shape, lambda b, h, i: (h, 0, 0))
    return pl.pallas_call(
        functools.partial(_ret_kernel, has_s0=False),
        out_shape=(jax.ShapeDtypeStruct((t_all, RET_W), F32),
                   jax.ShapeDtypeStruct((nb, H_RET, DH_RET, DH_RET), F32)),
        grid=(nb, H_RET, nc),
        in_specs=[zspec(0), zspec(H_RET), zspec(2 * H_RET), zspec(3 * H_RET),
                  pl.BlockSpec((c, DH_RET), lambda b, h, i: (i, 0)),
                  pl.BlockSpec((c, DH_RET), lambda b, h, i: (i, 0)),
                  hspec((c, c)), hspec((c, 1)), hspec((c, 1)), hspec((1, DH_RET)),
                  pl.BlockSpec((1, DH_RET), lambda b, h, i: (0, h))],
        out_specs=(pl.BlockSpec((c, DH_RET), lambda b, h, i: (b * nc + i, h)),
                   pl.BlockSpec((None, None, DH_RET, DH_RET), lambda b, h, i: (b, h, 0, 0))),
        compiler_params=_params("parallel", "parallel", "arbitrary"),
        name="retention_prompt",
    )(z, z, z, z, cos, sin, dmask, xi, zeta, g_c, gn_g.reshape(1, RET_W))


def _retention_sample(zs, state_ret, layer, gn_g, pos0, n_new):
    ns, r, _ = zs.shape
    dmask, xi, zeta, g_c = _ret_consts(r, n_new)
    cos, sin = _rope_tables(pos0 + jnp.arange(r))

    def zspec(col0):
        return pl.BlockSpec((None, r, DH_RET), lambda b, h, i: (b, 0, col0 + h))

    hspec = lambda shape: pl.BlockSpec((None,) + shape, lambda b, h, i: (h, 0, 0))
    return pl.pallas_call(
        functools.partial(_ret_kernel, has_s0=True),
        out_shape=(jax.ShapeDtypeStruct((ns, r, RET_W), F32),
                   jax.ShapeDtypeStruct((ns, H_RET, DH_RET, DH_RET), F32)),
        grid=(ns, H_RET, 1),
        in_specs=[zspec(0), zspec(H_RET), zspec(2 * H_RET), zspec(3 * H_RET),
                  pl.BlockSpec((r, DH_RET), lambda b, h, i: (0, 0)),
                  pl.BlockSpec((r, DH_RET), lambda b, h, i: (0, 0)),
                  hspec((r, r)), hspec((r, 1)), hspec((r, 1)), hspec((1, DH_RET)),
                  pl.BlockSpec((1, DH_RET), lambda b, h, i: (0, h)),
                  pl.BlockSpec((None, None, None, DH_RET, DH_RET), lambda b, h, i: (layer, b, h, 0, 0))],
        out_specs=(pl.BlockSpec((None, r, DH_RET), lambda b, h, i: (b, 0, h)),
                   pl.BlockSpec((None, None, DH_RET, DH_RET), lambda b, h, i: (b, h, 0, 0))),
        compiler_params=_params("parallel", "parallel", "arbitrary"),
        name="retention_sample",
    )(zs, zs, zs, zs, cos, sin, dmask, xi, zeta, g_c, gn_g.reshape(1, RET_W), state_ret)


def _s5_mats(a_re, a_im, b_re, b_im, c_re, c_im, d_skip, log_dt, c, c_eff):
    lam = lax.complex(a_re, a_im)
    lam_dt = lam * jnp.exp(log_dt)[:, None]
    lam_bar = jnp.exp(lam_dt)
    b_bar = ((lam_bar - 1.0) / lam)[..., None] * lax.complex(b_re, b_im)
    c_mat = lax.complex(c_re, c_im)
    g, p, cg = b_bar.shape

    def power(n):
        return jnp.exp(lam_dt[None] * jnp.asarray(n, F32)[:, None, None])

    lags = jnp.real(jnp.einsum('gop,jgp,gpi->jgoi', c_mat, power(jnp.arange(c)), b_bar, precision=HI))
    sig = jnp.arange(c)[:, None]
    tau = jnp.arange(c)[None, :]
    m = jnp.where((tau >= sig)[:, :, None, None, None], lags[jnp.maximum(tau - sig, 0)], 0.0)
    m = m.transpose(2, 0, 4, 1, 3)
    m = m + (jnp.eye(c)[None, :, None, :, None] * jnp.eye(cg)[None, None, :, None, :]
             * d_skip.reshape(g, 1, cg, 1, 1))
    m = m.reshape(g, c * cg, c * cg)
    w = power(c_eff - 1.0 - jnp.arange(c, dtype=F32))[:, :, :, None] * b_bar[None]
    w = jnp.where((jnp.arange(c) < c_eff)[:, None, None, None], w, 0.0)
    w = w.transpose(1, 0, 3, 2).reshape(g, c * cg, p)
    v = power(jnp.arange(c, dtype=F32) + 1.0)[:, :, None, :] * c_mat[None]
    v = v.transpose(1, 3, 0, 2).reshape(g, p, c * cg)
    a = power(jnp.asarray([c_eff], F32))[0]
    return (m.astype(F32), jnp.real(w), jnp.imag(w), jnp.real(v), -jnp.imag(v), jnp.real(a), jnp.imag(a))


def _s5_prompt_kernel(u_ref, m_ref, wre_ref, wim_ref, vre_ref, vim_ref, are_ref, aim_ref,
                      y_ref, hre_ref, him_ref, ere_sc, eim_sc, hsre_sc, hsim_sc):
    ns, nc, _ = u_ref.shape
    ere = jnp.dot(u_ref[0], wre_ref[0], precision=HI, preferred_element_type=F32)
    eim = jnp.dot(u_ref[0], wim_ref[0], precision=HI, preferred_element_type=F32)
    for s in range(1, ns):
        ere += jnp.dot(u_ref[s], wre_ref[s], precision=HI, preferred_element_type=F32)
        eim += jnp.dot(u_ref[s], wim_ref[s], precision=HI, preferred_element_type=F32)
    ere_sc[...] = ere
    eim_sc[...] = eim
    ar = are_ref[...]
    ai = aim_ref[...]

    def body(k, carry):
        hr, hi = carry
        hsre_sc[pl.ds(k, 1), :] = hr
        hsim_sc[pl.ds(k, 1), :] = hi
        er = ere_sc[pl.ds(k, 1), :]
        ei = eim_sc[pl.ds(k, 1), :]
        return ar * hr - ai * hi + er, ar * hi + ai * hr + ei

    zero = jnp.zeros((1, 2 * SSM_P), F32)
    hr, hi = lax.fori_loop(0, nc, body, (zero, zero))
    hre_ref[...] = hr
    him_ref[...] = hi
    hsre = hsre_sc[...]
    hsim = hsim_sc[...]
    for s in range(ns):
        y_ref[s] = (jnp.dot(u_ref[s], m_ref[...], precision=HI, preferred_element_type=F32)
                    + jnp.dot(hsre, vre_ref[s], precision=HI, preferred_element_type=F32)
                    + jnp.dot(hsim, vim_ref[s], precision=HI, preferred_element_type=F32))


def _s5_prompt(su, mats, nb, seq):
    assert nb == 2
    c = SSM_GROUP
    nc = seq // c
    cw = c * SSM_GROUP
    m, w_re, w_im, v_re, v_im, a_re, a_im = mats
    u = su.reshape(nb, nc, c, SSM_G, SSM_GROUP).transpose(3, 0, 1, 2, 4).reshape(SSM_G, nb, nc, cw)

    def lanes(w):
        return jnp.stack([jnp.pad(w, ((0, 0), (0, 0), (SSM_P * s, SSM_P * (nb - 1 - s)))) for s in range(nb)], 1)

    def rows(v):
        return jnp.stack([jnp.pad(v, ((0, 0), (SSM_P * s, SSM_P * (nb - 1 - s)), (0, 0))) for s in range(nb)], 1)

    gspec = lambda shape: pl.BlockSpec((None,) + shape, lambda g: (g,) + (0,) * len(shape))
    y, hre, him = pl.pallas_call(
        _s5_prompt_kernel,
        out_shape=(jax.ShapeDtypeStruct((SSM_G, nb, nc, cw), F32),
                   jax.ShapeDtypeStruct((SSM_G, 1, nb * SSM_P), F32),
                   jax.ShapeDtypeStruct((SSM_G, 1, nb * SSM_P), F32)),
        grid=(SSM_G,),
        in_specs=[gspec((nb, nc, cw)), gspec((cw, cw)),
                  gspec((nb, cw, nb * SSM_P)), gspec((nb, cw, nb * SSM_P)),
                  gspec((nb, nb * SSM_P, cw)), gspec((nb, nb * SSM_P, cw)),
                  gspec((1, nb * SSM_P)), gspec((1, nb * SSM_P))],
        out_specs=(gspec((nb, nc, cw)), gspec((1, nb * SSM_P)), gspec((1, nb * SSM_P))),
        scratch_shapes=[pltpu.VMEM((nc, nb * SSM_P), F32) for _ in range(4)],
        compiler_params=_params("parallel"),
        name="s5_prompt",
    )(u, m, lanes(w_re), lanes(w_im), rows(v_re), rows(v_im),
      jnp.tile(a_re, (1, nb))[:, None, :], jnp.tile(a_im, (1, nb))[:, None, :])
    y = y.reshape(SSM_G, nb, nc, c, SSM_GROUP).transpose(1, 2, 3, 0, 4).reshape(nb * seq, SSM_W)
    unpack = lambda h: h.reshape(SSM_G, nb, SSM_P).transpose(1, 0, 2)
    return y, unpack(hre), unpack(him)


def _s5_sample_kernel(u_ref, h0re_ref, h0im_ref, m_ref, wre_ref, wim_ref, vre_ref, vim_ref, are_ref, aim_ref,
                      y_ref, hre_ref, him_ref):
    u = u_ref[...]
    hr = h0re_ref[...]
    hi = h0im_ref[...]
    ar = are_ref[...]
    ai = aim_ref[...]
    hre_ref[...] = ar * hr - ai * hi + jnp.dot(u, wre_ref[...], precision=HI, preferred_element_type=F32)
    him_ref[...] = ar * hi + ai * hr + jnp.dot(u, wim_ref[...], precision=HI, preferred_element_type=F32)
    y_ref[...] = (jnp.dot(u, m_ref[...], precision=HI, preferred_element_type=F32)
                  + jnp.dot(hr, vre_ref[...], precision=HI, preferred_element_type=F32)
                  + jnp.dot(hi, vim_ref[...], precision=HI, preferred_element_type=F32))


def _s5_sample(su, h0_re, h0_im, mats, n_new):
    ns, r, _ = su.shape
    cw = r * SSM_GROUP
    m, w_re, w_im, v_re, v_im, a_re, a_im = mats
    u = su.reshape(ns, r, SSM_G, SSM_GROUP).transpose(2, 0, 1, 3).reshape(SSM_G, ns, cw)
    gspec = lambda shape: pl.BlockSpec((None,) + shape, lambda g: (g,) + (0,) * len(shape))
    y, hre, him = pl.pallas_call(
        _s5_sample_kernel,
        out_shape=(jax.ShapeDtypeStruct((SSM_G, ns, cw), F32),
                   jax.ShapeDtypeStruct((SSM_G, ns, SSM_P), F32),
                   jax.ShapeDtypeStruct((SSM_G, ns, SSM_P), F32)),
        grid=(SSM_G,),
        in_specs=[gspec((ns, cw)), gspec((ns, SSM_P)), gspec((ns, SSM_P)), gspec((cw, cw)),
                  gspec((cw, SSM_P)), gspec((cw, SSM_P)), gspec((SSM_P, cw)), gspec((SSM_P, cw)),
                  gspec((1, SSM_P)), gspec((1, SSM_P))],
        out_specs=(gspec((ns, cw)), gspec((ns, SSM_P)), gspec((ns, SSM_P))),
        compiler_params=_params("parallel"),
        name="s5_sample",
    )(u, h0_re.transpose(1, 0, 2), h0_im.transpose(1, 0, 2), m, w_re, w_im, v_re, v_im,
      a_re[:, None, :], a_im[:, None, :])
    y = y.reshape(SSM_G, ns, r, SSM_GROUP)[:, :, :n_new].transpose(1, 2, 0, 3).reshape(ns * n_new, SSM_W)
    return y, hre.transpose(1, 0, 2), him.transpose(1, 0, 2)


def _glu_kernel(y_ref, w_ref, b_ref, g_ref, o_ref):
    sy = _gelu(y_ref[...])
    gate = _sigmoid(jnp.dot(sy.astype(BF16), w_ref[...], preferred_element_type=F32) + b_ref[...])
    o_ref[...] = _rms_norm(sy * gate, g_ref[...])


def _s5_glu(y, w_glu, b_glu, norm_g, tm):
    t = y.shape[0]
    row = lambda: pl.BlockSpec((1, SSM_W), lambda i: (0, 0))
    return pl.pallas_call(
        _glu_kernel,
        out_shape=jax.ShapeDtypeStruct((t, SSM_W), F32),
        grid=(t // tm,),
        in_specs=[pl.BlockSpec((tm, SSM_W), lambda i: (i, 0)),
                  pl.BlockSpec((SSM_W, SSM_W), lambda i: (0, 0)), row(), row()],
        out_specs=pl.BlockSpec((tm, SSM_W), lambda i: (i, 0)),
        compiler_params=_params("parallel"),
        name="s5_glu",
    )(y, w_glu.astype(BF16), b_glu.reshape(1, SSM_W), norm_g.reshape(1, SSM_W))


def _outproj_kernel(ro_ref, fo_ref, so_ref, x_ref, wr_ref, wf_ref, ws_ref, g_ref, b_ref, o_ref):
    mix = jnp.dot(ro_ref[...].astype(BF16), wr_ref[...], preferred_element_type=F32)
    mix += jnp.dot(fo_ref[...].astype(BF16), wf_ref[...], preferred_element_type=F32)
    mix += jnp.dot(so_ref[...].astype(BF16), ws_ref[...], preferred_element_type=F32)
    o_ref[...] = _layer_norm(DEEPNORM_ALPHA * x_ref[...] + mix, g_ref[...], b_ref[...])


def _outproj_ln(ro, fo, so, x, w_out, ln_g, ln_b, tm):
    t = x.shape[0]
    w = w_out.astype(BF16)
    tile = lambda n: pl.BlockSpec((tm, n), lambda i: (i, 0))
    full = lambda r, n: pl.BlockSpec((r, n), lambda i: (0, 0))
    return pl.pallas_call(
        _outproj_kernel,
        out_shape=jax.ShapeDtypeStruct((t, D_MODEL), F32),
        grid=(t // tm,),
        in_specs=[tile(RET_W), tile(FOX_W), tile(SSM_W), tile(D_MODEL),
                  full(RET_W, D_MODEL), full(FOX_W, D_MODEL), full(SSM_W, D_MODEL),
                  full(1, D_MODEL), full(1, D_MODEL)],
        out_specs=tile(D_MODEL),
        compiler_params=_params("parallel"),
        name="outproj_ln",
    )(ro, fo, so, x, w[:RET_W], w[RET_W:RET_W + FOX_W], w[RET_W + FOX_W:],
      ln_g.reshape(1, D_MODEL), ln_b.reshape(1, D_MODEL))


def _topk_rows(problems, k):
    def body(it, _):
        for s_ref, val_ref, idx_ref in problems:
            n = s_ref.shape[0]
            iota = lax.broadcasted_iota(jnp.int32, s_ref.shape, 0)
            s = s_ref[...]
            m = jnp.max(s, axis=0, keepdims=True)
            idx = jnp.min(jnp.where(s == m, iota, n), axis=0, keepdims=True)
            val_ref[pl.ds(it, 1), :] = m
            idx_ref[pl.ds(it, 1), :] = idx
            s_ref[...] = jnp.where(iota == idx, -jnp.inf, s)
        return 0

    lax.fori_loop(0, k, body, 0)


CAND_COUNTS = [PEER_TOPK // (a + 1) for a in range(PEER_TOPK)]
CAND_STARTS = [sum(CAND_COUNTS[:a]) for a in range(PEER_TOPK)]
N_CAND = sum(CAND_COUNTS)
N_CAND_PAD = -(-N_CAND // 8) * 8


def _peer_topk_kernel(q_ref, sk_ref, i_ref, j_ref, gate_ref, s0_sc, s1_sc, cand_sc, v0_sc, v1_sc, x0_sc, x1_sc,
                      top_sc, sel_sc):
    kk = PEER_TOPK
    s = lax.dot_general(sk_ref[...], q_ref[...], NT_DIMS, precision=HI, preferred_element_type=F32)
    s0_sc[...] = s[:PEER_NKEYS]
    s1_sc[...] = s[PEER_NKEYS:]
    _topk_rows([(s0_sc, v0_sc, x0_sc), (s1_sc, v1_sc, x1_sc)], kk)
    v1 = v1_sc[...]
    for a in range(kk):
        cand_sc[CAND_STARTS[a]:CAND_STARTS[a] + CAND_COUNTS[a], :] = v0_sc[a:a + 1, :] + v1[:CAND_COUNTS[a]]
    if N_CAND_PAD > N_CAND:
        cand_sc[N_CAND:, :] = jnp.full((N_CAND_PAD - N_CAND, cand_sc.shape[1]), -jnp.inf, F32)
    _topk_rows([(cand_sc, top_sc, sel_sc)], kk)
    sel = sel_sc[...]
    a_idx = jnp.zeros_like(sel)
    start = jnp.zeros_like(sel)
    for a in range(1, kk):
        past = sel >= CAND_STARTS[a]
        a_idx = jnp.where(past, a, a_idx)
        start = jnp.where(past, CAND_STARTS[a], start)
    b_idx = sel - start
    ii = jnp.zeros_like(sel)
    jj = jnp.zeros_like(sel)
    for a in range(kk):
        ii = jnp.where(a_idx == a, x0_sc[a:a + 1, :], ii)
        jj = jnp.where(b_idx == a, x1_sc[a:a + 1, :], jj)
    top = top_sc[...]
    e = jnp.exp(top - top[0:1, :])
    i_ref[...] = ii
    j_ref[...] = jj
    gate_ref[...] = e / jnp.sum(e, axis=0, keepdims=True)


def _peer_topk(q, sub_keys, tm):
    t = q.shape[0]
    half = PEER_DK // 2
    sk = jnp.zeros((PEER_HEADS, 2 * PEER_NKEYS, PEER_DK), F32)
    sk = sk.at[:, :PEER_NKEYS, :half].set(sub_keys[:, 0]).at[:, PEER_NKEYS:, half:].set(sub_keys[:, 1])
    kk = PEER_TOPK
    slot = lambda: pl.BlockSpec((kk, tm), lambda i, h: (h, i))
    return pl.pallas_call(
        _peer_topk_kernel,
        out_shape=(jax.ShapeDtypeStruct((PEER_HEADS * kk, t), jnp.int32),
                   jax.ShapeDtypeStruct((PEER_HEADS * kk, t), jnp.int32),
                   jax.ShapeDtypeStruct((PEER_HEADS * kk, t), F32)),
        grid=(t // tm, PEER_HEADS),
        in_specs=[pl.BlockSpec((tm, PEER_DK), lambda i, h: (i, h)),
                  pl.BlockSpec((None, 2 * PEER_NKEYS, PEER_DK), lambda i, h: (h, 0, 0))],
        out_specs=(slot(), slot(), slot()),
        scratch_shapes=[pltpu.VMEM((PEER_NKEYS, tm), F32), pltpu.VMEM((PEER_NKEYS, tm), F32),
                        pltpu.VMEM((N_CAND_PAD, tm), F32),
                        pltpu.VMEM((kk, tm), F32), pltpu.VMEM((kk, tm), F32),
                        pltpu.VMEM((kk, tm), jnp.int32), pltpu.VMEM((kk, tm), jnp.int32),
                        pltpu.VMEM((kk, tm), F32), pltpu.VMEM((kk, tm), jnp.int32)],
        compiler_params=_params("parallel", "arbitrary"),
        name="peer_topk",
    )(q, sk)


G_TOKENS = 128
G_ROW_STRIDE = G_TOKENS + 8
G_UNROLL = 32


def _peer_gates_kernel(i_ref, j_ref, gate_ref, o_ref, g_sc):
    tb, ns = i_ref.shape
    nk = PEER_NKEYS
    io = lax.broadcasted_iota(jnp.int32, (nk, ns), 0)

    def body(tt, carry):
        for u in range(G_UNROLL):
            t = tt * G_UNROLL + u
            irow = i_ref[pl.ds(t, 1), :]
            jrow = j_ref[pl.ds(t, 1), :]
            grow = gate_ref[pl.ds(t, 1), :]
            pt = jnp.where(io == irow, grow, 0.0).astype(BF16)
            qt = jnp.where(io == jrow, 1.0, 0.0).astype(BF16)
            g = lax.dot_general(pt, qt, NT_DIMS, preferred_element_type=F32)
            g_sc[pl.ds(t, nk, stride=G_ROW_STRIDE), :] = g
        return carry

    lax.fori_loop(0, tb // G_UNROLL, body, 0)
    for i in range(nk):
        o_ref[:, i * nk:(i + 1) * nk] = g_sc[i * G_ROW_STRIDE:i * G_ROW_STRIDE + tb, :].astype(BF16)


def _peer_gates(ii, jj, gate):
    t = ii.shape[0]
    tb = G_TOKENS
    ns = PEER_HEADS * PEER_TOPK
    tok = lambda: pl.BlockSpec((tb, ns), lambda i: (i, 0))
    return pl.pallas_call(
        _peer_gates_kernel,
        out_shape=jax.ShapeDtypeStruct((t, PEER_N), BF16),
        grid=(t // tb,),
        in_specs=[tok(), tok(), tok()],
        out_specs=pl.BlockSpec((tb, PEER_N), lambda i: (i, 0)),
        scratch_shapes=[pltpu.VMEM((PEER_NKEYS * G_ROW_STRIDE, PEER_NKEYS), F32)],
        compiler_params=_params("parallel"),
        name="peer_gates",
    )(ii, jj, gate)


def _peer_experts_kernel(x_ref, g_ref, u_ref, v_ref, lg_ref, lb_ref, o_ref, xb_sc, acc_sc):
    e = pl.program_id(1)

    @pl.when(e == 0)
    def _():
        xb_sc[...] = x_ref[...].astype(BF16)
        acc_sc[...] = jnp.zeros_like(acc_sc)

    h = lax.dot_general(xb_sc[...], u_ref[...], NT_DIMS, preferred_element_type=F32)
    w = (g_ref[...].astype(F32) * _gelu(h)).astype(BF16)
    acc_sc[...] += jnp.dot(w, v_ref[...], preferred_element_type=F32)

    @pl.when(e == pl.num_programs(1) - 1)
    def _():
        o_ref[...] = _layer_norm(DEEPNORM_ALPHA * x_ref[...] + acc_sc[...], lg_ref[...], lb_ref[...])


def _peer_experts_ln(x, g, u_tabs, v_tabs, layer, ln_g, ln_b, tm, nb):
    t = x.shape[0]
    return pl.pallas_call(
        _peer_experts_kernel,
        out_shape=jax.ShapeDtypeStruct((t, D_MODEL), F32),
        grid=(t // tm, PEER_N // nb),
        in_specs=[pl.BlockSpec((tm, D_MODEL), lambda i, e: (i, 0)),
                  pl.BlockSpec((tm, nb), lambda i, e: (i, e)),
                  pl.BlockSpec((None, nb, D_MODEL), lambda i, e: (layer, e, 0)),
                  pl.BlockSpec((None, nb, D_MODEL), lambda i, e: (layer, e, 0)),
                  pl.BlockSpec((1, D_MODEL), lambda i, e: (0, 0)),
                  pl.BlockSpec((1, D_MODEL), lambda i, e: (0, 0))],
        out_specs=pl.BlockSpec((tm, D_MODEL), lambda i, e: (i, 0)),
        scratch_shapes=[pltpu.VMEM((tm, D_MODEL), BF16), pltpu.VMEM((tm, D_MODEL), F32)],
        compiler_params=_params("parallel", "arbitrary"),
        name="peer_experts_ln",
    )(x, g, u_tabs, v_tabs, ln_g.reshape(1, D_MODEL), ln_b.reshape(1, D_MODEL))


def _ple_kernel(x_ref, xn_ref, p_ref, wg_ref, bg_ref, we_ref, o_ref):
    gate = _sigmoid(jnp.dot(x_ref[...].astype(BF16), wg_ref[...], preferred_element_type=F32) + bg_ref[...])
    emb = jnp.dot(p_ref[...].astype(BF16), we_ref[...], preferred_element_type=F32)
    o_ref[...] = xn_ref[...] + gate * emb


def _ple(x, p, w_pg, b_pg, w_pe, tm, tn):
    t = x.shape[0]
    return pl.pallas_call(
        _ple_kernel,
        out_shape=jax.ShapeDtypeStruct((t, D_MODEL), F32),
        grid=(t // tm, D_MODEL // tn),
        in_specs=[pl.BlockSpec((tm, D_MODEL), lambda i, j: (i, 0)),
                  pl.BlockSpec((tm, tn), lambda i, j: (i, j)),
                  pl.BlockSpec((tm, PLE_DIM), lambda i, j: (i, 0)),
                  pl.BlockSpec((D_MODEL, tn), lambda i, j: (0, j)),
                  pl.BlockSpec((1, tn), lambda i, j: (0, j)),
                  pl.BlockSpec((PLE_DIM, tn), lambda i, j: (0, j))],
        out_specs=pl.BlockSpec((tm, tn), lambda i, j: (i, j)),
        compiler_params=_params("parallel", "arbitrary"),
        name="ple_gate",
    )(x, x, p, w_pg.astype(BF16), b_pg.reshape(1, D_MODEL), w_pe.astype(BF16))


def _split_bf16(w):
    hi = w.astype(BF16)
    return hi, (w - hi.astype(F32)).astype(BF16)


def _layer(x, p, lw, caches, peer_tabs, page_table, layer, nb, seq, ns, n_new, tiles):
    t_all = x.shape[0]
    tp = nb * seq
    r = 8
    cache_k, cache_v, lf_sums, state_ret, state_re, state_im = caches
    past_len = page_table.shape[1] * (cache_k.shape[2] // H_FOX)

    widths = [RET_W] * 4 + [FOX_W] * 3 + [H_FOX, SSM_W]
    offs = [0]
    for wd in widths:
        offs.append(offs[-1] + wd)
    w_in = lw['w_in']
    w_pack = jnp.concatenate([w_in[:, :offs[7]], w_in[:, offs[8]:]], axis=1).astype(BF16)
    w_ff = w_in[:, offs[7]:offs[8]]

    z = _matmul(x, (w_pack,), tiles['tm'], 512, "in_proj")
    lf = _forget_gates(x, w_ff, lw['b_fox_f'], tiles['tm'])
    zs = jnp.pad(z[tp:].reshape(ns, n_new, Z_W), ((0, 0), (0, r - n_new), (0, 0)))

    ro, ret_p = _retention_prompt(z, lw['ret_gn_g'], nb, seq, t_all)
    ro_s, ret_s = _retention_sample(zs, state_ret, layer, lw['ret_gn_g'], past_len, n_new)
    ro = ro.at[tp:].set(ro_s[:, :n_new].reshape(ns * n_new, RET_W))

    fo = _fox_prompt(z, _forget_cumsum(lf, nb, seq), lw['fox_norm_g'], nb, seq, t_all, tiles['attn'])
    lf_s = lf[:, tp:].T.reshape(ns, n_new, H_FOX)
    lfn = jnp.pad(lf_s, ((0, 0), (0, r - n_new), (0, 0))).reshape(ns, 1, r * H_FOX)
    fo_s = _fox_sample(z[tp:, Z_FQ:Z_FQ + FOX_W].reshape(ns, n_new * H_FOX, DH_FOX),
                       zs[:, :, Z_FK:Z_FK + FOX_W].reshape(ns, r * H_FOX, DH_FOX),
                       zs[:, :, Z_FV:Z_FV + FOX_W].reshape(ns, r * H_FOX, DH_FOX),
                       lfn, cache_k, cache_v, lf_sums, page_table, layer, lw['fox_norm_g'])
    fo = fo.at[tp:].set(fo_s.reshape(ns * n_new, FOX_W))

    ssm = (lw['ssm_a_re'], lw['ssm_a_im'], lw['ssm_b_re'], lw['ssm_b_im'], lw['ssm_c_re'], lw['ssm_c_im'],
           lw['ssm_d'], lw['ssm_log_dt'])
    y_p, hre_p, him_p = _s5_prompt(z[:tp, Z_SU:], _s5_mats(*ssm, SSM_GROUP, SSM_GROUP), nb, seq)
    y_s, hre_s, him_s = _s5_sample(zs[:, :, Z_SU:], state_re[layer], state_im[layer], _s5_mats(*ssm, r, n_new), n_new)
    so = _s5_glu(jnp.concatenate([y_p, y_s], 0), lw['w_glu'], lw['b_glu'], lw['ssm_norm_g'], tiles['tm'])

    x1 = _outproj_ln(ro, fo, so, x, lw['w_out'], lw['ln1_g'], lw['ln1_b'], tiles['tm_ln'])

    q = _matmul(x1, _split_bf16(lw['peer_w_q']), tiles['tm'], 512, "peer_query")
    ii, jj, gate = _peer_topk(q, lw['peer_sub_keys'], tiles['tm'])
    g = _peer_gates(ii.T, jj.T, gate.T)
    x2 = _peer_experts_ln(x1, g, peer_tabs[0], peer_tabs[1], layer, lw['ln2_g'], lw['ln2_b'], tiles['tm'], 512)

    x3 = _ple(x2, p, lw['w_pg'], lw['b_pg'], lw['w_pe'], tiles['tm'], 512)

    st_p = (z[:tp, Z_FK:Z_FK + FOX_W].reshape(nb, seq, H_FOX, DH_FOX),
            z[:tp, Z_FV:Z_FV + FOX_W].reshape(nb, seq, H_FOX, DH_FOX),
            lf[:, :tp].T.reshape(nb, seq, H_FOX), ret_p, hre_p, him_p)
    st_s = (z[tp:, Z_FK:Z_FK + FOX_W].reshape(ns, n_new, H_FOX, DH_FOX),
            z[tp:, Z_FV:Z_FV + FOX_W].reshape(ns, n_new, H_FOX, DH_FOX),
            lf_s, ret_s, hre_s, him_s)
    return x3, st_p, st_s


def kernel(x_prompt, x_sample, p_prompt, p_sample, cache_fox_k, cache_fox_v, cache_fox_lf, state_ret, state_ssm_re, state_ssm_im, page_table, w_in, w_out, ret_gn_g, b_fox_f, fox_norm_g, ssm_a_re, ssm_a_im, ssm_b_re, ssm_b_im, ssm_c_re, ssm_c_im, ssm_d, ssm_log_dt, w_glu, b_glu, ssm_norm_g, ln1_g, ln1_b, ln2_g, ln2_b, peer_w_q, peer_sub_keys, peer_u, peer_v, w_pe, w_pg, b_pg):
    nb, seq, d = x_prompt.shape
    ns, n_new, _ = x_sample.shape
    tp, ts = nb * seq, ns * n_new
    t_all = tp + ts
    tm = 640 if t_all % 640 == 0 else 128
    tiles = dict(tm=tm, tm_ln=tm // 2 if tm == 640 else tm, attn=256 if seq % 256 == 0 else 128)
    x = jnp.concatenate([x_prompt.reshape(tp, d), x_sample.reshape(ts, d)], 0)
    depth, n_pool, pg, nh, dh = cache_fox_k.shape
    caches = (cache_fox_k.reshape(depth, n_pool, pg * nh, dh), cache_fox_v.reshape(depth, n_pool, pg * nh, dh),
              _lf_prep(cache_fox_lf), state_ret, state_ssm_re, state_ssm_im)
    weights = dict(w_in=w_in, w_out=w_out, ret_gn_g=ret_gn_g, b_fox_f=b_fox_f, fox_norm_g=fox_norm_g,
                   ssm_a_re=ssm_a_re, ssm_a_im=ssm_a_im, ssm_b_re=ssm_b_re, ssm_b_im=ssm_b_im,
                   ssm_c_re=ssm_c_re, ssm_c_im=ssm_c_im, ssm_d=ssm_d, ssm_log_dt=ssm_log_dt,
                   w_glu=w_glu, b_glu=b_glu, ssm_norm_g=ssm_norm_g, ln1_g=ln1_g, ln1_b=ln1_b,
                   ln2_g=ln2_g, ln2_b=ln2_b, peer_w_q=peer_w_q, peer_sub_keys=peer_sub_keys,
                   w_pe=w_pe, w_pg=w_pg, b_pg=b_pg)
    peer_tabs = (peer_u.astype(BF16), peer_v.astype(BF16))
    sts_p, sts_s = [], []
    for l in range(DEPTH):
        lw = {k: v[l] for k, v in weights.items()}
        p = jnp.concatenate([p_prompt[l].reshape(tp, PLE_DIM), p_sample[l].reshape(ts, PLE_DIM)], 0)
        x, st_p, st_s = _layer(x, p, lw, caches, peer_tabs, page_table, l, nb, seq, ns, n_new, tiles)
        sts_p.append(st_p)
        sts_s.append(st_s)
    stack = lambda sts, i: jnp.stack([s[i] for s in sts])
    return ((x[:tp].reshape(nb, seq, d), x[tp:].reshape(ns, n_new, d))
            + tuple(stack(sts_p, i) for i in range(6)) + tuple(stack(sts_s, i) for i in range(6)))
```

```python
import functools
import math

import jax
import jax.numpy as jnp
from jax import lax
from jax.experimental import pallas as pl
from jax.experimental.pallas import tpu as pltpu

F32 = jnp.float32
BF16 = jnp.bfloat16
HI = lax.Precision.HIGHEST

D_MODEL = 2048
DEPTH = 2
H_RET, DH_RET = 4, 128
RET_W = H_RET * DH_RET
H_FOX, DH_FOX = 8, 128
FOX_W = H_FOX * DH_FOX
SSM_W, SSM_GROUP = 512, 16
SSM_G = SSM_W // SSM_GROUP
SSM_P = 64
PEER_HEADS, PEER_NKEYS, PEER_DK, PEER_TOPK = 8, 128, 128, 16
PEER_N = PEER_NKEYS * PEER_NKEYS
PLE_DIM = 256
BLOCK = 128
ROPE_BASE = 10000.0
DEEPNORM_ALPHA = (2 * DEPTH) ** 0.25
EPS = 1e-5

Z_FQ, Z_FK, Z_FV, Z_FF = 4 * RET_W, 4 * RET_W + FOX_W, 4 * RET_W + 2 * FOX_W, 4 * RET_W + 3 * FOX_W
Z_SU = Z_FF + H_FOX
Z_W = Z_SU + SSM_W

VMEM_LIMIT = 56 * 1024 * 1024
NT_DIMS = (((1,), (1,)), ((), ()))
TN_DIMS = (((0,), (0,)), ((), ()))


def _params(*sem):
    return pltpu.CompilerParams(dimension_semantics=sem, vmem_limit_bytes=VMEM_LIMIT)


def _gelu(x):
    return 0.5 * x * (1.0 + jnp.tanh(math.sqrt(2.0 / math.pi) * (x + 0.044715 * (x * x * x))))


def _sigmoid(x):
    return 1.0 / (1.0 + jnp.exp(-x))


def _log_sigmoid(x):
    return jnp.minimum(x, 0.0) - jnp.log1p(jnp.exp(-jnp.abs(x)))


def _layer_norm(y, g, b):
    mu = jnp.mean(y, -1, keepdims=True)
    var = jnp.mean(jnp.square(y - mu), -1, keepdims=True)
    return (y - mu) * lax.rsqrt(var + EPS) * g + b


def _rms_norm(y, g):
    return y * lax.rsqrt(jnp.mean(y * y, -1, keepdims=True) + EPS) * g


def _mm_kernel(x_ref, w_ref, o_ref):
    o_ref[...] = jnp.dot(x_ref[...].astype(BF16), w_ref[...], preferred_element_type=F32)


def _mm3_kernel(x_ref, wh_ref, wl_ref, o_ref):
    x = x_ref[...]
    xh = x.astype(BF16)
    xl = (x - xh.astype(F32)).astype(BF16)
    wh = wh_ref[...]
    acc = jnp.dot(xh, wh, preferred_element_type=F32)
    acc += jnp.dot(xl, wh, preferred_element_type=F32)
    acc += jnp.dot(xh, wl_ref[...], preferred_element_type=F32)
    o_ref[...] = acc


def _matmul(x, ws, tm, tn, name):
    t, k = x.shape
    n = ws[0].shape[1]
    kern = _mm_kernel if len(ws) == 1 else _mm3_kernel
    return pl.pallas_call(
        kern,
        out_shape=jax.ShapeDtypeStruct((t, n), F32),
        grid=(t // tm, n // tn),
        in_specs=[pl.BlockSpec((tm, k), lambda i, j: (i, 0))]
        + [pl.BlockSpec((k, tn), lambda i, j: (0, j)) for _ in ws],
        out_specs=pl.BlockSpec((tm, tn), lambda i, j: (i, j)),
        compiler_params=_params("parallel", "arbitrary"),
        name=name,
    )(x, *ws)


def _mm_w32_kernel(x_ref, w_ref, o_ref):
    o_ref[...] = jnp.dot(x_ref[...].astype(BF16), w_ref[...].astype(BF16), preferred_element_type=F32)


def _in_proj(x, w_all, layer, tm, tn):
    t, k = x.shape
    n = w_all.shape[2]
    return pl.pallas_call(
        _mm_w32_kernel,
        out_shape=jax.ShapeDtypeStruct((t, n), F32),
        grid=(t // tm, pl.cdiv(n, tn)),
        in_specs=[pl.BlockSpec((tm, k), lambda i, j: (i, 0)),
                  pl.BlockSpec((None, k, tn), lambda i, j: (layer, 0, j))],
        out_specs=pl.BlockSpec((tm, tn), lambda i, j: (i, j)),
        compiler_params=_params("parallel", "arbitrary"),
        name="in_proj",
    )(x, w_all)


def _gate_kernel(x_ref, w_ref, b_ref, lf_ref):
    f = lax.dot_general(w_ref[...], x_ref[...], NT_DIMS, precision=HI, preferred_element_type=F32)
    lf_ref[...] = _log_sigmoid(f + b_ref[...])


def _forget_gates(x, w_ff, b_ff, tm):
    t, k = x.shape
    return pl.pallas_call(
        _gate_kernel,
        out_shape=jax.ShapeDtypeStruct((H_FOX, t), F32),
        grid=(t // tm,),
        in_specs=[
            pl.BlockSpec((tm, k), lambda i: (i, 0)),
            pl.BlockSpec((H_FOX, k), lambda i: (0, 0)),
            pl.BlockSpec((H_FOX, 1), lambda i: (0, 0)),
        ],
        out_specs=pl.BlockSpec((H_FOX, tm), lambda i: (0, i)),
        compiler_params=_params("parallel"),
        name="forget_gates",
    )(x, w_ff.T, b_ff.reshape(H_FOX, 1))


def _cumsum_kernel(lf_ref, c_ref, carry):
    @pl.when(pl.program_id(1) == 0)
    def _():
        carry[...] = jnp.zeros_like(carry)

    n = lf_ref.shape[1]
    r = lax.broadcasted_iota(jnp.int32, (n, n), 0)
    c = lax.broadcasted_iota(jnp.int32, (n, n), 1)
    upper = (r <= c).astype(F32)
    cs = jnp.dot(lf_ref[...], upper, precision=HI, preferred_element_type=F32) + carry[...]
    c_ref[...] = cs
    carry[...] = cs[:, n - 1:n]


def _forget_cumsum(lf, nb, seq):
    n = 256 if seq % 256 == 0 else 128
    nc = seq // n
    return pl.pallas_call(
        _cumsum_kernel,
        out_shape=jax.ShapeDtypeStruct((H_FOX, nb * seq), F32),
        grid=(nb, nc),
        in_specs=[pl.BlockSpec((H_FOX, n), lambda b, i: (0, b * nc + i))],
        out_specs=pl.BlockSpec((H_FOX, n), lambda b, i: (0, b * nc + i)),
        scratch_shapes=[pltpu.VMEM((H_FOX, 1), F32)],
        compiler_params=_params("parallel", "arbitrary"),
        name="forget_cumsum",
    )(lf)


def _fox_prompt_kernel(q_ref, k_ref, v_ref, c_ref, g_ref, o_ref, m_sc, l_sc, acc_sc, *, scale):
    qi = pl.program_id(1)
    ki = pl.program_id(2)
    bq = q_ref.shape[0]
    bk = k_ref.shape[0]

    @pl.when(ki == 0)
    def _():
        m_sc[...] = jnp.full_like(m_sc, -jnp.inf)
        l_sc[...] = jnp.zeros_like(l_sc)
        acc_sc[...] = jnp.zeros_like(acc_sc)

    last = (qi * bq + bq - 1) // bk

    def attend(diagonal):
        if diagonal:
            causal = (ki * bk + lax.broadcasted_iota(jnp.int32, (bq, bk), 1)
                      <= qi * bq + lax.broadcasted_iota(jnp.int32, (bq, bk), 0))
        for h in range(H_FOX):
            sl = slice(h * DH_FOX, (h + 1) * DH_FOX)
            q = q_ref[:, sl].astype(BF16)
            k = k_ref[:, sl].astype(BF16)
            s = lax.dot_general(q, k, NT_DIMS, preferred_element_type=F32) * scale - c_ref[h:h + 1, :]
            if diagonal:
                s = jnp.where(causal, s, -jnp.inf)
            m_prev = m_sc[h]
            m_new = jnp.maximum(m_prev, jnp.max(s, axis=1, keepdims=True))
            alpha = jnp.exp(m_prev - m_new)
            p = jnp.exp(s - jnp.tile(m_new, (1, bk // 128)))
            l_sc[h] = alpha * l_sc[h] + jnp.sum(p, axis=1, keepdims=True)
            acc_sc[:, sl] = alpha * acc_sc[:, sl] + jnp.dot(
                p.astype(BF16), v_ref[:, sl].astype(BF16), preferred_element_type=F32)
            m_sc[h] = m_new

    @pl.when(ki < last)
    def _():
        attend(False)

    @pl.when(ki == last)
    def _():
        attend(True)
        for h in range(H_FOX):
            sl = slice(h * DH_FOX, (h + 1) * DH_FOX)
            acc_sc[:, sl] = acc_sc[:, sl] / l_sc[h]
        o_ref[...] = _rms_norm(acc_sc[...], g_ref[...])


def _fox_prompt(z, c, norm_g, nb, seq, t_all, blk):
    bk = blk
    nq, nk = seq // blk, seq // bk
    cq, ck, cv = Z_FQ // FOX_W, Z_FK // FOX_W, Z_FV // FOX_W
    kern = functools.partial(_fox_prompt_kernel, scale=DH_FOX ** -0.5)

    def key_block(b, i, j):
        return b * nk + jnp.minimum(j, (i * blk + blk - 1) // bk)

    return pl.pallas_call(
        kern,
        out_shape=jax.ShapeDtypeStruct((t_all, FOX_W), F32),
        grid=(nb, nq, nk),
        in_specs=[
            pl.BlockSpec((blk, FOX_W), lambda b, i, j: (b * nq + i, cq)),
            pl.BlockSpec((bk, FOX_W), lambda b, i, j: (key_block(b, i, j), ck)),
            pl.BlockSpec((bk, FOX_W), lambda b, i, j: (key_block(b, i, j), cv)),
            pl.BlockSpec((H_FOX, bk), lambda b, i, j: (0, key_block(b, i, j))),
            pl.BlockSpec((1, FOX_W), lambda b, i, j: (0, 0)),
        ],
        out_specs=pl.BlockSpec((blk, FOX_W), lambda b, i, j: (b * nq + i, 0)),
        scratch_shapes=[
            pltpu.VMEM((H_FOX, blk, 128), F32),
            pltpu.VMEM((H_FOX, blk, 128), F32),
            pltpu.VMEM((blk, FOX_W), F32),
        ],
        compiler_params=_params("parallel", "parallel", "arbitrary"),
        name="fox_prompt",
    )(z, z, z, c, norm_g.reshape(1, FOX_W))


def _lf_prep_kernel(lf_ref, w_ref, o_ref):
    x = lf_ref[...]
    p1 = x.astype(BF16)
    r1 = x - p1.astype(F32)
    p2 = r1.astype(BF16)
    p3 = (r1 - p2.astype(F32)).astype(BF16)
    w = w_ref[...]
    acc = jnp.dot(p3, w, preferred_element_type=F32)
    acc += jnp.dot(p2, w, preferred_element_type=F32)
    acc += jnp.dot(p1, w, preferred_element_type=F32)
    o_ref[...] = acc


def _lf_prep(cache_lf):
    depth, n_pool, pg, nh = cache_lf.shape
    c = pg * nh
    rows = depth * n_pool
    tm = max(d for d in (256, 128, 64, 32, 16, 8) if rows % d == 0)
    src = jnp.arange(c)[:, None]
    dst = jnp.arange(c)[None, :]
    same_head = (src % nh) == (dst % nh)
    w = jnp.concatenate([same_head & (src // nh > dst // nh), same_head], axis=1).astype(BF16)
    return pl.pallas_call(
        _lf_prep_kernel,
        out_shape=jax.ShapeDtypeStruct((rows, 2 * c), F32),
        grid=(rows // tm,),
        in_specs=[pl.BlockSpec((tm, c), lambda i: (i, 0)), pl.BlockSpec((c, 2 * c), lambda i: (0, 0))],
        out_specs=pl.BlockSpec((tm, 2 * c), lambda i: (i, 0)),
        compiler_params=_params("parallel"),
        name="lf_prep",
    )(cache_lf.reshape(rows, c), w)


def _fox_sample_kernel(pt_ref, q_ref, kn_ref, vn_ref, lfn_ref, g_ref, *rest, scale, npg, page_of):
    k_refs, v_refs, x_refs = rest[:npg], rest[npg:2 * npg], rest[2 * npg:3 * npg]
    o_ref, m_sc, l_sc, acc_sc, carry_sc = rest[3 * npg:]
    step = pl.program_id(1)
    nr = q_ref.shape[0]
    ncol = k_refs[0].shape[0]
    head_mask = H_FOX - 1
    head_bits = H_FOX.bit_length() - 1

    @pl.when(step == 0)
    def _():
        m_sc[...] = jnp.full_like(m_sc, -jnp.inf)
        l_sc[...] = jnp.zeros_like(l_sc)
        acc_sc[...] = jnp.zeros_like(acc_sc)
        carry_sc[...] = jnp.zeros_like(carry_sc)

    def equal_heads(ncol):
        rows = lax.broadcasted_iota(jnp.int32, (nr, ncol), 0)
        cols = lax.broadcasted_iota(jnp.int32, (nr, ncol), 1)
        return rows, cols, (rows & head_mask) == (cols & head_mask)

    q = q_ref[...].astype(BF16)
    _, _, same_head = equal_heads(ncol)
    carry = carry_sc[...]
    m_prev = m_sc[...]
    m_new = m_prev
    scores = []
    for i in range(npg):
        x = x_refs[i][pl.ds(page_of(pl.program_id(0), step, i, pt_ref) % 8, 1), :]
        bias = x[:, :ncol] + carry
        carry = carry + x[:, ncol:]
        s = lax.dot_general(q, k_refs[i][...].astype(BF16), NT_DIMS, preferred_element_type=F32) * scale + bias
        s = jnp.where(same_head, s, -jnp.inf)
        m_new = jnp.maximum(m_new, jnp.max(s, axis=1, keepdims=True))
        scores.append(s)
    carry_sc[...] = carry
    alpha = jnp.exp(m_prev - m_new)
    l_new = alpha * l_sc[...]
    acc = alpha * acc_sc[...]
    for i in range(npg):
        p = jnp.exp(scores[i] - m_new)
        l_new += jnp.sum(p, axis=1, keepdims=True)
        acc += jnp.dot(p.astype(BF16), v_refs[i][...].astype(BF16), preferred_element_type=F32)
    m_sc[...] = m_new
    l_sc[...] = l_new
    acc_sc[...] = acc

    @pl.when(step == pl.num_programs(1) - 1)
    def _():
        nc = kn_ref.shape[0]
        rows, cols, same = equal_heads(nc)
        causal = (cols >> head_bits) <= (rows >> head_bits)
        r2 = lax.broadcasted_iota(jnp.int32, (nc, nc), 0)
        c2 = lax.broadcasted_iota(jnp.int32, (nc, nc), 1)
        prefix = jnp.where((r2 & head_mask) == (c2 & head_mask), jnp.where(r2 <= c2, 1.0, 0.0), 0.0)
        cn = jnp.dot(jnp.broadcast_to(lfn_ref[...], (8, nc)), prefix, precision=HI,
                     preferred_element_type=F32)[0:1]
        s = lax.dot_general(q, kn_ref[...].astype(BF16), NT_DIMS, preferred_element_type=F32) * scale - cn
        s = jnp.where(same, jnp.where(causal, s, -jnp.inf), -jnp.inf)
        m_fin = jnp.maximum(m_new, jnp.max(s, axis=1, keepdims=True))
        a_fin = jnp.exp(m_new - m_fin)
        p = jnp.exp(s - m_fin)
        l_fin = a_fin * l_new + jnp.sum(p, axis=1, keepdims=True)
        out = (a_fin * acc + jnp.dot(p.astype(BF16), vn_ref[...].astype(BF16),
                                     preferred_element_type=F32)) / l_fin
        r3 = lax.broadcasted_iota(jnp.int32, (nr, nr), 0)
        c3 = lax.broadcasted_iota(jnp.int32, (nr, nr), 1)
        same_token = jnp.where((r3 >> head_bits) == (c3 >> head_bits), 1.0, 0.0)
        ssq = jnp.sum(jnp.dot(same_token, out * out, precision=HI, preferred_element_type=F32),
                      axis=1, keepdims=True)
        o_ref[...] = out * lax.rsqrt(ssq / (H_FOX * DH_FOX) + EPS) * g_ref[...]


def _fox_sample(q, kn, vn, lfn, cache_k, cache_v, lf_sums, page_table, layer, norm_g):
    ns, nr, _ = q.shape
    nc = kn.shape[1]
    n_pages = page_table.shape[1]
    n_pool, cols = cache_k.shape[1], cache_k.shape[2]
    assert n_pool % 8 == 0 and lf_sums.shape == (cache_k.shape[0] * n_pool, 2 * cols)
    npg = max(d for d in range(1, 9) if n_pages % d == 0)
    g = jnp.tile(norm_g.reshape(H_FOX, DH_FOX), (nr // H_FOX, 1))

    def page_of(b, s, i, pt):
        return pt[b * n_pages + (n_pages - 1 - (s * npg + i))]

    kern = functools.partial(_fox_sample_kernel, scale=DH_FOX ** -0.5, npg=npg, page_of=page_of)

    def paged(i):
        return pl.BlockSpec((None, None, cols, DH_FOX), lambda b, s, pt: (layer, page_of(b, s, i, pt), 0, 0))

    def paged_sums(i):
        return pl.BlockSpec((8, 2 * cols), lambda b, s, pt: ((layer * n_pool + page_of(b, s, i, pt)) // 8, 0))

    grid_spec = pltpu.PrefetchScalarGridSpec(
        num_scalar_prefetch=1,
        grid=(ns, n_pages // npg),
        in_specs=[
            pl.BlockSpec((None, nr, DH_FOX), lambda b, s, pt: (b, 0, 0)),
            pl.BlockSpec((None, nc, DH_FOX), lambda b, s, pt: (b, 0, 0)),
            pl.BlockSpec((None, nc, DH_FOX), lambda b, s, pt: (b, 0, 0)),
            pl.BlockSpec((None, 1, nc), lambda b, s, pt: (b, 0, 0)),
            pl.BlockSpec((nr, DH_FOX), lambda b, s, pt: (0, 0)),
        ] + [paged(i) for i in range(npg)] * 2 + [paged_sums(i) for i in range(npg)],
        out_specs=pl.BlockSpec((None, nr, DH_FOX), lambda b, s, pt: (b, 0, 0)),
        scratch_shapes=[
            pltpu.VMEM((nr, 1), F32),
            pltpu.VMEM((nr, 1), F32),
            pltpu.VMEM((nr, DH_FOX), F32),
            pltpu.VMEM((1, cols), F32),
        ],
    )
    return pl.pallas_call(
        kern,
        out_shape=jax.ShapeDtypeStruct((ns, nr, DH_FOX), F32),
        grid_spec=grid_spec,
        compiler_params=_params("parallel", "arbitrary"),
        name="fox_sample",
    )(page_table.reshape(-1), q, kn, vn, lfn, g,
      *([cache_k] * npg), *([cache_v] * npg), *([lf_sums] * npg))


def _ret_kernel(*refs, has_s0):
    if has_s0:
        (q_ref, k_ref, v_ref, gt_ref, cos_ref, sin_ref, dm_ref, xi_ref, zt_ref, gc_ref, gn_ref, s0_ref,
         o_ref, s_ref) = refs
    else:
        (q_ref, k_ref, v_ref, gt_ref, cos_ref, sin_ref, dm_ref, xi_ref, zt_ref, gc_ref, gn_ref,
         o_ref, s_ref) = refs

    @pl.when(pl.program_id(1) == 0)
    def _():
        if has_s0:
            s_ref[...] = s0_ref[...]
        else:
            s_ref[...] = jnp.zeros_like(s_ref)

    cos = cos_ref[...]
    sin = sin_ref[...]

    def rot(x):
        return x * cos + pltpu.roll(x, DH_RET // 2, 1) * sin

    for h in range(H_RET):
        sl = slice(h * DH_RET, (h + 1) * DH_RET)
        q = rot(q_ref[:, sl])
        k = rot(k_ref[:, sl]) * DH_RET ** -0.5
        v = v_ref[:, sl]
        s = s_ref[h]
        att = lax.dot_general(q, k, NT_DIMS, preferred_element_type=F32) * dm_ref[h]
        o = jnp.dot(att, v, preferred_element_type=F32) + jnp.dot(q * xi_ref[h], s, preferred_element_type=F32)
        s_ref[h] = gc_ref[h] * s + lax.dot_general(k * zt_ref[h], v, TN_DIMS, preferred_element_type=F32)
        mu = jnp.mean(o, -1, keepdims=True)
        var = jnp.mean(jnp.square(o - mu), -1, keepdims=True)
        on = (o - mu) * lax.rsqrt(var + EPS)
        gt = gt_ref[:, sl]
        o_ref[:, sl] = on * gn_ref[:, sl] * (gt * _sigmoid(gt))


def _ret_consts(c, c_eff):
    lg = jnp.log1p(-jnp.exp2(-5.0 - jnp.arange(H_RET, dtype=F32)))
    idx = jnp.arange(c, dtype=F32)
    rel = idx[:, None] - idx[None, :]
    dmask = jnp.where(rel >= 0, jnp.exp(lg[:, None, None] * jnp.maximum(rel, 0.0)), 0.0)
    xi = jnp.exp(lg[:, None] * (idx + 1.0))[..., None]
    zeta = jnp.exp(lg[:, None] * (c_eff - 1.0 - idx))[..., None]
    g_c = jnp.broadcast_to(jnp.exp(lg * c_eff)[:, None, None], (H_RET, 1, DH_RET))
    return dmask, xi, zeta, g_c


def _rope_tables(pos):
    inv = ROPE_BASE ** (-jnp.arange(0, DH_RET, 2, dtype=F32) / DH_RET)
    ang = pos.astype(F32)[:, None] * inv[None, :]
    cos, sin = jnp.cos(ang), jnp.sin(ang)
    return jnp.concatenate([cos, cos], -1), jnp.concatenate([-sin, sin], -1)


def _retention_prompt(z, gn_g, nb, seq, t_all):
    c = BLOCK
    nc = seq // c
    dmask, xi, zeta, g_c = _ret_consts(c, c)
    cos, sin = _rope_tables(jnp.arange(seq))

    def zspec(col):
        return pl.BlockSpec((c, RET_W), lambda b, i: (b * nc + i, col))

    full = lambda shape: pl.BlockSpec(shape, lambda b, i: (0,) * len(shape))
    return pl.pallas_call(
        functools.partial(_ret_kernel, has_s0=False),
        out_shape=(jax.ShapeDtypeStruct((t_all, RET_W), F32),
                   jax.ShapeDtypeStruct((nb, H_RET, DH_RET, DH_RET), F32)),
        grid=(nb, nc),
        in_specs=[zspec(0), zspec(1), zspec(2), zspec(3),
                  pl.BlockSpec((c, DH_RET), lambda b, i: (i, 0)),
                  pl.BlockSpec((c, DH_RET), lambda b, i: (i, 0)),
                  full((H_RET, c, c)), full((H_RET, c, 1)), full((H_RET, c, 1)), full((H_RET, 1, DH_RET)),
                  full((1, RET_W))],
        out_specs=(pl.BlockSpec((c, RET_W), lambda b, i: (b * nc + i, 0)),
                   pl.BlockSpec((None, H_RET, DH_RET, DH_RET), lambda b, i: (b, 0, 0, 0))),
        compiler_params=_params("parallel", "arbitrary"),
        name="retention_prompt",
    )(z, z, z, z, cos, sin, dmask, xi, zeta, g_c, gn_g.reshape(1, RET_W))


def _retention_sample(zs, state_ret, layer, gn_g, pos0, n_new):
    ns, r, _ = zs.shape
    dmask, xi, zeta, g_c = _ret_consts(r, n_new)
    cos, sin = _rope_tables(pos0 + jnp.arange(r))

    def zspec(col):
        return pl.BlockSpec((None, r, RET_W), lambda b, i: (b, 0, col))

    full = lambda shape: pl.BlockSpec(shape, lambda b, i: (0,) * len(shape))
    return pl.pallas_call(
        functools.partial(_ret_kernel, has_s0=True),
        out_shape=(jax.ShapeDtypeStruct((ns, r, RET_W), F32),
                   jax.ShapeDtypeStruct((ns, H_RET, DH_RET, DH_RET), F32)),
        grid=(ns, 1),
        in_specs=[zspec(0), zspec(1), zspec(2), zspec(3),
                  full((r, DH_RET)), full((r, DH_RET)),
                  full((H_RET, r, r)), full((H_RET, r, 1)), full((H_RET, r, 1)), full((H_RET, 1, DH_RET)),
                  full((1, RET_W)),
                  pl.BlockSpec((None, None, H_RET, DH_RET, DH_RET), lambda b, i: (layer, b, 0, 0, 0))],
        out_specs=(pl.BlockSpec((None, r, RET_W), lambda b, i: (b, 0, 0)),
                   pl.BlockSpec((None, H_RET, DH_RET, DH_RET), lambda b, i: (b, 0, 0, 0))),
        compiler_params=_params("parallel", "arbitrary"),
        name="retention_sample",
    )(zs, zs, zs, zs, cos, sin, dmask, xi, zeta, g_c, gn_g.reshape(1, RET_W), state_ret)


def _s5_mats(a_re, a_im, b_re, b_im, c_re, c_im, d_skip, log_dt, c, c_eff):
    lam = lax.complex(a_re, a_im)
    lam_dt = lam * jnp.exp(log_dt)[:, None]
    lam_bar = jnp.exp(lam_dt)
    b_bar = ((lam_bar - 1.0) / lam)[..., None] * lax.complex(b_re, b_im)
    c_mat = lax.complex(c_re, c_im)
    g, p, cg = b_bar.shape

    def power(n):
        return jnp.exp(lam_dt[None] * jnp.asarray(n, F32)[:, None, None])

    lags = jnp.real(jnp.einsum('gop,jgp,gpi->jgoi', c_mat, power(jnp.arange(c)), b_bar, precision=HI))
    sig = jnp.arange(c)[:, None, None]
    tau = jnp.arange(c)[None, :, None]
    lag_of = (tau - sig == jnp.arange(c)[None, None, :]).astype(F32)
    m = jnp.einsum('stj,jgoi->gsito', lag_of, lags, precision=HI)
    m = m + (jnp.eye(c)[None, :, None, :, None] * jnp.eye(cg)[None, None, :, None, :]
             * d_skip.reshape(g, 1, cg, 1, 1))
    m = m.reshape(g, c * cg, c * cg)
    w = power(c_eff - 1.0 - jnp.arange(c, dtype=F32))[:, :, :, None] * b_bar[None]
    w = jnp.where((jnp.arange(c) < c_eff)[:, None, None, None], w, 0.0)
    w = w.transpose(1, 0, 3, 2).reshape(g, c * cg, p)
    v = power(jnp.arange(c, dtype=F32) + 1.0)[:, :, None, :] * c_mat[None]
    v = v.transpose(1, 3, 0, 2).reshape(g, p, c * cg)
    a = power(jnp.asarray([c_eff], F32))[0]
    return (m.astype(F32), jnp.real(w), jnp.imag(w), jnp.real(v), -jnp.imag(v), jnp.real(a), jnp.imag(a))


def _s5_prompt_kernel(u_ref, m_ref, wre_ref, wim_ref, vre_ref, vim_ref, are_ref, aim_ref,
                      y_ref, hre_ref, him_ref, ere_sc, eim_sc, hsre_sc, hsim_sc):
    ns, nc, _ = u_ref.shape
    p = wre_ref.shape[1]
    ere_sc[...] = jnp.concatenate(
        [jnp.dot(u_ref[s], wre_ref[...], precision=HI, preferred_element_type=F32) for s in range(ns)], axis=1)
    eim_sc[...] = jnp.concatenate(
        [jnp.dot(u_ref[s], wim_ref[...], precision=HI, preferred_element_type=F32) for s in range(ns)], axis=1)
    ar = are_ref[...]
    ai = aim_ref[...]

    def body(k, carry):
        hr, hi = carry
        hsre_sc[pl.ds(k, 1), :] = hr
        hsim_sc[pl.ds(k, 1), :] = hi
        er = ere_sc[pl.ds(k, 1), :]
        ei = eim_sc[pl.ds(k, 1), :]
        return ar * hr - ai * hi + er, ar * hi + ai * hr + ei

    zero = jnp.zeros((1, 2 * SSM_P), F32)
    hr, hi = lax.fori_loop(0, nc, body, (zero, zero))
    hre_ref[...] = hr
    him_ref[...] = hi
    hsre = hsre_sc[...]
    hsim = hsim_sc[...]
    for s in range(ns):
        y_ref[s] = (jnp.dot(u_ref[s], m_ref[...], precision=HI, preferred_element_type=F32)
                    + jnp.dot(hsre[:, s * p:(s + 1) * p], vre_ref[...], precision=HI, preferred_element_type=F32)
                    + jnp.dot(hsim[:, s * p:(s + 1) * p], vim_ref[...], precision=HI, preferred_element_type=F32))


def _s5_prompt(su, mats, nb, seq):
    assert nb == 2
    c = SSM_GROUP
    nc = seq // c
    cw = c * SSM_GROUP
    m, w_re, w_im, v_re, v_im, a_re, a_im = mats
    u = su.reshape(nb, nc, c, SSM_G, SSM_GROUP).transpose(3, 0, 1, 2, 4).reshape(SSM_G, nb, nc, cw)

    gspec = lambda shape: pl.BlockSpec((None,) + shape, lambda g: (g,) + (0,) * len(shape))
    y, hre, him = pl.pallas_call(
        _s5_prompt_kernel,
        out_shape=(jax.ShapeDtypeStruct((SSM_G, nb, nc, cw), F32),
                   jax.ShapeDtypeStruct((SSM_G, 1, nb * SSM_P), F32),
                   jax.ShapeDtypeStruct((SSM_G, 1, nb * SSM_P), F32)),
        grid=(SSM_G,),
        in_specs=[gspec((nb, nc, cw)), gspec((cw, cw)),
                  gspec((cw, SSM_P)), gspec((cw, SSM_P)), gspec((SSM_P, cw)), gspec((SSM_P, cw)),
                  gspec((1, nb * SSM_P)), gspec((1, nb * SSM_P))],
        out_specs=(gspec((nb, nc, cw)), gspec((1, nb * SSM_P)), gspec((1, nb * SSM_P))),
        scratch_shapes=[pltpu.VMEM((nc, nb * SSM_P), F32) for _ in range(4)],
        compiler_params=_params("parallel"),
        name="s5_prompt",
    )(u, m, w_re, w_im, v_re, v_im, jnp.tile(a_re, (1, nb))[:, None, :], jnp.tile(a_im, (1, nb))[:, None, :])
    y = y.reshape(SSM_G, nb, nc, c, SSM_GROUP).transpose(1, 2, 3, 0, 4).reshape(nb * seq, SSM_W)
    unpack = lambda h: h.reshape(SSM_G, nb, SSM_P).transpose(1, 0, 2)
    return y, unpack(hre), unpack(him)


def _s5_sample_kernel(u_ref, h0re_ref, h0im_ref, m_ref, wre_ref, wim_ref, vre_ref, vim_ref, are_ref, aim_ref,
                      y_ref, hre_ref, him_ref):
    u = u_ref[...]
    hr = h0re_ref[...]
    hi = h0im_ref[...]
    ar = are_ref[...]
    ai = aim_ref[...]
    hre_ref[...] = ar * hr - ai * hi + jnp.dot(u, wre_ref[...], precision=HI, preferred_element_type=F32)
    him_ref[...] = ar * hi + ai * hr + jnp.dot(u, wim_ref[...], precision=HI, preferred_element_type=F32)
    y_ref[...] = (jnp.dot(u, m_ref[...], precision=HI, preferred_element_type=F32)
                  + jnp.dot(hr, vre_ref[...], precision=HI, preferred_element_type=F32)
                  + jnp.dot(hi, vim_ref[...], precision=HI, preferred_element_type=F32))


def _s5_sample(su, h0_re, h0_im, mats, n_new):
    ns, r, _ = su.shape
    cw = r * SSM_GROUP
    m, w_re, w_im, v_re, v_im, a_re, a_im = mats
    u = su.reshape(ns, r, SSM_G, SSM_GROUP).transpose(2, 0, 1, 3).reshape(SSM_G, ns, cw)
    gspec = lambda shape: pl.BlockSpec((None,) + shape, lambda g: (g,) + (0,) * len(shape))
    y, hre, him = pl.pallas_call(
        _s5_sample_kernel,
        out_shape=(jax.ShapeDtypeStruct((SSM_G, ns, cw), F32),
                   jax.ShapeDtypeStruct((SSM_G, ns, SSM_P), F32),
                   jax.ShapeDtypeStruct((SSM_G, ns, SSM_P), F32)),
        grid=(SSM_G,),
        in_specs=[gspec((ns, cw)), gspec((ns, SSM_P)), gspec((ns, SSM_P)), gspec((cw, cw)),
                  gspec((cw, SSM_P)), gspec((cw, SSM_P)), gspec((SSM_P, cw)), gspec((SSM_P, cw)),
                  gspec((1, SSM_P)), gspec((1, SSM_P))],
        out_specs=(gspec((ns, cw)), gspec((ns, SSM_P)), gspec((ns, SSM_P))),
        compiler_params=_params("parallel"),
        name="s5_sample",
    )(u, h0_re.transpose(1, 0, 2), h0_im.transpose(1, 0, 2), m, w_re, w_im, v_re, v_im,
      a_re[:, None, :], a_im[:, None, :])
    y = y.reshape(SSM_G, ns, r, SSM_GROUP)[:, :, :n_new].transpose(1, 2, 0, 3).reshape(ns * n_new, SSM_W)
    return y, hre.transpose(1, 0, 2), him.transpose(1, 0, 2)


def _glu_kernel(y_ref, w_ref, b_ref, g_ref, o_ref):
    sy = _gelu(y_ref[...])
    gate = _sigmoid(jnp.dot(sy.astype(BF16), w_ref[...], preferred_element_type=F32) + b_ref[...])
    o_ref[...] = _rms_norm(sy * gate, g_ref[...])


def _s5_glu(y, w_glu, b_glu, norm_g, tm):
    t = y.shape[0]
    row = lambda: pl.BlockSpec((1, SSM_W), lambda i: (0, 0))
    return pl.pallas_call(
        _glu_kernel,
        out_shape=jax.ShapeDtypeStruct((t, SSM_W), F32),
        grid=(t // tm,),
        in_specs=[pl.BlockSpec((tm, SSM_W), lambda i: (i, 0)),
                  pl.BlockSpec((SSM_W, SSM_W), lambda i: (0, 0)), row(), row()],
        out_specs=pl.BlockSpec((tm, SSM_W), lambda i: (i, 0)),
        compiler_params=_params("parallel"),
        name="s5_glu",
    )(y, w_glu.astype(BF16), b_glu.reshape(1, SSM_W), norm_g.reshape(1, SSM_W))


def _outproj_kernel(ro_ref, fo_ref, so_ref, x_ref, wr_ref, wf_ref, ws_ref, g_ref, b_ref, o_ref):
    mix = jnp.dot(ro_ref[...].astype(BF16), wr_ref[...], preferred_element_type=F32)
    mix += jnp.dot(fo_ref[...].astype(BF16), wf_ref[...], preferred_element_type=F32)
    mix += jnp.dot(so_ref[...].astype(BF16), ws_ref[...], preferred_element_type=F32)
    o_ref[...] = _layer_norm(DEEPNORM_ALPHA * x_ref[...] + mix, g_ref[...], b_ref[...])


def _outproj_ln(ro, fo, so, x, w_out, ln_g, ln_b, tm):
    t = x.shape[0]
    w = w_out.astype(BF16)
    tile = lambda n: pl.BlockSpec((tm, n), lambda i: (i, 0))
    full = lambda r, n: pl.BlockSpec((r, n), lambda i: (0, 0))
    return pl.pallas_call(
        _outproj_kernel,
        out_shape=jax.ShapeDtypeStruct((t, D_MODEL), F32),
        grid=(t // tm,),
        in_specs=[tile(RET_W), tile(FOX_W), tile(SSM_W), tile(D_MODEL),
                  full(RET_W, D_MODEL), full(FOX_W, D_MODEL), full(SSM_W, D_MODEL),
                  full(1, D_MODEL), full(1, D_MODEL)],
        out_specs=tile(D_MODEL),
        compiler_params=_params("parallel"),
        name="outproj_ln",
    )(ro, fo, so, x, w[:RET_W], w[RET_W:RET_W + FOX_W], w[RET_W + FOX_W:],
      ln_g.reshape(1, D_MODEL), ln_b.reshape(1, D_MODEL))


def _topk_rows(problems, k):
    def body(it, _):
        for s_ref, val_ref, idx_ref in problems:
            n = s_ref.shape[0]
            iota = lax.broadcasted_iota(jnp.int32, s_ref.shape, 0)
            s = s_ref[...]
            m = jnp.max(s, axis=0, keepdims=True)
            idx = jnp.min(jnp.where(s == m, iota, n), axis=0, keepdims=True)
            val_ref[pl.ds(it, 1), :] = m
            idx_ref[pl.ds(it, 1), :] = idx
            s_ref[...] = jnp.where(iota == idx, -jnp.inf, s)
        return 0

    lax.fori_loop(0, k, body, 0)


CAND_COUNTS = [PEER_TOPK // (a + 1) for a in range(PEER_TOPK)]
CAND_STARTS = [sum(CAND_COUNTS[:a]) for a in range(PEER_TOPK)]
N_CAND = sum(CAND_COUNTS)
N_CAND_PAD = -(-N_CAND // 8) * 8


def _peer_topk_kernel(q_ref, sk_ref, i_ref, j_ref, gate_ref, s0_sc, s1_sc, cand_sc, v0_sc, v1_sc, x0_sc, x1_sc,
                      top_sc, sel_sc):
    kk = PEER_TOPK
    s = lax.dot_general(sk_ref[...], q_ref[...], NT_DIMS, precision=HI, preferred_element_type=F32)
    s0_sc[...] = s[:PEER_NKEYS]
    s1_sc[...] = s[PEER_NKEYS:]
    _topk_rows([(s0_sc, v0_sc, x0_sc), (s1_sc, v1_sc, x1_sc)], kk)
    v1 = v1_sc[...]
    for a in range(kk):
        cand_sc[CAND_STARTS[a]:CAND_STARTS[a] + CAND_COUNTS[a], :] = v0_sc[a:a + 1, :] + v1[:CAND_COUNTS[a]]
    if N_CAND_PAD > N_CAND:
        cand_sc[N_CAND:, :] = jnp.full((N_CAND_PAD - N_CAND, cand_sc.shape[1]), -jnp.inf, F32)
    _topk_rows([(cand_sc, top_sc, sel_sc)], kk)
    sel = sel_sc[...]
    a_idx = jnp.zeros_like(sel)
    start = jnp.zeros_like(sel)
    for a in range(1, kk):
        past = sel >= CAND_STARTS[a]
        a_idx = jnp.where(past, a, a_idx)
        start = jnp.where(past, CAND_STARTS[a], start)
    b_idx = sel - start
    ii = jnp.zeros_like(sel)
    jj = jnp.zeros_like(sel)
    for a in range(kk):
        ii = jnp.where(a_idx == a, x0_sc[a:a + 1, :], ii)
        jj = jnp.where(b_idx == a, x1_sc[a:a + 1, :], jj)
    top = top_sc[...]
    e = jnp.exp(top - top[0:1, :])
    i_ref[...] = ii
    j_ref[...] = jj
    gate_ref[...] = e / jnp.sum(e, axis=0, keepdims=True)


def _peer_topk(q, sub_keys, tm):
    t = q.shape[0]
    half = PEER_DK // 2
    sk = jnp.zeros((PEER_HEADS, 2 * PEER_NKEYS, PEER_DK), F32)
    sk = sk.at[:, :PEER_NKEYS, :half].set(sub_keys[:, 0]).at[:, PEER_NKEYS:, half:].set(sub_keys[:, 1])
    kk = PEER_TOPK
    slot = lambda: pl.BlockSpec((kk, tm), lambda i, h: (h, i))
    return pl.pallas_call(
        _peer_topk_kernel,
        out_shape=(jax.ShapeDtypeStruct((PEER_HEADS * kk, t), jnp.int32),
                   jax.ShapeDtypeStruct((PEER_HEADS * kk, t), jnp.int32),
                   jax.ShapeDtypeStruct((PEER_HEADS * kk, t), F32)),
        grid=(t // tm, PEER_HEADS),
        in_specs=[pl.BlockSpec((tm, PEER_DK), lambda i, h: (i, h)),
                  pl.BlockSpec((None, 2 * PEER_NKEYS, PEER_DK), lambda i, h: (h, 0, 0))],
        out_specs=(slot(), slot(), slot()),
        scratch_shapes=[pltpu.VMEM((PEER_NKEYS, tm), F32), pltpu.VMEM((PEER_NKEYS, tm), F32),
                        pltpu.VMEM((N_CAND_PAD, tm), F32),
                        pltpu.VMEM((kk, tm), F32), pltpu.VMEM((kk, tm), F32),
                        pltpu.VMEM((kk, tm), jnp.int32), pltpu.VMEM((kk, tm), jnp.int32),
                        pltpu.VMEM((kk, tm), F32), pltpu.VMEM((kk, tm), jnp.int32)],
        compiler_params=_params("parallel", "arbitrary"),
        name="peer_topk",
    )(q, sk)


G_TOKENS = 128
G_ROW_STRIDE = G_TOKENS + 8
G_UNROLL = 32


def _peer_gates_kernel(i_ref, j_ref, gate_ref, o_ref, g_sc):
    tb, ns = i_ref.shape
    nk = PEER_NKEYS
    io = lax.broadcasted_iota(jnp.int32, (nk, ns), 0)

    def body(tt, carry):
        for u in range(G_UNROLL):
            t = tt * G_UNROLL + u
            irow = i_ref[pl.ds(t, 1), :]
            jrow = j_ref[pl.ds(t, 1), :]
            grow = gate_ref[pl.ds(t, 1), :]
            pt = jnp.where(io == irow, grow, 0.0).astype(BF16)
            qt = jnp.where(io == jrow, 1.0, 0.0).astype(BF16)
            g = lax.dot_general(pt, qt, NT_DIMS, preferred_element_type=F32)
            g_sc[pl.ds(t, nk, stride=G_ROW_STRIDE), :] = g
        return carry

    lax.fori_loop(0, tb // G_UNROLL, body, 0)
    for i in range(nk):
        o_ref[:, i * nk:(i + 1) * nk] = g_sc[i * G_ROW_STRIDE:i * G_ROW_STRIDE + tb, :].astype(BF16)


def _peer_gates(ii, jj, gate):
    t = ii.shape[0]
    tb = G_TOKENS
    ns = PEER_HEADS * PEER_TOPK
    tok = lambda: pl.BlockSpec((tb, ns), lambda i: (i, 0))
    return pl.pallas_call(
        _peer_gates_kernel,
        out_shape=jax.ShapeDtypeStruct((t, PEER_N), BF16),
        grid=(t // tb,),
        in_specs=[tok(), tok(), tok()],
        out_specs=pl.BlockSpec((tb, PEER_N), lambda i: (i, 0)),
        scratch_shapes=[pltpu.VMEM((PEER_NKEYS * G_ROW_STRIDE, PEER_NKEYS), F32)],
        compiler_params=_params("parallel"),
        name="peer_gates",
    )(ii, jj, gate)


def _peer_experts_kernel(x_ref, g_ref, u_ref, v_ref, lg_ref, lb_ref, o_ref, xb_sc, acc_sc):
    e = pl.program_id(1)

    @pl.when(e == 0)
    def _():
        xb_sc[...] = x_ref[...].astype(BF16)
        acc_sc[...] = jnp.zeros_like(acc_sc)

    h = lax.dot_general(xb_sc[...], u_ref[...], NT_DIMS, preferred_element_type=F32)
    w = (g_ref[...].astype(F32) * _gelu(h)).astype(BF16)
    acc_sc[...] += jnp.dot(w, v_ref[...], preferred_element_type=F32)

    @pl.when(e == pl.num_programs(1) - 1)
    def _():
        o_ref[...] = _layer_norm(DEEPNORM_ALPHA * x_ref[...] + acc_sc[...], lg_ref[...], lb_ref[...])


def _peer_experts_ln(x, g, u_tabs, v_tabs, layer, ln_g, ln_b, tm, nb):
    t = x.shape[0]
    return pl.pallas_call(
        _peer_experts_kernel,
        out_shape=jax.ShapeDtypeStruct((t, D_MODEL), F32),
        grid=(t // tm, PEER_N // nb),
        in_specs=[pl.BlockSpec((tm, D_MODEL), lambda i, e: (i, 0)),
                  pl.BlockSpec((tm, nb), lambda i, e: (i, e)),
                  pl.BlockSpec((None, nb, D_MODEL), lambda i, e: (layer, e, 0)),
                  pl.BlockSpec((None, nb, D_MODEL), lambda i, e: (layer, e, 0)),
                  pl.BlockSpec((1, D_MODEL), lambda i, e: (0, 0)),
                  pl.BlockSpec((1, D_MODEL), lambda i, e: (0, 0))],
        out_specs=pl.BlockSpec((tm, D_MODEL), lambda i, e: (i, 0)),
        scratch_shapes=[pltpu.VMEM((tm, D_MODEL), BF16), pltpu.VMEM((tm, D_MODEL), F32)],
        compiler_params=_params("parallel", "arbitrary"),
        name="peer_experts_ln",
    )(x, g, u_tabs, v_tabs, ln_g.reshape(1, D_MODEL), ln_b.reshape(1, D_MODEL))


def _ple_kernel(x_ref, xn_ref, p_ref, wg_ref, bg_ref, we_ref, o_ref):
    gate = _sigmoid(jnp.dot(x_ref[...].astype(BF16), wg_ref[...], preferred_element_type=F32) + bg_ref[...])
    emb = jnp.dot(p_ref[...].astype(BF16), we_ref[...], preferred_element_type=F32)
    o_ref[...] = xn_ref[...] + gate * emb


def _ple(x, p, w_pg, b_pg, w_pe, tm, tn):
    t = x.shape[0]
    return pl.pallas_call(
        _ple_kernel,
        out_shape=jax.ShapeDtypeStruct((t, D_MODEL), F32),
        grid=(t // tm, D_MODEL // tn),
        in_specs=[pl.BlockSpec((tm, D_MODEL), lambda i, j: (i, 0)),
                  pl.BlockSpec((tm, tn), lambda i, j: (i, j)),
                  pl.BlockSpec((tm, PLE_DIM), lambda i, j: (i, 0)),
                  pl.BlockSpec((D_MODEL, tn), lambda i, j: (0, j)),
                  pl.BlockSpec((1, tn), lambda i, j: (0, j)),
                  pl.BlockSpec((PLE_DIM, tn), lambda i, j: (0, j))],
        out_specs=pl.BlockSpec((tm, tn), lambda i, j: (i, j)),
        compiler_params=_params("parallel", "arbitrary"),
        name="ple_gate",
    )(x, x, p, w_pg.astype(BF16), b_pg.reshape(1, D_MODEL), w_pe.astype(BF16))


def _split_bf16(w):
    hi = w.astype(BF16)
    return hi, (w - hi.astype(F32)).astype(BF16)


def _layer(x, p, lw, caches, stacked, page_table, layer, nb, seq, ns, n_new, tiles):
    t_all = x.shape[0]
    tp = nb * seq
    r = 8
    cache_k, cache_v, lf_sums, state_ret, state_re, state_im = caches
    past_len = page_table.shape[1] * (cache_k.shape[2] // H_FOX)

    u_tabs, v_tabs, w_in_all = stacked
    z = _in_proj(x, w_in_all, layer, tiles['tm'], 512)
    lf = _forget_gates(x, w_in_all[layer, :, Z_FF:Z_SU], lw['b_fox_f'], tiles['tm'])
    zs = jnp.pad(z[tp:].reshape(ns, n_new, Z_W), ((0, 0), (0, r - n_new), (0, 0)))

    ro, ret_p = _retention_prompt(z, lw['ret_gn_g'], nb, seq, t_all)
    ro_s, ret_s = _retention_sample(zs, state_ret, layer, lw['ret_gn_g'], past_len, n_new)
    ro = ro.at[tp:].set(ro_s[:, :n_new].reshape(ns * n_new, RET_W))

    fo = _fox_prompt(z, _forget_cumsum(lf, nb, seq), lw['fox_norm_g'], nb, seq, t_all, tiles['attn'])
    lf_s = lf[:, tp:].T.reshape(ns, n_new, H_FOX)
    lfn = jnp.pad(lf_s, ((0, 0), (0, r - n_new), (0, 0))).reshape(ns, 1, r * H_FOX)
    fo_s = _fox_sample(z[tp:, Z_FQ:Z_FQ + FOX_W].reshape(ns, n_new * H_FOX, DH_FOX),
                       zs[:, :, Z_FK:Z_FK + FOX_W].reshape(ns, r * H_FOX, DH_FOX),
                       zs[:, :, Z_FV:Z_FV + FOX_W].reshape(ns, r * H_FOX, DH_FOX),
                       lfn, cache_k, cache_v, lf_sums, page_table, layer, lw['fox_norm_g'])
    fo = fo.at[tp:].set(fo_s.reshape(ns * n_new, FOX_W))

    ssm = (lw['ssm_a_re'], lw['ssm_a_im'], lw['ssm_b_re'], lw['ssm_b_im'], lw['ssm_c_re'], lw['ssm_c_im'],
           lw['ssm_d'], lw['ssm_log_dt'])
    y_p, hre_p, him_p = _s5_prompt(z[:tp, Z_SU:], _s5_mats(*ssm, SSM_GROUP, SSM_GROUP), nb, seq)
    y_s, hre_s, him_s = _s5_sample(zs[:, :, Z_SU:], state_re[layer], state_im[layer], _s5_mats(*ssm, r, n_new), n_new)
    so = _s5_glu(jnp.concatenate([y_p, y_s], 0), lw['w_glu'], lw['b_glu'], lw['ssm_norm_g'], tiles['tm'])

    x1 = _outproj_ln(ro, fo, so, x, lw['w_out'], lw['ln1_g'], lw['ln1_b'], tiles['tm_ln'])

    q = _matmul(x1, _split_bf16(lw['peer_w_q']), tiles['tm'], 512, "peer_query")
    ii, jj, gate = _peer_topk(q, lw['peer_sub_keys'], tiles['tm'])
    g = _peer_gates(ii.T, jj.T, gate.T)
    x2 = _peer_experts_ln(x1, g, u_tabs, v_tabs, layer, lw['ln2_g'], lw['ln2_b'], tiles['tm'], 512)

    x3 = _ple(x2, p, lw['w_pg'], lw['b_pg'], lw['w_pe'], tiles['tm'], 512)

    st_p = (z[:tp, Z_FK:Z_FK + FOX_W].reshape(nb, seq, H_FOX, DH_FOX),
            z[:tp, Z_FV:Z_FV + FOX_W].reshape(nb, seq, H_FOX, DH_FOX),
            lf[:, :tp].T.reshape(nb, seq, H_FOX), ret_p, hre_p, him_p)
    st_s = (z[tp:, Z_FK:Z_FK + FOX_W].reshape(ns, n_new, H_FOX, DH_FOX),
            z[tp:, Z_FV:Z_FV + FOX_W].reshape(ns, n_new, H_FOX, DH_FOX),
            lf_s, ret_s, hre_s, him_s)
    return x3, st_p, st_s


def kernel(x_prompt, x_sample, p_prompt, p_sample, cache_fox_k, cache_fox_v, cache_fox_lf, state_ret, state_ssm_re, state_ssm_im, page_table, w_in, w_out, ret_gn_g, b_fox_f, fox_norm_g, ssm_a_re, ssm_a_im, ssm_b_re, ssm_b_im, ssm_c_re, ssm_c_im, ssm_d, ssm_log_dt, w_glu, b_glu, ssm_norm_g, ln1_g, ln1_b, ln2_g, ln2_b, peer_w_q, peer_sub_keys, peer_u, peer_v, w_pe, w_pg, b_pg):
    nb, seq, d = x_prompt.shape
    ns, n_new, _ = x_sample.shape
    tp, ts = nb * seq, ns * n_new
    t_all = tp + ts
    tm = 640 if t_all % 640 == 0 else 128
    tiles = dict(tm=tm, tm_ln=tm // 2 if tm == 640 else tm, attn=256 if seq % 256 == 0 else 128)
    x = jnp.concatenate([x_prompt.reshape(tp, d), x_sample.reshape(ts, d)], 0)
    depth, n_pool, pg, nh, dh = cache_fox_k.shape
    caches = (cache_fox_k.reshape(depth, n_pool, pg * nh, dh), cache_fox_v.reshape(depth, n_pool, pg * nh, dh),
              _lf_prep(cache_fox_lf), state_ret, state_ssm_re, state_ssm_im)
    weights = dict(w_out=w_out, ret_gn_g=ret_gn_g, b_fox_f=b_fox_f, fox_norm_g=fox_norm_g,
                   ssm_a_re=ssm_a_re, ssm_a_im=ssm_a_im, ssm_b_re=ssm_b_re, ssm_b_im=ssm_b_im,
                   ssm_c_re=ssm_c_re, ssm_c_im=ssm_c_im, ssm_d=ssm_d, ssm_log_dt=ssm_log_dt,
                   w_glu=w_glu, b_glu=b_glu, ssm_norm_g=ssm_norm_g, ln1_g=ln1_g, ln1_b=ln1_b,
                   ln2_g=ln2_g, ln2_b=ln2_b, peer_w_q=peer_w_q, peer_sub_keys=peer_sub_keys,
                   w_pe=w_pe, w_pg=w_pg, b_pg=b_pg)
    stacked = (peer_u.astype(BF16), peer_v.astype(BF16), w_in)
    sts_p, sts_s = [], []
    for l in range(DEPTH):
        lw = {k: v[l] for k, v in weights.items()}
        p = jnp.concatenate([p_prompt[l].reshape(tp, PLE_DIM), p_sample[l].reshape(ts, PLE_DIM)], 0)
        x, st_p, st_s = _layer(x, p, lw, caches, stacked, page_table, l, nb, seq, ns, n_new, tiles)
        sts_p.append(st_p)
        sts_s.append(st_s)
    stack = lambda sts, i: jnp.stack([s[i] for s in sts])
    return ((x[:tp].reshape(nb, seq, d), x[tp:].reshape(ns, n_new, d))
            + tuple(stack(sts_p, i) for i in range(6)) + tuple(stack(sts_s, i) for i in range(6)))
```

```python
import functools
import math

import jax
import jax.numpy as jnp
from jax import lax
from jax.experimental import pallas as pl
from jax.experimental.pallas import tpu as pltpu

F32 = jnp.float32
BF16 = jnp.bfloat16
HI = lax.Precision.HIGHEST

D_MODEL = 2048
DEPTH = 2
H_RET, DH_RET = 4, 128
RET_W = H_RET * DH_RET
H_FOX, DH_FOX = 8, 128
FOX_W = H_FOX * DH_FOX
SSM_W, SSM_GROUP = 512, 16
SSM_G = SSM_W // SSM_GROUP
SSM_P = 64
PEER_HEADS, PEER_NKEYS, PEER_DK, PEER_TOPK = 8, 128, 128, 16
PEER_N = PEER_NKEYS * PEER_NKEYS
PLE_DIM = 256
BLOCK = 128
ROPE_BASE = 10000.0
DEEPNORM_ALPHA = (2 * DEPTH) ** 0.25
EPS = 1e-5

Z_FQ, Z_FK, Z_FV, Z_FF = 4 * RET_W, 4 * RET_W + FOX_W, 4 * RET_W + 2 * FOX_W, 4 * RET_W + 3 * FOX_W
Z_SU = Z_FF + H_FOX
Z_W = Z_SU + SSM_W

VMEM_LIMIT = 56 * 1024 * 1024
NT_DIMS = (((1,), (1,)), ((), ()))
TN_DIMS = (((0,), (0,)), ((), ()))


def _params(*sem):
    return pltpu.CompilerParams(dimension_semantics=sem, vmem_limit_bytes=VMEM_LIMIT)


def _gelu(x):
    return 0.5 * x * (1.0 + jnp.tanh(math.sqrt(2.0 / math.pi) * (x + 0.044715 * (x * x * x))))


def _sigmoid(x):
    return 1.0 / (1.0 + jnp.exp(-x))


def _log_sigmoid(x):
    return jnp.minimum(x, 0.0) - jnp.log1p(jnp.exp(-jnp.abs(x)))


def _layer_norm(y, g, b):
    mu = jnp.mean(y, -1, keepdims=True)
    var = jnp.mean(jnp.square(y - mu), -1, keepdims=True)
    return (y - mu) * lax.rsqrt(var + EPS) * g + b


def _rms_norm(y, g):
    return y * lax.rsqrt(jnp.mean(y * y, -1, keepdims=True) + EPS) * g


def _mm_kernel(x_ref, w_ref, o_ref):
    o_ref[...] = jnp.dot(x_ref[...].astype(BF16), w_ref[...], preferred_element_type=F32)


def _mm3_kernel(x_ref, wh_ref, wl_ref, o_ref):
    x = x_ref[...]
    xh = x.astype(BF16)
    xl = (x - xh.astype(F32)).astype(BF16)
    wh = wh_ref[...]
    acc = jnp.dot(xh, wh, preferred_element_type=F32)
    acc += jnp.dot(xl, wh, preferred_element_type=F32)
    acc += jnp.dot(xh, wl_ref[...], preferred_element_type=F32)
    o_ref[...] = acc


def _matmul(x, ws, tm, tn, name):
    t, k = x.shape
    n = ws[0].shape[1]
    kern = _mm_kernel if len(ws) == 1 else _mm3_kernel
    return pl.pallas_call(
        kern,
        out_shape=jax.ShapeDtypeStruct((t, n), F32),
        grid=(t // tm, n // tn),
        in_specs=[pl.BlockSpec((tm, k), lambda i, j: (i, 0))]
        + [pl.BlockSpec((k, tn), lambda i, j: (0, j)) for _ in ws],
        out_specs=pl.BlockSpec((tm, tn), lambda i, j: (i, j)),
        compiler_params=_params("parallel", "arbitrary"),
        name=name,
    )(x, *ws)


def _mm_w32_kernel(x_ref, w_ref, o_ref, xb_sc):
    @pl.when(pl.program_id(1) == 0)
    def _():
        xb_sc[...] = x_ref[...].astype(BF16)

    o_ref[...] = jnp.dot(xb_sc[...], w_ref[...].astype(BF16), preferred_element_type=F32)


def _in_proj(x, w_all, layer, tm, tn):
    t, k = x.shape
    n = w_all.shape[2]
    return pl.pallas_call(
        _mm_w32_kernel,
        out_shape=jax.ShapeDtypeStruct((t, n), F32),
        grid=(t // tm, pl.cdiv(n, tn)),
        in_specs=[pl.BlockSpec((tm, k), lambda i, j: (i, 0)),
                  pl.BlockSpec((None, k, tn), lambda i, j: (layer, 0, j))],
        out_specs=pl.BlockSpec((tm, tn), lambda i, j: (i, j)),
        scratch_shapes=[pltpu.VMEM((tm, k), BF16)],
        compiler_params=_params("parallel", "arbitrary"),
        name="in_proj",
    )(x, w_all)


def _gate_kernel(x_ref, w_ref, b_ref, lf_ref):
    f = lax.dot_general(w_ref[...], x_ref[...], NT_DIMS, precision=HI, preferred_element_type=F32)
    lf_ref[...] = _log_sigmoid(f + b_ref[...])


def _forget_gates(x, w_ff, b_ff, tm):
    t, k = x.shape
    return pl.pallas_call(
        _gate_kernel,
        out_shape=jax.ShapeDtypeStruct((H_FOX, t), F32),
        grid=(t // tm,),
        in_specs=[
            pl.BlockSpec((tm, k), lambda i: (i, 0)),
            pl.BlockSpec((H_FOX, k), lambda i: (0, 0)),
            pl.BlockSpec((H_FOX, 1), lambda i: (0, 0)),
        ],
        out_specs=pl.BlockSpec((H_FOX, tm), lambda i: (0, i)),
        compiler_params=_params("parallel"),
        name="forget_gates",
    )(x, w_ff.T, b_ff.reshape(H_FOX, 1))


def _cumsum_kernel(lf_ref, c_ref, carry):
    @pl.when(pl.program_id(1) == 0)
    def _():
        carry[...] = jnp.zeros_like(carry)

    n = lf_ref.shape[1]
    r = lax.broadcasted_iota(jnp.int32, (n, n), 0)
    c = lax.broadcasted_iota(jnp.int32, (n, n), 1)
    upper = (r <= c).astype(F32)
    cs = jnp.dot(lf_ref[...], upper, precision=HI, preferred_element_type=F32) + carry[...]
    c_ref[...] = cs
    carry[...] = cs[:, n - 1:n]


def _forget_cumsum(lf, nb, seq):
    n = 256 if seq % 256 == 0 else 128
    nc = seq // n
    return pl.pallas_call(
        _cumsum_kernel,
        out_shape=jax.ShapeDtypeStruct((H_FOX, nb * seq), F32),
        grid=(nb, nc),
        in_specs=[pl.BlockSpec((H_FOX, n), lambda b, i: (0, b * nc + i))],
        out_specs=pl.BlockSpec((H_FOX, n), lambda b, i: (0, b * nc + i)),
        scratch_shapes=[pltpu.VMEM((H_FOX, 1), F32)],
        compiler_params=_params("parallel", "arbitrary"),
        name="forget_cumsum",
    )(lf)


def _fox_prompt_kernel(q_ref, k_ref, v_ref, c_ref, g_ref, o_ref, qs_sc, m_sc, l_sc, acc_sc, *, scale):
    qi = pl.program_id(1)
    ki = pl.program_id(2)
    bq = q_ref.shape[0]
    bk = k_ref.shape[0]
    log2e = math.log2(math.e)

    @pl.when(ki == 0)
    def _():
        qs_sc[...] = (q_ref[...] * (scale * log2e)).astype(BF16)
        m_sc[...] = jnp.full_like(m_sc, -jnp.inf)
        l_sc[...] = jnp.zeros_like(l_sc)
        acc_sc[...] = jnp.zeros_like(acc_sc)

    last = (qi * bq + bq - 1) // bk

    def attend(diagonal):
        if diagonal:
            causal = (ki * bk + lax.broadcasted_iota(jnp.int32, (bq, bk), 1)
                      <= qi * bq + lax.broadcasted_iota(jnp.int32, (bq, bk), 0))
        c2 = c_ref[...] * log2e
        for h in range(H_FOX):
            sl = slice(h * DH_FOX, (h + 1) * DH_FOX)
            k = k_ref[:, sl].astype(BF16)
            s = lax.dot_general(qs_sc[:, sl], k, NT_DIMS, preferred_element_type=F32) - c2[h:h + 1, :]
            if diagonal:
                s = jnp.where(causal, s, -jnp.inf)
            m_prev = m_sc[h]
            m_new = jnp.maximum(m_prev, jnp.max(s, axis=1, keepdims=True))
            alpha = jnp.exp2(m_prev - m_new)
            p = jnp.exp2(s - jnp.tile(m_new, (1, bk // 128)))
            l_sc[h] = alpha * l_sc[h] + jnp.sum(p, axis=1, keepdims=True)
            acc_sc[:, sl] = alpha * acc_sc[:, sl] + jnp.dot(
                p.astype(BF16), v_ref[:, sl].astype(BF16), preferred_element_type=F32)
            m_sc[h] = m_new

    @pl.when(ki < last)
    def _():
        attend(False)

    @pl.when(ki == last)
    def _():
        attend(True)
        for h in range(H_FOX):
            sl = slice(h * DH_FOX, (h + 1) * DH_FOX)
            acc_sc[:, sl] = acc_sc[:, sl] / l_sc[h]
        o_ref[...] = _rms_norm(acc_sc[...], g_ref[...])


def _fox_prompt(z, c, norm_g, nb, seq, t_all, blk):
    bk = blk
    nq, nk = seq // blk, seq // bk
    cq, ck, cv = Z_FQ // FOX_W, Z_FK // FOX_W, Z_FV // FOX_W
    kern = functools.partial(_fox_prompt_kernel, scale=DH_FOX ** -0.5)

    def key_block(b, i, j):
        return b * nk + jnp.minimum(j, (i * blk + blk - 1) // bk)

    return pl.pallas_call(
        kern,
        out_shape=jax.ShapeDtypeStruct((t_all, FOX_W), F32),
        grid=(nb, nq, nk),
        in_specs=[
            pl.BlockSpec((blk, FOX_W), lambda b, i, j: (b * nq + i, cq)),
            pl.BlockSpec((bk, FOX_W), lambda b, i, j: (key_block(b, i, j), ck)),
            pl.BlockSpec((bk, FOX_W), lambda b, i, j: (key_block(b, i, j), cv)),
            pl.BlockSpec((H_FOX, bk), lambda b, i, j: (0, key_block(b, i, j))),
            pl.BlockSpec((1, FOX_W), lambda b, i, j: (0, 0)),
        ],
        out_specs=pl.BlockSpec((blk, FOX_W), lambda b, i, j: (b * nq + i, 0)),
        scratch_shapes=[
            pltpu.VMEM((blk, FOX_W), BF16),
            pltpu.VMEM((H_FOX, blk, 128), F32),
            pltpu.VMEM((H_FOX, blk, 128), F32),
            pltpu.VMEM((blk, FOX_W), F32),
        ],
        compiler_params=_params("parallel", "parallel", "arbitrary"),
        name="fox_prompt",
    )(z, z, z, c, norm_g.reshape(1, FOX_W))


def _lf_prep_kernel(lf_ref, w_ref, o_ref):
    x = lf_ref[...]
    p1 = x.astype(BF16)
    r1 = x - p1.astype(F32)
    p2 = r1.astype(BF16)
    p3 = (r1 - p2.astype(F32)).astype(BF16)
    w = w_ref[...]
    acc = jnp.dot(p3, w, preferred_element_type=F32)
    acc += jnp.dot(p2, w, preferred_element_type=F32)
    acc += jnp.dot(p1, w, preferred_element_type=F32)
    o_ref[...] = acc


def _lf_prep(cache_lf):
    depth, n_pool, pg, nh = cache_lf.shape
    c = pg * nh
    rows = depth * n_pool
    tm = max(d for d in (256, 128, 64, 32, 16, 8) if rows % d == 0)
    src = jnp.arange(c)[:, None]
    dst = jnp.arange(c)[None, :]
    same_head = (src % nh) == (dst % nh)
    w = jnp.concatenate([same_head & (src // nh > dst // nh), same_head], axis=1).astype(BF16)
    return pl.pallas_call(
        _lf_prep_kernel,
        out_shape=jax.ShapeDtypeStruct((rows, 2 * c), F32),
        grid=(rows // tm,),
        in_specs=[pl.BlockSpec((tm, c), lambda i: (i, 0)), pl.BlockSpec((c, 2 * c), lambda i: (0, 0))],
        out_specs=pl.BlockSpec((tm, 2 * c), lambda i: (i, 0)),
        compiler_params=_params("parallel"),
        name="lf_prep",
    )(cache_lf.reshape(rows, c), w)


def _fox_sample_kernel(pt_ref, q_ref, kn_ref, vn_ref, lfn_ref, g_ref, *rest, scale, npg, page_of):
    k_refs, v_refs, x_refs = rest[:npg], rest[npg:2 * npg], rest[2 * npg:3 * npg]
    o_ref, m_sc, l_sc, acc_sc, carry_sc = rest[3 * npg:]
    step = pl.program_id(1)
    nr = q_ref.shape[0]
    ncol = k_refs[0].shape[0]
    head_mask = H_FOX - 1
    head_bits = H_FOX.bit_length() - 1

    @pl.when(step == 0)
    def _():
        m_sc[...] = jnp.full_like(m_sc, -jnp.inf)
        l_sc[...] = jnp.zeros_like(l_sc)
        acc_sc[...] = jnp.zeros_like(acc_sc)
        carry_sc[...] = jnp.zeros_like(carry_sc)

    def equal_heads(ncol):
        rows = lax.broadcasted_iota(jnp.int32, (nr, ncol), 0)
        cols = lax.broadcasted_iota(jnp.int32, (nr, ncol), 1)
        return rows, cols, (rows & head_mask) == (cols & head_mask)

    q = q_ref[...].astype(BF16)
    _, _, same_head = equal_heads(ncol)
    carry = carry_sc[...]
    m_prev = m_sc[...]
    m_new = m_prev
    scores = []
    for i in range(npg):
        x = x_refs[i][pl.ds(page_of(pl.program_id(0), step, i, pt_ref) % 8, 1), :]
        bias = x[:, :ncol] + carry
        carry = carry + x[:, ncol:]
        s = lax.dot_general(q, k_refs[i][...].astype(BF16), NT_DIMS, preferred_element_type=F32) * scale + bias
        s = jnp.where(same_head, s, -jnp.inf)
        m_new = jnp.maximum(m_new, jnp.max(s, axis=1, keepdims=True))
        scores.append(s)
    carry_sc[...] = carry
    alpha = jnp.exp(m_prev - m_new)
    l_new = alpha * l_sc[...]
    acc = alpha * acc_sc[...]
    for i in range(npg):
        p = jnp.exp(scores[i] - m_new)
        l_new += jnp.sum(p, axis=1, keepdims=True)
        acc += jnp.dot(p.astype(BF16), v_refs[i][...].astype(BF16), preferred_element_type=F32)
    m_sc[...] = m_new
    l_sc[...] = l_new
    acc_sc[...] = acc

    @pl.when(step == pl.num_programs(1) - 1)
    def _():
        nc = kn_ref.shape[0]
        rows, cols, same = equal_heads(nc)
        causal = (cols >> head_bits) <= (rows >> head_bits)
        r2 = lax.broadcasted_iota(jnp.int32, (nc, nc), 0)
        c2 = lax.broadcasted_iota(jnp.int32, (nc, nc), 1)
        prefix = jnp.where((r2 & head_mask) == (c2 & head_mask), jnp.where(r2 <= c2, 1.0, 0.0), 0.0)
        cn = jnp.dot(jnp.broadcast_to(lfn_ref[...], (8, nc)), prefix, precision=HI,
                     preferred_element_type=F32)[0:1]
        s = lax.dot_general(q, kn_ref[...].astype(BF16), NT_DIMS, preferred_element_type=F32) * scale - cn
        s = jnp.where(same, jnp.where(causal, s, -jnp.inf), -jnp.inf)
        m_fin = jnp.maximum(m_new, jnp.max(s, axis=1, keepdims=True))
        a_fin = jnp.exp(m_new - m_fin)
        p = jnp.exp(s - m_fin)
        l_fin = a_fin * l_new + jnp.sum(p, axis=1, keepdims=True)
        out = (a_fin * acc + jnp.dot(p.astype(BF16), vn_ref[...].astype(BF16),
                                     preferred_element_type=F32)) / l_fin
        r3 = lax.broadcasted_iota(jnp.int32, (nr, nr), 0)
        c3 = lax.broadcasted_iota(jnp.int32, (nr, nr), 1)
        same_token = jnp.where((r3 >> head_bits) == (c3 >> head_bits), 1.0, 0.0)
        ssq = jnp.sum(jnp.dot(same_token, out * out, precision=HI, preferred_element_type=F32),
                      axis=1, keepdims=True)
        o_ref[...] = out * lax.rsqrt(ssq / (H_FOX * DH_FOX) + EPS) * g_ref[...]


def _fox_sample(q, kn, vn, lfn, cache_k, cache_v, lf_sums, page_table, layer, norm_g):
    ns, nr, _ = q.shape
    nc = kn.shape[1]
    n_pages = page_table.shape[1]
    n_pool, cols = cache_k.shape[1], cache_k.shape[2]
    assert n_pool % 8 == 0 and lf_sums.shape == (cache_k.shape[0] * n_pool, 2 * cols)
    npg = max(d for d in range(1, 9) if n_pages % d == 0)
    g = jnp.tile(norm_g.reshape(H_FOX, DH_FOX), (nr // H_FOX, 1))

    def page_of(b, s, i, pt):
        return pt[b * n_pages + (n_pages - 1 - (s * npg + i))]

    kern = functools.partial(_fox_sample_kernel, scale=DH_FOX ** -0.5, npg=npg, page_of=page_of)

    def paged(i):
        return pl.BlockSpec((None, None, cols, DH_FOX), lambda b, s, pt: (layer, page_of(b, s, i, pt), 0, 0))

    def paged_sums(i):
        return pl.BlockSpec((8, 2 * cols), lambda b, s, pt: ((layer * n_pool + page_of(b, s, i, pt)) // 8, 0))

    grid_spec = pltpu.PrefetchScalarGridSpec(
        num_scalar_prefetch=1,
        grid=(ns, n_pages // npg),
        in_specs=[
            pl.BlockSpec((None, nr, DH_FOX), lambda b, s, pt: (b, 0, 0)),
            pl.BlockSpec((None, nc, DH_FOX), lambda b, s, pt: (b, 0, 0)),
            pl.BlockSpec((None, nc, DH_FOX), lambda b, s, pt: (b, 0, 0)),
            pl.BlockSpec((None, 1, nc), lambda b, s, pt: (b, 0, 0)),
            pl.BlockSpec((nr, DH_FOX), lambda b, s, pt: (0, 0)),
        ] + [paged(i) for i in range(npg)] * 2 + [paged_sums(i) for i in range(npg)],
        out_specs=pl.BlockSpec((None, nr, DH_FOX), lambda b, s, pt: (b, 0, 0)),
        scratch_shapes=[
            pltpu.VMEM((nr, 1), F32),
            pltpu.VMEM((nr, 1), F32),
            pltpu.VMEM((nr, DH_FOX), F32),
            pltpu.VMEM((1, cols), F32),
        ],
    )
    return pl.pallas_call(
        kern,
        out_shape=jax.ShapeDtypeStruct((ns, nr, DH_FOX), F32),
        grid_spec=grid_spec,
        compiler_params=_params("parallel", "arbitrary"),
        name="fox_sample",
    )(page_table.reshape(-1), q, kn, vn, lfn, g,
      *([cache_k] * npg), *([cache_v] * npg), *([lf_sums] * npg))


def _ret_kernel(*refs, has_s0):
    if has_s0:
        (q_ref, k_ref, v_ref, gt_ref, cos_ref, sin_ref, dm_ref, xi_ref, zt_ref, gc_ref, gn_ref, s0_ref,
         o_ref, s_ref) = refs
    else:
        (q_ref, k_ref, v_ref, gt_ref, cos_ref, sin_ref, dm_ref, xi_ref, zt_ref, gc_ref, gn_ref,
         o_ref, s_ref) = refs

    @pl.when(pl.program_id(1) == 0)
    def _():
        if has_s0:
            s_ref[...] = s0_ref[...]
        else:
            s_ref[...] = jnp.zeros_like(s_ref)

    cos = cos_ref[...]
    sin = sin_ref[...]

    def rot(x):
        return x * cos + pltpu.roll(x, DH_RET // 2, 1) * sin

    for h in range(H_RET):
        sl = slice(h * DH_RET, (h + 1) * DH_RET)
        q = rot(q_ref[:, sl])
        k = rot(k_ref[:, sl]) * DH_RET ** -0.5
        v = v_ref[:, sl]
        s = s_ref[h]
        att = lax.dot_general(q, k, NT_DIMS, preferred_element_type=F32) * dm_ref[h]
        o = jnp.dot(att, v, preferred_element_type=F32) + jnp.dot(q * xi_ref[h], s, preferred_element_type=F32)
        s_ref[h] = gc_ref[h] * s + lax.dot_general(k * zt_ref[h], v, TN_DIMS, preferred_element_type=F32)
        mu = jnp.mean(o, -1, keepdims=True)
        var = jnp.mean(jnp.square(o - mu), -1, keepdims=True)
        on = (o - mu) * lax.rsqrt(var + EPS)
        gt = gt_ref[:, sl]
        o_ref[:, sl] = on * gn_ref[:, sl] * (gt * _sigmoid(gt))


def _ret_consts(c, c_eff):
    lg = jnp.log1p(-jnp.exp2(-5.0 - jnp.arange(H_RET, dtype=F32)))
    idx = jnp.arange(c, dtype=F32)
    rel = idx[:, None] - idx[None, :]
    dmask = jnp.where(rel >= 0, jnp.exp(lg[:, None, None] * jnp.maximum(rel, 0.0)), 0.0)
    xi = jnp.exp(lg[:, None] * (idx + 1.0))[..., None]
    zeta = jnp.exp(lg[:, None] * (c_eff - 1.0 - idx))[..., None]
    g_c = jnp.broadcast_to(jnp.exp(lg * c_eff)[:, None, None], (H_RET, 1, DH_RET))
    return dmask, xi, zeta, g_c


def _rope_tables(pos):
    inv = ROPE_BASE ** (-jnp.arange(0, DH_RET, 2, dtype=F32) / DH_RET)
    ang = pos.astype(F32)[:, None] * inv[None, :]
    cos, sin = jnp.cos(ang), jnp.sin(ang)
    return jnp.concatenate([cos, cos], -1), jnp.concatenate([-sin, sin], -1)


def _retention_prompt(z, gn_g, nb, seq, t_all):
    c = BLOCK
    nc = seq // c
    dmask, xi, zeta, g_c = _ret_consts(c, c)
    cos, sin = _rope_tables(jnp.arange(seq))

    def zspec(col):
        return pl.BlockSpec((c, RET_W), lambda b, i: (b * nc + i, col))

    full = lambda shape: pl.BlockSpec(shape, lambda b, i: (0,) * len(shape))
    return pl.pallas_call(
        functools.partial(_ret_kernel, has_s0=False),
        out_shape=(jax.ShapeDtypeStruct((t_all, RET_W), F32),
                   jax.ShapeDtypeStruct((nb, H_RET, DH_RET, DH_RET), F32)),
        grid=(nb, nc),
        in_specs=[zspec(0), zspec(1), zspec(2), zspec(3),
                  pl.BlockSpec((c, DH_RET), lambda b, i: (i, 0)),
                  pl.BlockSpec((c, DH_RET), lambda b, i: (i, 0)),
                  full((H_RET, c, c)), full((H_RET, c, 1)), full((H_RET, c, 1)), full((H_RET, 1, DH_RET)),
                  full((1, RET_W))],
        out_specs=(pl.BlockSpec((c, RET_W), lambda b, i: (b * nc + i, 0)),
                   pl.BlockSpec((None, H_RET, DH_RET, DH_RET), lambda b, i: (b, 0, 0, 0))),
        compiler_params=_params("parallel", "arbitrary"),
        name="retention_prompt",
    )(z, z, z, z, cos, sin, dmask, xi, zeta, g_c, gn_g.reshape(1, RET_W))


def _retention_sample(zs, state_ret, layer, gn_g, pos0, n_new):
    ns, r, _ = zs.shape
    dmask, xi, zeta, g_c = _ret_consts(r, n_new)
    cos, sin = _rope_tables(pos0 + jnp.arange(r))

    def zspec(col):
        return pl.BlockSpec((None, r, RET_W), lambda b, i: (b, 0, col))

    full = lambda shape: pl.BlockSpec(shape, lambda b, i: (0,) * len(shape))
    return pl.pallas_call(
        functools.partial(_ret_kernel, has_s0=True),
        out_shape=(jax.ShapeDtypeStruct((ns, r, RET_W), F32),
                   jax.ShapeDtypeStruct((ns, H_RET, DH_RET, DH_RET), F32)),
        grid=(ns, 1),
        in_specs=[zspec(0), zspec(1), zspec(2), zspec(3),
                  full((r, DH_RET)), full((r, DH_RET)),
                  full((H_RET, r, r)), full((H_RET, r, 1)), full((H_RET, r, 1)), full((H_RET, 1, DH_RET)),
                  full((1, RET_W)),
                  pl.BlockSpec((None, None, H_RET, DH_RET, DH_RET), lambda b, i: (layer, b, 0, 0, 0))],
        out_specs=(pl.BlockSpec((None, r, RET_W), lambda b, i: (b, 0, 0)),
                   pl.BlockSpec((None, H_RET, DH_RET, DH_RET), lambda b, i: (b, 0, 0, 0))),
        compiler_params=_params("parallel", "arbitrary"),
        name="retention_sample",
    )(zs, zs, zs, zs, cos, sin, dmask, xi, zeta, g_c, gn_g.reshape(1, RET_W), state_ret)


def _s5_mats(a_re, a_im, b_re, b_im, c_re, c_im, d_skip, log_dt, c, c_eff):
    lam = lax.complex(a_re, a_im)
    lam_dt = lam * jnp.exp(log_dt)[:, None]
    lam_bar = jnp.exp(lam_dt)
    b_bar = ((lam_bar - 1.0) / lam)[..., None] * lax.complex(b_re, b_im)
    c_mat = lax.complex(c_re, c_im)
    g, p, cg = b_bar.shape

    def power(n):
        return jnp.exp(lam_dt[None] * jnp.asarray(n, F32)[:, None, None])

    lags = jnp.real(jnp.einsum('gop,jgp,gpi->jgoi', c_mat, power(jnp.arange(c)), b_bar, precision=HI))
    sig = jnp.arange(c)[:, None, None]
    tau = jnp.arange(c)[None, :, None]
    lag_of = (tau - sig == jnp.arange(c)[None, None, :]).astype(F32)
    m = jnp.einsum('stj,jgoi->gsito', lag_of, lags, precision=HI)
    m = m + (jnp.eye(c)[None, :, None, :, None] * jnp.eye(cg)[None, None, :, None, :]
             * d_skip.reshape(g, 1, cg, 1, 1))
    m = m.reshape(g, c * cg, c * cg)
    w = power(c_eff - 1.0 - jnp.arange(c, dtype=F32))[:, :, :, None] * b_bar[None]
    w = jnp.where((jnp.arange(c) < c_eff)[:, None, None, None], w, 0.0)
    w = w.transpose(1, 0, 3, 2).reshape(g, c * cg, p)
    v = power(jnp.arange(c, dtype=F32) + 1.0)[:, :, None, :] * c_mat[None]
    v = v.transpose(1, 3, 0, 2).reshape(g, p, c * cg)
    a = power(jnp.asarray([c_eff], F32))[0]
    return (m.astype(F32), jnp.real(w), jnp.imag(w), jnp.real(v), -jnp.imag(v), jnp.real(a), jnp.imag(a))


def _s5_prompt_kernel(u_ref, m_ref, wre_ref, wim_ref, vre_ref, vim_ref, are_ref, aim_ref,
                      y_ref, hre_ref, him_ref, ere_sc, eim_sc, hsre_sc, hsim_sc):
    ns, nc, _ = u_ref.shape
    p = wre_ref.shape[1]
    ere_sc[...] = jnp.concatenate(
        [jnp.dot(u_ref[s], wre_ref[...], precision=HI, preferred_element_type=F32) for s in range(ns)], axis=1)
    eim_sc[...] = jnp.concatenate(
        [jnp.dot(u_ref[s], wim_ref[...], precision=HI, preferred_element_type=F32) for s in range(ns)], axis=1)
    ar = are_ref[...]
    ai = aim_ref[...]

    def body(k, carry):
        hr, hi = carry
        hsre_sc[pl.ds(k, 1), :] = hr
        hsim_sc[pl.ds(k, 1), :] = hi
        er = ere_sc[pl.ds(k, 1), :]
        ei = eim_sc[pl.ds(k, 1), :]
        return ar * hr - ai * hi + er, ar * hi + ai * hr + ei

    zero = jnp.zeros((1, 2 * SSM_P), F32)
    hr, hi = lax.fori_loop(0, nc, body, (zero, zero))
    hre_ref[...] = hr
    him_ref[...] = hi
    hsre = hsre_sc[...]
    hsim = hsim_sc[...]
    for s in range(ns):
        y_ref[s] = (jnp.dot(u_ref[s], m_ref[...], precision=HI, preferred_element_type=F32)
                    + jnp.dot(hsre[:, s * p:(s + 1) * p], vre_ref[...], precision=HI, preferred_element_type=F32)
                    + jnp.dot(hsim[:, s * p:(s + 1) * p], vim_ref[...], precision=HI, preferred_element_type=F32))


def _s5_prompt(su, mats, nb, seq):
    assert nb == 2
    c = SSM_GROUP
    nc = seq // c
    cw = c * SSM_GROUP
    m, w_re, w_im, v_re, v_im, a_re, a_im = mats
    u = su.reshape(nb, nc, c, SSM_G, SSM_GROUP).transpose(3, 0, 1, 2, 4).reshape(SSM_G, nb, nc, cw)

    gspec = lambda shape: pl.BlockSpec((None,) + shape, lambda g: (g,) + (0,) * len(shape))
    y, hre, him = pl.pallas_call(
        _s5_prompt_kernel,
        out_shape=(jax.ShapeDtypeStruct((SSM_G, nb, nc, cw), F32),
                   jax.ShapeDtypeStruct((SSM_G, 1, nb * SSM_P), F32),
                   jax.ShapeDtypeStruct((SSM_G, 1, nb * SSM_P), F32)),
        grid=(SSM_G,),
        in_specs=[gspec((nb, nc, cw)), gspec((cw, cw)),
                  gspec((cw, SSM_P)), gspec((cw, SSM_P)), gspec((SSM_P, cw)), gspec((SSM_P, cw)),
                  gspec((1, nb * SSM_P)), gspec((1, nb * SSM_P))],
        out_specs=(gspec((nb, nc, cw)), gspec((1, nb * SSM_P)), gspec((1, nb * SSM_P))),
        scratch_shapes=[pltpu.VMEM((nc, nb * SSM_P), F32) for _ in range(4)],
        compiler_params=_params("parallel"),
        name="s5_prompt",
    )(u, m, w_re, w_im, v_re, v_im, jnp.tile(a_re, (1, nb))[:, None, :], jnp.tile(a_im, (1, nb))[:, None, :])
    y = y.reshape(SSM_G, nb, nc, c, SSM_GROUP).transpose(1, 2, 3, 0, 4).reshape(nb * seq, SSM_W)
    unpack = lambda h: h.reshape(SSM_G, nb, SSM_P).transpose(1, 0, 2)
    return y, unpack(hre), unpack(him)


def _s5_sample_kernel(u_ref, h0re_ref, h0im_ref, m_ref, wre_ref, wim_ref, vre_ref, vim_ref, are_ref, aim_ref,
                      y_ref, hre_ref, him_ref):
    u = u_ref[...]
    hr = h0re_ref[...]
    hi = h0im_ref[...]
    ar = are_ref[...]
    ai = aim_ref[...]
    hre_ref[...] = ar * hr - ai * hi + jnp.dot(u, wre_ref[...], precision=HI, preferred_element_type=F32)
    him_ref[...] = ar * hi + ai * hr + jnp.dot(u, wim_ref[...], precision=HI, preferred_element_type=F32)
    y_ref[...] = (jnp.dot(u, m_ref[...], precision=HI, preferred_element_type=F32)
                  + jnp.dot(hr, vre_ref[...], precision=HI, preferred_element_type=F32)
                  + jnp.dot(hi, vim_ref[...], precision=HI, preferred_element_type=F32))


def _s5_sample(su, h0_re, h0_im, mats, n_new):
    ns, r, _ = su.shape
    cw = r * SSM_GROUP
    m, w_re, w_im, v_re, v_im, a_re, a_im = mats
    u = su.reshape(ns, r, SSM_G, SSM_GROUP).transpose(2, 0, 1, 3).reshape(SSM_G, ns, cw)
    gspec = lambda shape: pl.BlockSpec((None,) + shape, lambda g: (g,) + (0,) * len(shape))
    y, hre, him = pl.pallas_call(
        _s5_sample_kernel,
        out_shape=(jax.ShapeDtypeStruct((SSM_G, ns, cw), F32),
                   jax.ShapeDtypeStruct((SSM_G, ns, SSM_P), F32),
                   jax.ShapeDtypeStruct((SSM_G, ns, SSM_P), F32)),
        grid=(SSM_G,),
        in_specs=[gspec((ns, cw)), gspec((ns, SSM_P)), gspec((ns, SSM_P)), gspec((cw, cw)),
                  gspec((cw, SSM_P)), gspec((cw, SSM_P)), gspec((SSM_P, cw)), gspec((SSM_P, cw)),
                  gspec((1, SSM_P)), gspec((1, SSM_P))],
        out_specs=(gspec((ns, cw)), gspec((ns, SSM_P)), gspec((ns, SSM_P))),
        compiler_params=_params("parallel"),
        name="s5_sample",
    )(u, h0_re.transpose(1, 0, 2), h0_im.transpose(1, 0, 2), m, w_re, w_im, v_re, v_im,
      a_re[:, None, :], a_im[:, None, :])
    y = y.reshape(SSM_G, ns, r, SSM_GROUP)[:, :, :n_new].transpose(1, 2, 0, 3).reshape(ns * n_new, SSM_W)
    return y, hre.transpose(1, 0, 2), him.transpose(1, 0, 2)


def _glu_kernel(y_ref, w_ref, b_ref, g_ref, o_ref):
    sy = _gelu(y_ref[...])
    gate = _sigmoid(jnp.dot(sy.astype(BF16), w_ref[...], preferred_element_type=F32) + b_ref[...])
    o_ref[...] = _rms_norm(sy * gate, g_ref[...])


def _s5_glu(y, w_glu, b_glu, norm_g, tm):
    t = y.shape[0]
    row = lambda: pl.BlockSpec((1, SSM_W), lambda i: (0, 0))
    return pl.pallas_call(
        _glu_kernel,
        out_shape=jax.ShapeDtypeStruct((t, SSM_W), F32),
        grid=(t // tm,),
        in_specs=[pl.BlockSpec((tm, SSM_W), lambda i: (i, 0)),
                  pl.BlockSpec((SSM_W, SSM_W), lambda i: (0, 0)), row(), row()],
        out_specs=pl.BlockSpec((tm, SSM_W), lambda i: (i, 0)),
        compiler_params=_params("parallel"),
        name="s5_glu",
    )(y, w_glu.astype(BF16), b_glu.reshape(1, SSM_W), norm_g.reshape(1, SSM_W))


def _outproj_kernel(ro_ref, fo_ref, so_ref, x_ref, wr_ref, wf_ref, ws_ref, g_ref, b_ref, o_ref):
    mix = jnp.dot(ro_ref[...].astype(BF16), wr_ref[...], preferred_element_type=F32)
    mix += jnp.dot(fo_ref[...].astype(BF16), wf_ref[...], preferred_element_type=F32)
    mix += jnp.dot(so_ref[...].astype(BF16), ws_ref[...], preferred_element_type=F32)
    o_ref[...] = _layer_norm(DEEPNORM_ALPHA * x_ref[...] + mix, g_ref[...], b_ref[...])


def _outproj_ln(ro, fo, so, x, w_out, ln_g, ln_b, tm):
    t = x.shape[0]
    w = w_out.astype(BF16)
    tile = lambda n: pl.BlockSpec((tm, n), lambda i: (i, 0))
    full = lambda r, n: pl.BlockSpec((r, n), lambda i: (0, 0))
    return pl.pallas_call(
        _outproj_kernel,
        out_shape=jax.ShapeDtypeStruct((t, D_MODEL), F32),
        grid=(t // tm,),
        in_specs=[tile(RET_W), tile(FOX_W), tile(SSM_W), tile(D_MODEL),
                  full(RET_W, D_MODEL), full(FOX_W, D_MODEL), full(SSM_W, D_MODEL),
                  full(1, D_MODEL), full(1, D_MODEL)],
        out_specs=tile(D_MODEL),
        compiler_params=_params("parallel"),
        name="outproj_ln",
    )(ro, fo, so, x, w[:RET_W], w[RET_W:RET_W + FOX_W], w[RET_W + FOX_W:],
      ln_g.reshape(1, D_MODEL), ln_b.reshape(1, D_MODEL))


def _topk_rows(problems, k):
    def body(it, _):
        for s_ref, val_ref, idx_ref in problems:
            n = s_ref.shape[0]
            iota = lax.broadcasted_iota(jnp.int32, s_ref.shape, 0)
            s = s_ref[...]
            m = jnp.max(s, axis=0, keepdims=True)
            idx = jnp.min(jnp.where(s == m, iota, n), axis=0, keepdims=True)
            val_ref[pl.ds(it, 1), :] = m
            idx_ref[pl.ds(it, 1), :] = idx
            s_ref[...] = jnp.where(iota == idx, -jnp.inf, s)
        return 0

    lax.fori_loop(0, k, body, 0)


CAND_COUNTS = [PEER_TOPK // (a + 1) for a in range(PEER_TOPK)]
CAND_STARTS = [sum(CAND_COUNTS[:a]) for a in range(PEER_TOPK)]
N_CAND = sum(CAND_COUNTS)
N_CAND_PAD = -(-N_CAND // 8) * 8


def _peer_topk_kernel(q_ref, sk_ref, i_ref, j_ref, gate_ref, s0_sc, s1_sc, cand_sc, v0_sc, v1_sc, x0_sc, x1_sc,
                      top_sc, sel_sc):
    kk = PEER_TOPK
    s = lax.dot_general(sk_ref[...], q_ref[...], NT_DIMS, precision=HI, preferred_element_type=F32)
    s0_sc[...] = s[:PEER_NKEYS]
    s1_sc[...] = s[PEER_NKEYS:]
    _topk_rows([(s0_sc, v0_sc, x0_sc), (s1_sc, v1_sc, x1_sc)], kk)
    v1 = v1_sc[...]
    for a in range(kk):
        cand_sc[CAND_STARTS[a]:CAND_STARTS[a] + CAND_COUNTS[a], :] = v0_sc[a:a + 1, :] + v1[:CAND_COUNTS[a]]
    if N_CAND_PAD > N_CAND:
        cand_sc[N_CAND:, :] = jnp.full((N_CAND_PAD - N_CAND, cand_sc.shape[1]), -jnp.inf, F32)
    _topk_rows([(cand_sc, top_sc, sel_sc)], kk)
    sel = sel_sc[...]
    a_idx = jnp.zeros_like(sel)
    start = jnp.zeros_like(sel)
    for a in range(1, kk):
        past = sel >= CAND_STARTS[a]
        a_idx = jnp.where(past, a, a_idx)
        start = jnp.where(past, CAND_STARTS[a], start)
    b_idx = sel - start
    ii = jnp.zeros_like(sel)
    jj = jnp.zeros_like(sel)
    for a in range(kk):
        ii = jnp.where(a_idx == a, x0_sc[a:a + 1, :], ii)
        jj = jnp.where(b_idx == a, x1_sc[a:a + 1, :], jj)
    top = top_sc[...]
    e = jnp.exp(top - top[0:1, :])
    i_ref[...] = ii
    j_ref[...] = jj
    gate_ref[...] = e / jnp.sum(e, axis=0, keepdims=True)


def _peer_topk(q, sub_keys, tm):
    t = q.shape[0]
    half = PEER_DK // 2
    sk = jnp.zeros((PEER_HEADS, 2 * PEER_NKEYS, PEER_DK), F32)
    sk = sk.at[:, :PEER_NKEYS, :half].set(sub_keys[:, 0]).at[:, PEER_NKEYS:, half:].set(sub_keys[:, 1])
    kk = PEER_TOPK
    slot = lambda: pl.BlockSpec((kk, tm), lambda i, h: (h, i))
    return pl.pallas_call(
        _peer_topk_kernel,
        out_shape=(jax.ShapeDtypeStruct((PEER_HEADS * kk, t), jnp.int32),
                   jax.ShapeDtypeStruct((PEER_HEADS * kk, t), jnp.int32),
                   jax.ShapeDtypeStruct((PEER_HEADS * kk, t), F32)),
        grid=(t // tm, PEER_HEADS),
        in_specs=[pl.BlockSpec((tm, PEER_DK), lambda i, h: (i, h)),
                  pl.BlockSpec((None, 2 * PEER_NKEYS, PEER_DK), lambda i, h: (h, 0, 0))],
        out_specs=(slot(), slot(), slot()),
        scratch_shapes=[pltpu.VMEM((PEER_NKEYS, tm), F32), pltpu.VMEM((PEER_NKEYS, tm), F32),
                        pltpu.VMEM((N_CAND_PAD, tm), F32),
                        pltpu.VMEM((kk, tm), F32), pltpu.VMEM((kk, tm), F32),
                        pltpu.VMEM((kk, tm), jnp.int32), pltpu.VMEM((kk, tm), jnp.int32),
                        pltpu.VMEM((kk, tm), F32), pltpu.VMEM((kk, tm), jnp.int32)],
        compiler_params=_params("parallel", "arbitrary"),
        name="peer_topk",
    )(q, sk)


G_TOKENS = 128
G_ROW_STRIDE = G_TOKENS + 8
G_UNROLL = 32


def _peer_gates_kernel(i_ref, j_ref, gate_ref, o_ref, g_sc):
    tb, ns = i_ref.shape
    nk = PEER_NKEYS
    io = lax.broadcasted_iota(jnp.int32, (nk, ns), 0)

    def body(tt, carry):
        for u in range(G_UNROLL):
            t = tt * G_UNROLL + u
            irow = i_ref[pl.ds(t, 1), :]
            jrow = j_ref[pl.ds(t, 1), :]
            grow = gate_ref[pl.ds(t, 1), :]
            pt = jnp.where(io == irow, grow, 0.0).astype(BF16)
            qt = jnp.where(io == jrow, 1.0, 0.0).astype(BF16)
            g = lax.dot_general(pt, qt, NT_DIMS, preferred_element_type=F32)
            g_sc[pl.ds(t, nk, stride=G_ROW_STRIDE), :] = g
        return carry

    lax.fori_loop(0, tb // G_UNROLL, body, 0)
    for i in range(nk):
        o_ref[:, i * nk:(i + 1) * nk] = g_sc[i * G_ROW_STRIDE:i * G_ROW_STRIDE + tb, :].astype(BF16)


def _peer_gates(ii, jj, gate):
    t = ii.shape[0]
    tb = G_TOKENS
    ns = PEER_HEADS * PEER_TOPK
    tok = lambda: pl.BlockSpec((tb, ns), lambda i: (i, 0))
    return pl.pallas_call(
        _peer_gates_kernel,
        out_shape=jax.ShapeDtypeStruct((t, PEER_N), BF16),
        grid=(t // tb,),
        in_specs=[tok(), tok(), tok()],
        out_specs=pl.BlockSpec((tb, PEER_N), lambda i: (i, 0)),
        scratch_shapes=[pltpu.VMEM((PEER_NKEYS * G_ROW_STRIDE, PEER_NKEYS), F32)],
        compiler_params=_params("parallel"),
        name="peer_gates",
    )(ii, jj, gate)


def _peer_experts_kernel(x_ref, g_ref, u_ref, v_ref, lg_ref, lb_ref, o_ref, xb_sc, acc_sc):
    e = pl.program_id(1)

    @pl.when(e == 0)
    def _():
        xb_sc[...] = x_ref[...].astype(BF16)
        acc_sc[...] = jnp.zeros_like(acc_sc)

    h = lax.dot_general(xb_sc[...], u_ref[...], NT_DIMS, preferred_element_type=F32)
    w = (g_ref[...].astype(F32) * _gelu(h)).astype(BF16)
    acc_sc[...] += jnp.dot(w, v_ref[...], preferred_element_type=F32)

    @pl.when(e == pl.num_programs(1) - 1)
    def _():
        o_ref[...] = _layer_norm(DEEPNORM_ALPHA * x_ref[...] + acc_sc[...], lg_ref[...], lb_ref[...])


def _peer_experts_ln(x, g, u_tabs, v_tabs, layer, ln_g, ln_b, tm, nb):
    t = x.shape[0]
    return pl.pallas_call(
        _peer_experts_kernel,
        out_shape=jax.ShapeDtypeStruct((t, D_MODEL), F32),
        grid=(t // tm, PEER_N // nb),
        in_specs=[pl.BlockSpec((tm, D_MODEL), lambda i, e: (i, 0)),
                  pl.BlockSpec((tm, nb), lambda i, e: (i, e)),
                  pl.BlockSpec((None, nb, D_MODEL), lambda i, e: (layer, e, 0)),
                  pl.BlockSpec((None, nb, D_MODEL), lambda i, e: (layer, e, 0)),
                  pl.BlockSpec((1, D_MODEL), lambda i, e: (0, 0)),
                  pl.BlockSpec((1, D_MODEL), lambda i, e: (0, 0))],
        out_specs=pl.BlockSpec((tm, D_MODEL), lambda i, e: (i, 0)),
        scratch_shapes=[pltpu.VMEM((tm, D_MODEL), BF16), pltpu.VMEM((tm, D_MODEL), F32)],
        compiler_params=_params("parallel", "arbitrary"),
        name="peer_experts_ln",
    )(x, g, u_tabs, v_tabs, ln_g.reshape(1, D_MODEL), ln_b.reshape(1, D_MODEL))


def _ple_kernel(x_ref, xn_ref, p_ref, wg_ref, bg_ref, we_ref, o_ref):
    gate = _sigmoid(jnp.dot(x_ref[...].astype(BF16), wg_ref[...], preferred_element_type=F32) + bg_ref[...])
    emb = jnp.dot(p_ref[...].astype(BF16), we_ref[...], preferred_element_type=F32)
    o_ref[...] = xn_ref[...] + gate * emb


def _ple(x, p, w_pg, b_pg, w_pe, tm, tn):
    t = x.shape[0]
    return pl.pallas_call(
        _ple_kernel,
        out_shape=jax.ShapeDtypeStruct((t, D_MODEL), F32),
        grid=(t // tm, D_MODEL // tn),
        in_specs=[pl.BlockSpec((tm, D_MODEL), lambda i, j: (i, 0)),
                  pl.BlockSpec((tm, tn), lambda i, j: (i, j)),
                  pl.BlockSpec((tm, PLE_DIM), lambda i, j: (i, 0)),
                  pl.BlockSpec((D_MODEL, tn), lambda i, j: (0, j)),
                  pl.BlockSpec((1, tn), lambda i, j: (0, j)),
                  pl.BlockSpec((PLE_DIM, tn), lambda i, j: (0, j))],
        out_specs=pl.BlockSpec((tm, tn), lambda i, j: (i, j)),
        compiler_params=_params("parallel", "arbitrary"),
        name="ple_gate",
    )(x, x, p, w_pg.astype(BF16), b_pg.reshape(1, D_MODEL), w_pe.astype(BF16))


def _split_bf16(w):
    hi = w.astype(BF16)
    return hi, (w - hi.astype(F32)).astype(BF16)


def _layer(x, p, lw, caches, stacked, page_table, layer, nb, seq, ns, n_new, tiles):
    t_all = x.shape[0]
    tp = nb * seq
    r = 8
    cache_k, cache_v, lf_sums, state_ret, state_re, state_im = caches
    past_len = page_table.shape[1] * (cache_k.shape[2] // H_FOX)

    u_tabs, v_tabs, w_in_all = stacked
    z = _in_proj(x, w_in_all, layer, tiles['tm_in'], 512)
    lf = _forget_gates(x, w_in_all[layer, :, Z_FF:Z_SU], lw['b_fox_f'], tiles['tm'])
    zs = jnp.pad(z[tp:].reshape(ns, n_new, Z_W), ((0, 0), (0, r - n_new), (0, 0)))

    ro, ret_p = _retention_prompt(z, lw['ret_gn_g'], nb, seq, t_all)
    ro_s, ret_s = _retention_sample(zs, state_ret, layer, lw['ret_gn_g'], past_len, n_new)
    ro = ro.at[tp:].set(ro_s[:, :n_new].reshape(ns * n_new, RET_W))

    fo = _fox_prompt(z, _forget_cumsum(lf, nb, seq), lw['fox_norm_g'], nb, seq, t_all, tiles['attn'])
    lf_s = lf[:, tp:].T.reshape(ns, n_new, H_FOX)
    lfn = jnp.pad(lf_s, ((0, 0), (0, r - n_new), (0, 0))).reshape(ns, 1, r * H_FOX)
    fo_s = _fox_sample(z[tp:, Z_FQ:Z_FQ + FOX_W].reshape(ns, n_new * H_FOX, DH_FOX),
                       zs[:, :, Z_FK:Z_FK + FOX_W].reshape(ns, r * H_FOX, DH_FOX),
                       zs[:, :, Z_FV:Z_FV + FOX_W].reshape(ns, r * H_FOX, DH_FOX),
                       lfn, cache_k, cache_v, lf_sums, page_table, layer, lw['fox_norm_g'])
    fo = fo.at[tp:].set(fo_s.reshape(ns * n_new, FOX_W))

    ssm = (lw['ssm_a_re'], lw['ssm_a_im'], lw['ssm_b_re'], lw['ssm_b_im'], lw['ssm_c_re'], lw['ssm_c_im'],
           lw['ssm_d'], lw['ssm_log_dt'])
    y_p, hre_p, him_p = _s5_prompt(z[:tp, Z_SU:], _s5_mats(*ssm, SSM_GROUP, SSM_GROUP), nb, seq)
    y_s, hre_s, him_s = _s5_sample(zs[:, :, Z_SU:], state_re[layer], state_im[layer], _s5_mats(*ssm, r, n_new), n_new)
    so = _s5_glu(jnp.concatenate([y_p, y_s], 0), lw['w_glu'], lw['b_glu'], lw['ssm_norm_g'], tiles['tm'])

    x1 = _outproj_ln(ro, fo, so, x, lw['w_out'], lw['ln1_g'], lw['ln1_b'], tiles['tm_ln'])

    q = _matmul(x1, _split_bf16(lw['peer_w_q']), tiles['tm'], 512, "peer_query")
    ii, jj, gate = _peer_topk(q, lw['peer_sub_keys'], tiles['tm'])
    g = _peer_gates(ii.T, jj.T, gate.T)
    x2 = _peer_experts_ln(x1, g, u_tabs, v_tabs, layer, lw['ln2_g'], lw['ln2_b'], tiles['tm'], 512)

    x3 = _ple(x2, p, lw['w_pg'], lw['b_pg'], lw['w_pe'], tiles['tm'], 512)

    st_p = (z[:tp, Z_FK:Z_FK + FOX_W].reshape(nb, seq, H_FOX, DH_FOX),
            z[:tp, Z_FV:Z_FV + FOX_W].reshape(nb, seq, H_FOX, DH_FOX),
            lf[:, :tp].T.reshape(nb, seq, H_FOX), ret_p, hre_p, him_p)
    st_s = (z[tp:, Z_FK:Z_FK + FOX_W].reshape(ns, n_new, H_FOX, DH_FOX),
            z[tp:, Z_FV:Z_FV + FOX_W].reshape(ns, n_new, H_FOX, DH_FOX),
            lf_s, ret_s, hre_s, him_s)
    return x3, st_p, st_s


def kernel(x_prompt, x_sample, p_prompt, p_sample, cache_fox_k, cache_fox_v, cache_fox_lf, state_ret, state_ssm_re, state_ssm_im, page_table, w_in, w_out, ret_gn_g, b_fox_f, fox_norm_g, ssm_a_re, ssm_a_im, ssm_b_re, ssm_b_im, ssm_c_re, ssm_c_im, ssm_d, ssm_log_dt, w_glu, b_glu, ssm_norm_g, ln1_g, ln1_b, ln2_g, ln2_b, peer_w_q, peer_sub_keys, peer_u, peer_v, w_pe, w_pg, b_pg):
    nb, seq, d = x_prompt.shape
    ns, n_new, _ = x_sample.shape
    tp, ts = nb * seq, ns * n_new
    t_all = tp + ts
    tm = 640 if t_all % 640 == 0 else 128
    tiles = dict(tm=tm, tm_ln=tm // 2 if tm == 640 else tm, tm_in=832 if t_all % 832 == 0 else tm,
                 attn=256 if seq % 256 == 0 else 128)
    x = jnp.concatenate([x_prompt.reshape(tp, d), x_sample.reshape(ts, d)], 0)
    depth, n_pool, pg, nh, dh = cache_fox_k.shape
    caches = (cache_fox_k.reshape(depth, n_pool, pg * nh, dh), cache_fox_v.reshape(depth, n_pool, pg * nh, dh),
              _lf_prep(cache_fox_lf), state_ret, state_ssm_re, state_ssm_im)
    weights = dict(w_out=w_out, ret_gn_g=ret_gn_g, b_fox_f=b_fox_f, fox_norm_g=fox_norm_g,
                   ssm_a_re=ssm_a_re, ssm_a_im=ssm_a_im, ssm_b_re=ssm_b_re, ssm_b_im=ssm_b_im,
                   ssm_c_re=ssm_c_re, ssm_c_im=ssm_c_im, ssm_d=ssm_d, ssm_log_dt=ssm_log_dt,
                   w_glu=w_glu, b_glu=b_glu, ssm_norm_g=ssm_norm_g, ln1_g=ln1_g, ln1_b=ln1_b,
                   ln2_g=ln2_g, ln2_b=ln2_b, peer_w_q=peer_w_q, peer_sub_keys=peer_sub_keys,
                   w_pe=w_pe, w_pg=w_pg, b_pg=b_pg)
    stacked = (peer_u.astype(BF16), peer_v.astype(BF16), w_in)
    sts_p, sts_s = [], []
    for l in range(DEPTH):
        lw = {k: v[l] for k, v in weights.items()}
        p = jnp.concatenate([p_prompt[l].reshape(tp, PLE_DIM), p_sample[l].reshape(ts, PLE_DIM)], 0)
        x, st_p, st_s = _layer(x, p, lw, caches, stacked, page_table, l, nb, seq, ns, n_new, tiles)
        sts_p.append(st_p)
        sts_s.append(st_s)
    stack = lambda sts, i: jnp.stack([s[i] for s in sts])
    return ((x[:tp].reshape(nb, seq, d), x[tp:].reshape(ns, n_new, d))
            + tuple(stack(sts_p, i) for i in range(6)) + tuple(stack(sts_s, i) for i in range(6)))
```

```python
import functools
import math

import jax
import jax.numpy as jnp
from jax import lax
from jax.experimental import pallas as pl
from jax.experimental.pallas import tpu as pltpu

F32 = jnp.float32
BF16 = jnp.bfloat16
HI = lax.Precision.HIGHEST

D_MODEL = 2048
DEPTH = 2
H_RET, DH_RET = 4, 128
RET_W = H_RET * DH_RET
H_FOX, DH_FOX = 8, 128
FOX_W = H_FOX * DH_FOX
SSM_W, SSM_GROUP = 512, 16
SSM_G = SSM_W // SSM_GROUP
SSM_P = 64
PEER_HEADS, PEER_NKEYS, PEER_DK, PEER_TOPK = 8, 128, 128, 16
PEER_N = PEER_NKEYS * PEER_NKEYS
PLE_DIM = 256
BLOCK = 128
ROPE_BASE = 10000.0
DEEPNORM_ALPHA = (2 * DEPTH) ** 0.25
EPS = 1e-5

Z_FQ, Z_FK, Z_FV, Z_FF = 4 * RET_W, 4 * RET_W + FOX_W, 4 * RET_W + 2 * FOX_W, 4 * RET_W + 3 * FOX_W
Z_SU = Z_FF + H_FOX
Z_W = Z_SU + SSM_W

VMEM_LIMIT = 56 * 1024 * 1024
NT_DIMS = (((1,), (1,)), ((), ()))
TN_DIMS = (((0,), (0,)), ((), ()))


def _params(*sem):
    return pltpu.CompilerParams(dimension_semantics=sem, vmem_limit_bytes=VMEM_LIMIT)


def _gelu(x):
    return 0.5 * x * (1.0 + jnp.tanh(math.sqrt(2.0 / math.pi) * (x + 0.044715 * (x * x * x))))


def _sigmoid(x):
    return 1.0 / (1.0 + jnp.exp(-x))


def _log_sigmoid(x):
    return jnp.minimum(x, 0.0) - jnp.log1p(jnp.exp(-jnp.abs(x)))


def _layer_norm(y, g, b):
    mu = jnp.mean(y, -1, keepdims=True)
    var = jnp.mean(jnp.square(y - mu), -1, keepdims=True)
    return (y - mu) * lax.rsqrt(var + EPS) * g + b


def _rms_norm(y, g):
    return y * lax.rsqrt(jnp.mean(y * y, -1, keepdims=True) + EPS) * g


def _mm_kernel(x_ref, w_ref, o_ref):
    o_ref[...] = jnp.dot(x_ref[...].astype(BF16), w_ref[...], preferred_element_type=F32)


def _mm3_kernel(x_ref, wh_ref, wl_ref, o_ref):
    x = x_ref[...]
    xh = x.astype(BF16)
    xl = (x - xh.astype(F32)).astype(BF16)
    wh = wh_ref[...]
    acc = jnp.dot(xh, wh, preferred_element_type=F32)
    acc += jnp.dot(xl, wh, preferred_element_type=F32)
    acc += jnp.dot(xh, wl_ref[...], preferred_element_type=F32)
    o_ref[...] = acc


def _matmul(x, ws, tm, tn, name):
    t, k = x.shape
    n = ws[0].shape[1]
    kern = _mm_kernel if len(ws) == 1 else _mm3_kernel
    return pl.pallas_call(
        kern,
        out_shape=jax.ShapeDtypeStruct((t, n), F32),
        grid=(t // tm, n // tn),
        in_specs=[pl.BlockSpec((tm, k), lambda i, j: (i, 0))]
        + [pl.BlockSpec((k, tn), lambda i, j: (0, j)) for _ in ws],
        out_specs=pl.BlockSpec((tm, tn), lambda i, j: (i, j)),
        compiler_params=_params("parallel", "arbitrary"),
        name=name,
    )(x, *ws)


def _mm_w32_kernel(x_ref, w_ref, o_ref, xb_sc):
    @pl.when(pl.program_id(1) == 0)
    def _():
        xb_sc[...] = x_ref[...].astype(BF16)

    o_ref[...] = jnp.dot(xb_sc[...], w_ref[...], preferred_element_type=F32)


def _in_proj(x, w_all, layer, tm, tn):
    t, k = x.shape
    n = w_all.shape[2]
    return pl.pallas_call(
        _mm_w32_kernel,
        out_shape=jax.ShapeDtypeStruct((t, n), F32),
        grid=(t // tm, pl.cdiv(n, tn)),
        in_specs=[pl.BlockSpec((tm, k), lambda i, j: (i, 0)),
                  pl.BlockSpec((None, k, tn), lambda i, j: (layer, 0, j))],
        out_specs=pl.BlockSpec((tm, tn), lambda i, j: (i, j)),
        scratch_shapes=[pltpu.VMEM((tm, k), BF16)],
        compiler_params=_params("parallel", "arbitrary"),
        name="in_proj",
    )(x, w_all)


def _gate_kernel(x_ref, w_ref, b_ref, lf_ref):
    f = lax.dot_general(w_ref[...], x_ref[...], NT_DIMS, precision=HI, preferred_element_type=F32)
    lf_ref[...] = _log_sigmoid(f + b_ref[...])


def _forget_gates(x, w_ff, b_ff, tm):
    t, k = x.shape
    return pl.pallas_call(
        _gate_kernel,
        out_shape=jax.ShapeDtypeStruct((H_FOX, t), F32),
        grid=(t // tm,),
        in_specs=[
            pl.BlockSpec((tm, k), lambda i: (i, 0)),
            pl.BlockSpec((H_FOX, k), lambda i: (0, 0)),
            pl.BlockSpec((H_FOX, 1), lambda i: (0, 0)),
        ],
        out_specs=pl.BlockSpec((H_FOX, tm), lambda i: (0, i)),
        compiler_params=_params("parallel"),
        name="forget_gates",
    )(x, w_ff.T, b_ff.reshape(H_FOX, 1))


def _cumsum_kernel(lf_ref, c_ref, carry):
    @pl.when(pl.program_id(1) == 0)
    def _():
        carry[...] = jnp.zeros_like(carry)

    n = lf_ref.shape[1]
    r = lax.broadcasted_iota(jnp.int32, (n, n), 0)
    c = lax.broadcasted_iota(jnp.int32, (n, n), 1)
    upper = (r <= c).astype(F32)
    cs = jnp.dot(lf_ref[...], upper, precision=HI, preferred_element_type=F32) + carry[...]
    c_ref[...] = cs
    carry[...] = cs[:, n - 1:n]


def _forget_cumsum(lf, nb, seq):
    n = 256 if seq % 256 == 0 else 128
    nc = seq // n
    return pl.pallas_call(
        _cumsum_kernel,
        out_shape=jax.ShapeDtypeStruct((H_FOX, nb * seq), F32),
        grid=(nb, nc),
        in_specs=[pl.BlockSpec((H_FOX, n), lambda b, i: (0, b * nc + i))],
        out_specs=pl.BlockSpec((H_FOX, n), lambda b, i: (0, b * nc + i)),
        scratch_shapes=[pltpu.VMEM((H_FOX, 1), F32)],
        compiler_params=_params("parallel", "arbitrary"),
        name="forget_cumsum",
    )(lf)


def _fox_prompt_kernel(q_ref, k_ref, v_ref, c_ref, g_ref, o_ref, qs_sc, m_sc, l_sc, acc_sc, *, scale):
    qi = pl.program_id(1)
    ki = pl.program_id(2)
    bq = q_ref.shape[0]
    bk = k_ref.shape[0]
    log2e = math.log2(math.e)

    @pl.when(ki == 0)
    def _():
        qs_sc[...] = (q_ref[...] * (scale * log2e)).astype(BF16)
        m_sc[...] = jnp.full_like(m_sc, -jnp.inf)
        l_sc[...] = jnp.zeros_like(l_sc)
        acc_sc[...] = jnp.zeros_like(acc_sc)

    last = (qi * bq + bq - 1) // bk

    def attend(diagonal):
        if diagonal:
            causal = (ki * bk + lax.broadcasted_iota(jnp.int32, (bq, bk), 1)
                      <= qi * bq + lax.broadcasted_iota(jnp.int32, (bq, bk), 0))
        c2 = c_ref[...] * log2e
        for h in range(H_FOX):
            sl = slice(h * DH_FOX, (h + 1) * DH_FOX)
            k = k_ref[:, sl].astype(BF16)
            s = lax.dot_general(qs_sc[:, sl], k, NT_DIMS, preferred_element_type=F32) - c2[h:h + 1, :]
            if diagonal:
                s = jnp.where(causal, s, -jnp.inf)
            m_prev = m_sc[h]
            m_new = jnp.maximum(m_prev, jnp.max(s, axis=1, keepdims=True))
            alpha = jnp.exp2(m_prev - m_new)
            p = jnp.exp2(s - jnp.tile(m_new, (1, bk // 128)))
            l_sc[h] = alpha * l_sc[h] + jnp.sum(p, axis=1, keepdims=True)
            acc_sc[:, sl] = alpha * acc_sc[:, sl] + jnp.dot(
                p.astype(BF16), v_ref[:, sl].astype(BF16), preferred_element_type=F32)
            m_sc[h] = m_new

    @pl.when(ki < last)
    def _():
        attend(False)

    @pl.when(ki == last)
    def _():
        attend(True)
        for h in range(H_FOX):
            sl = slice(h * DH_FOX, (h + 1) * DH_FOX)
            acc_sc[:, sl] = acc_sc[:, sl] / l_sc[h]
        o_ref[...] = _rms_norm(acc_sc[...], g_ref[...])


def _fox_prompt(z, c, norm_g, nb, seq, t_all, blk):
    bk = blk
    nq, nk = seq // blk, seq // bk
    cq, ck, cv = Z_FQ // FOX_W, Z_FK // FOX_W, Z_FV // FOX_W
    kern = functools.partial(_fox_prompt_kernel, scale=DH_FOX ** -0.5)

    def key_block(b, i, j):
        return b * nk + jnp.minimum(j, (i * blk + blk - 1) // bk)

    return pl.pallas_call(
        kern,
        out_shape=jax.ShapeDtypeStruct((t_all, FOX_W), F32),
        grid=(nb, nq, nk),
        in_specs=[
            pl.BlockSpec((blk, FOX_W), lambda b, i, j: (b * nq + i, cq)),
            pl.BlockSpec((bk, FOX_W), lambda b, i, j: (key_block(b, i, j), ck)),
            pl.BlockSpec((bk, FOX_W), lambda b, i, j: (key_block(b, i, j), cv)),
            pl.BlockSpec((H_FOX, bk), lambda b, i, j: (0, key_block(b, i, j))),
            pl.BlockSpec((1, FOX_W), lambda b, i, j: (0, 0)),
        ],
        out_specs=pl.BlockSpec((blk, FOX_W), lambda b, i, j: (b * nq + i, 0)),
        scratch_shapes=[
            pltpu.VMEM((blk, FOX_W), BF16),
            pltpu.VMEM((H_FOX, blk, 128), F32),
            pltpu.VMEM((H_FOX, blk, 128), F32),
            pltpu.VMEM((blk, FOX_W), F32),
        ],
        compiler_params=_params("parallel", "parallel", "arbitrary"),
        name="fox_prompt",
    )(z, z, z, c, norm_g.reshape(1, FOX_W))


def _lf_prep_kernel(lf_ref, w_ref, o_ref):
    x = lf_ref[...]
    p1 = x.astype(BF16)
    r1 = x - p1.astype(F32)
    p2 = r1.astype(BF16)
    p3 = (r1 - p2.astype(F32)).astype(BF16)
    w = w_ref[...]
    acc = jnp.dot(p3, w, preferred_element_type=F32)
    acc += jnp.dot(p2, w, preferred_element_type=F32)
    acc += jnp.dot(p1, w, preferred_element_type=F32)
    o_ref[...] = acc


def _lf_prep(cache_lf):
    depth, n_pool, pg, nh = cache_lf.shape
    c = pg * nh
    rows = depth * n_pool
    tm = max(d for d in (256, 128, 64, 32, 16, 8) if rows % d == 0)
    src = jnp.arange(c)[:, None]
    dst = jnp.arange(c)[None, :]
    same_head = (src % nh) == (dst % nh)
    w = jnp.concatenate([same_head & (src // nh > dst // nh), same_head], axis=1).astype(BF16)
    return pl.pallas_call(
        _lf_prep_kernel,
        out_shape=jax.ShapeDtypeStruct((rows, 2 * c), F32),
        grid=(rows // tm,),
        in_specs=[pl.BlockSpec((tm, c), lambda i: (i, 0)), pl.BlockSpec((c, 2 * c), lambda i: (0, 0))],
        out_specs=pl.BlockSpec((tm, 2 * c), lambda i: (i, 0)),
        compiler_params=_params("parallel"),
        name="lf_prep",
    )(cache_lf.reshape(rows, c), w)


def _fox_sample_kernel(pt_ref, q_ref, kn_ref, vn_ref, lfn_ref, g_ref, *rest, scale, npg, page_of):
    k_refs, v_refs, x_refs = rest[:npg], rest[npg:2 * npg], rest[2 * npg:3 * npg]
    o_ref, m_sc, l_sc, acc_sc, carry_sc = rest[3 * npg:]
    step = pl.program_id(1)
    nr = q_ref.shape[0]
    ncol = k_refs[0].shape[0]
    head_mask = H_FOX - 1
    head_bits = H_FOX.bit_length() - 1

    @pl.when(step == 0)
    def _():
        m_sc[...] = jnp.full_like(m_sc, -jnp.inf)
        l_sc[...] = jnp.zeros_like(l_sc)
        acc_sc[...] = jnp.zeros_like(acc_sc)
        carry_sc[...] = jnp.zeros_like(carry_sc)

    def equal_heads(ncol):
        rows = lax.broadcasted_iota(jnp.int32, (nr, ncol), 0)
        cols = lax.broadcasted_iota(jnp.int32, (nr, ncol), 1)
        return rows, cols, (rows & head_mask) == (cols & head_mask)

    q = q_ref[...].astype(BF16)
    _, _, same_head = equal_heads(ncol)
    carry = carry_sc[...]
    m_prev = m_sc[...]
    m_new = m_prev
    scores = []
    for i in range(npg):
        x = x_refs[i][pl.ds(page_of(pl.program_id(0), step, i, pt_ref) % 8, 1), :]
        bias = x[:, :ncol] + carry
        carry = carry + x[:, ncol:]
        s = lax.dot_general(q, k_refs[i][...].astype(BF16), NT_DIMS, preferred_element_type=F32) * scale + bias
        s = jnp.where(same_head, s, -jnp.inf)
        m_new = jnp.maximum(m_new, jnp.max(s, axis=1, keepdims=True))
        scores.append(s)
    carry_sc[...] = carry
    alpha = jnp.exp(m_prev - m_new)
    l_new = alpha * l_sc[...]
    acc = alpha * acc_sc[...]
    for i in range(npg):
        p = jnp.exp(scores[i] - m_new)
        l_new += jnp.sum(p, axis=1, keepdims=True)
        acc += jnp.dot(p.astype(BF16), v_refs[i][...].astype(BF16), preferred_element_type=F32)
    m_sc[...] = m_new
    l_sc[...] = l_new
    acc_sc[...] = acc

    @pl.when(step == pl.num_programs(1) - 1)
    def _():
        nc = kn_ref.shape[0]
        rows, cols, same = equal_heads(nc)
        causal = (cols >> head_bits) <= (rows >> head_bits)
        r2 = lax.broadcasted_iota(jnp.int32, (nc, nc), 0)
        c2 = lax.broadcasted_iota(jnp.int32, (nc, nc), 1)
        prefix = jnp.where((r2 & head_mask) == (c2 & head_mask), jnp.where(r2 <= c2, 1.0, 0.0), 0.0)
        cn = jnp.dot(jnp.broadcast_to(lfn_ref[...], (8, nc)), prefix, precision=HI,
                     preferred_element_type=F32)[0:1]
        s = lax.dot_general(q, kn_ref[...].astype(BF16), NT_DIMS, preferred_element_type=F32) * scale - cn
        s = jnp.where(same, jnp.where(causal, s, -jnp.inf), -jnp.inf)
        m_fin = jnp.maximum(m_new, jnp.max(s, axis=1, keepdims=True))
        a_fin = jnp.exp(m_new - m_fin)
        p = jnp.exp(s - m_fin)
        l_fin = a_fin * l_new + jnp.sum(p, axis=1, keepdims=True)
        out = (a_fin * acc + jnp.dot(p.astype(BF16), vn_ref[...].astype(BF16),
                                     preferred_element_type=F32)) / l_fin
        r3 = lax.broadcasted_iota(jnp.int32, (nr, nr), 0)
        c3 = lax.broadcasted_iota(jnp.int32, (nr, nr), 1)
        same_token = jnp.where((r3 >> head_bits) == (c3 >> head_bits), 1.0, 0.0)
        ssq = jnp.sum(jnp.dot(same_token, out * out, precision=HI, preferred_element_type=F32),
                      axis=1, keepdims=True)
        o_ref[...] = out * lax.rsqrt(ssq / (H_FOX * DH_FOX) + EPS) * g_ref[...]


def _fox_sample(q, kn, vn, lfn, cache_k, cache_v, lf_sums, page_table, layer, norm_g):
    ns, nr, _ = q.shape
    nc = kn.shape[1]
    n_pages = page_table.shape[1]
    n_pool, cols = cache_k.shape[1], cache_k.shape[2]
    assert n_pool % 8 == 0 and lf_sums.shape == (cache_k.shape[0] * n_pool, 2 * cols)
    npg = max(d for d in range(1, 9) if n_pages % d == 0)
    g = jnp.tile(norm_g.reshape(H_FOX, DH_FOX), (nr // H_FOX, 1))

    def page_of(b, s, i, pt):
        return pt[b * n_pages + (n_pages - 1 - (s * npg + i))]

    kern = functools.partial(_fox_sample_kernel, scale=DH_FOX ** -0.5, npg=npg, page_of=page_of)

    def paged(i):
        return pl.BlockSpec((None, None, cols, DH_FOX), lambda b, s, pt: (layer, page_of(b, s, i, pt), 0, 0))

    def paged_sums(i):
        return pl.BlockSpec((8, 2 * cols), lambda b, s, pt: ((layer * n_pool + page_of(b, s, i, pt)) // 8, 0))

    grid_spec = pltpu.PrefetchScalarGridSpec(
        num_scalar_prefetch=1,
        grid=(ns, n_pages // npg),
        in_specs=[
            pl.BlockSpec((None, nr, DH_FOX), lambda b, s, pt: (b, 0, 0)),
            pl.BlockSpec((None, nc, DH_FOX), lambda b, s, pt: (b, 0, 0)),
            pl.BlockSpec((None, nc, DH_FOX), lambda b, s, pt: (b, 0, 0)),
            pl.BlockSpec((None, 1, nc), lambda b, s, pt: (b, 0, 0)),
            pl.BlockSpec((nr, DH_FOX), lambda b, s, pt: (0, 0)),
        ] + [paged(i) for i in range(npg)] * 2 + [paged_sums(i) for i in range(npg)],
        out_specs=pl.BlockSpec((None, nr, DH_FOX), lambda b, s, pt: (b, 0, 0)),
        scratch_shapes=[
            pltpu.VMEM((nr, 1), F32),
            pltpu.VMEM((nr, 1), F32),
            pltpu.VMEM((nr, DH_FOX), F32),
            pltpu.VMEM((1, cols), F32),
        ],
    )
    return pl.pallas_call(
        kern,
        out_shape=jax.ShapeDtypeStruct((ns, nr, DH_FOX), F32),
        grid_spec=grid_spec,
        compiler_params=_params("parallel", "arbitrary"),
        name="fox_sample",
    )(page_table.reshape(-1), q, kn, vn, lfn, g,
      *([cache_k] * npg), *([cache_v] * npg), *([lf_sums] * npg))


def _ret_kernel(*refs, has_s0):
    if has_s0:
        (q_ref, k_ref, v_ref, gt_ref, cos_ref, sin_ref, dm_ref, xi_ref, zt_ref, gc_ref, gn_ref, s0_ref,
         o_ref, s_ref) = refs
    else:
        (q_ref, k_ref, v_ref, gt_ref, cos_ref, sin_ref, dm_ref, xi_ref, zt_ref, gc_ref, gn_ref,
         o_ref, s_ref) = refs

    @pl.when(pl.program_id(1) == 0)
    def _():
        if has_s0:
            s_ref[...] = s0_ref[...]
        else:
            s_ref[...] = jnp.zeros_like(s_ref)

    cos = cos_ref[...]
    sin = sin_ref[...]

    def rot(x):
        return x * cos + pltpu.roll(x, DH_RET // 2, 1) * sin

    for h in range(H_RET):
        sl = slice(h * DH_RET, (h + 1) * DH_RET)
        q = rot(q_ref[:, sl])
        k = rot(k_ref[:, sl]) * DH_RET ** -0.5
        v = v_ref[:, sl]
        s = s_ref[h]
        att = lax.dot_general(q, k, NT_DIMS, preferred_element_type=F32) * dm_ref[h]
        o = jnp.dot(att, v, preferred_element_type=F32) + jnp.dot(q * xi_ref[h], s, preferred_element_type=F32)
        s_ref[h] = gc_ref[h] * s + lax.dot_general(k * zt_ref[h], v, TN_DIMS, preferred_element_type=F32)
        mu = jnp.mean(o, -1, keepdims=True)
        var = jnp.mean(jnp.square(o - mu), -1, keepdims=True)
        on = (o - mu) * lax.rsqrt(var + EPS)
        gt = gt_ref[:, sl]
        o_ref[:, sl] = on * gn_ref[:, sl] * (gt * _sigmoid(gt))


def _ret_consts(c, c_eff):
    lg = jnp.log1p(-jnp.exp2(-5.0 - jnp.arange(H_RET, dtype=F32)))
    idx = jnp.arange(c, dtype=F32)
    rel = idx[:, None] - idx[None, :]
    dmask = jnp.where(rel >= 0, jnp.exp(lg[:, None, None] * jnp.maximum(rel, 0.0)), 0.0)
    xi = jnp.exp(lg[:, None] * (idx + 1.0))[..., None]
    zeta = jnp.exp(lg[:, None] * (c_eff - 1.0 - idx))[..., None]
    g_c = jnp.broadcast_to(jnp.exp(lg * c_eff)[:, None, None], (H_RET, 1, DH_RET))
    return dmask, xi, zeta, g_c


def _rope_tables(pos):
    inv = ROPE_BASE ** (-jnp.arange(0, DH_RET, 2, dtype=F32) / DH_RET)
    ang = pos.astype(F32)[:, None] * inv[None, :]
    cos, sin = jnp.cos(ang), jnp.sin(ang)
    return jnp.concatenate([cos, cos], -1), jnp.concatenate([-sin, sin], -1)


def _retention_prompt(z, gn_g, nb, seq, t_all):
    c = BLOCK
    nc = seq // c
    dmask, xi, zeta, g_c = _ret_consts(c, c)
    cos, sin = _rope_tables(jnp.arange(seq))

    def zspec(col):
        return pl.BlockSpec((c, RET_W), lambda b, i: (b * nc + i, col))

    full = lambda shape: pl.BlockSpec(shape, lambda b, i: (0,) * len(shape))
    return pl.pallas_call(
        functools.partial(_ret_kernel, has_s0=False),
        out_shape=(jax.ShapeDtypeStruct((t_all, RET_W), F32),
                   jax.ShapeDtypeStruct((nb, H_RET, DH_RET, DH_RET), F32)),
        grid=(nb, nc),
        in_specs=[zspec(0), zspec(1), zspec(2), zspec(3),
                  pl.BlockSpec((c, DH_RET), lambda b, i: (i, 0)),
                  pl.BlockSpec((c, DH_RET), lambda b, i: (i, 0)),
                  full((H_RET, c, c)), full((H_RET, c, 1)), full((H_RET, c, 1)), full((H_RET, 1, DH_RET)),
                  full((1, RET_W))],
        out_specs=(pl.BlockSpec((c, RET_W), lambda b, i: (b * nc + i, 0)),
                   pl.BlockSpec((None, H_RET, DH_RET, DH_RET), lambda b, i: (b, 0, 0, 0))),
        compiler_params=_params("parallel", "arbitrary"),
        name="retention_prompt",
    )(z, z, z, z, cos, sin, dmask, xi, zeta, g_c, gn_g.reshape(1, RET_W))


def _retention_sample(zs, state_ret, layer, gn_g, pos0, n_new):
    ns, r, _ = zs.shape
    dmask, xi, zeta, g_c = _ret_consts(r, n_new)
    cos, sin = _rope_tables(pos0 + jnp.arange(r))

    def zspec(col):
        return pl.BlockSpec((None, r, RET_W), lambda b, i: (b, 0, col))

    full = lambda shape: pl.BlockSpec(shape, lambda b, i: (0,) * len(shape))
    return pl.pallas_call(
        functools.partial(_ret_kernel, has_s0=True),
        out_shape=(jax.ShapeDtypeStruct((ns, r, RET_W), F32),
                   jax.ShapeDtypeStruct((ns, H_RET, DH_RET, DH_RET), F32)),
        grid=(ns, 1),
        in_specs=[zspec(0), zspec(1), zspec(2), zspec(3),
                  full((r, DH_RET)), full((r, DH_RET)),
                  full((H_RET, r, r)), full((H_RET, r, 1)), full((H_RET, r, 1)), full((H_RET, 1, DH_RET)),
                  full((1, RET_W)),
                  pl.BlockSpec((None, None, H_RET, DH_RET, DH_RET), lambda b, i: (layer, b, 0, 0, 0))],
        out_specs=(pl.BlockSpec((None, r, RET_W), lambda b, i: (b, 0, 0)),
                   pl.BlockSpec((None, H_RET, DH_RET, DH_RET), lambda b, i: (b, 0, 0, 0))),
        compiler_params=_params("parallel", "arbitrary"),
        name="retention_sample",
    )(zs, zs, zs, zs, cos, sin, dmask, xi, zeta, g_c, gn_g.reshape(1, RET_W), state_ret)


def _s5_mats(a_re, a_im, b_re, b_im, c_re, c_im, d_skip, log_dt, c, c_eff):
    lam = lax.complex(a_re, a_im)
    lam_dt = lam * jnp.exp(log_dt)[:, None]
    lam_bar = jnp.exp(lam_dt)
    b_bar = ((lam_bar - 1.0) / lam)[..., None] * lax.complex(b_re, b_im)
    c_mat = lax.complex(c_re, c_im)
    g, p, cg = b_bar.shape

    def power(n):
        return jnp.exp(lam_dt[None] * jnp.asarray(n, F32)[:, None, None])

    lags = jnp.real(jnp.einsum('gop,jgp,gpi->jgoi', c_mat, power(jnp.arange(c)), b_bar, precision=HI))
    sig = jnp.arange(c)[:, None, None]
    tau = jnp.arange(c)[None, :, None]
    lag_of = (tau - sig == jnp.arange(c)[None, None, :]).astype(F32)
    m = jnp.einsum('stj,jgoi->gsito', lag_of, lags, precision=HI)
    m = m + (jnp.eye(c)[None, :, None, :, None] * jnp.eye(cg)[None, None, :, None, :]
             * d_skip.reshape(g, 1, cg, 1, 1))
    m = m.reshape(g, c * cg, c * cg)
    w = power(c_eff - 1.0 - jnp.arange(c, dtype=F32))[:, :, :, None] * b_bar[None]
    w = jnp.where((jnp.arange(c) < c_eff)[:, None, None, None], w, 0.0)
    w = w.transpose(1, 0, 3, 2).reshape(g, c * cg, p)
    v = power(jnp.arange(c, dtype=F32) + 1.0)[:, :, None, :] * c_mat[None]
    v = v.transpose(1, 3, 0, 2).reshape(g, p, c * cg)
    a = power(jnp.asarray([c_eff], F32))[0]
    return (m.astype(F32), jnp.real(w), jnp.imag(w), jnp.real(v), -jnp.imag(v), jnp.real(a), jnp.imag(a))


def _s5_prompt_kernel(u_ref, m_ref, wre_ref, wim_ref, vre_ref, vim_ref, are_ref, aim_ref,
                      y_ref, hre_ref, him_ref, ere_sc, eim_sc, hsre_sc, hsim_sc):
    ns, nc, _ = u_ref.shape
    p = wre_ref.shape[1]
    ere_sc[...] = jnp.concatenate(
        [jnp.dot(u_ref[s], wre_ref[...], precision=HI, preferred_element_type=F32) for s in range(ns)], axis=1)
    eim_sc[...] = jnp.concatenate(
        [jnp.dot(u_ref[s], wim_ref[...], precision=HI, preferred_element_type=F32) for s in range(ns)], axis=1)
    ar = are_ref[...]
    ai = aim_ref[...]

    def body(k, carry):
        hr, hi = carry
        hsre_sc[pl.ds(k, 1), :] = hr
        hsim_sc[pl.ds(k, 1), :] = hi
        er = ere_sc[pl.ds(k, 1), :]
        ei = eim_sc[pl.ds(k, 1), :]
        return ar * hr - ai * hi + er, ar * hi + ai * hr + ei

    zero = jnp.zeros((1, 2 * SSM_P), F32)
    hr, hi = lax.fori_loop(0, nc, body, (zero, zero))
    hre_ref[...] = hr
    him_ref[...] = hi
    hsre = hsre_sc[...]
    hsim = hsim_sc[...]
    for s in range(ns):
        y_ref[s] = (jnp.dot(u_ref[s], m_ref[...], precision=HI, preferred_element_type=F32)
                    + jnp.dot(hsre[:, s * p:(s + 1) * p], vre_ref[...], precision=HI, preferred_element_type=F32)
                    + jnp.dot(hsim[:, s * p:(s + 1) * p], vim_ref[...], precision=HI, preferred_element_type=F32))


def _s5_pack_kernel(su_ref, u_ref):
    ng, kb = u_ref.shape[0], u_ref.shape[1]
    rows = [su_ref[pl.ds(t, kb, stride=SSM_GROUP), :] for t in range(SSM_GROUP)]
    for g in range(ng):
        u_ref[g] = jnp.concatenate([r[:, g * SSM_GROUP:(g + 1) * SSM_GROUP] for r in rows], axis=1)


def _s5_unpack_kernel(y_ref, o_ref):
    ng, kb = y_ref.shape[0], y_ref.shape[1]
    groups = [y_ref[g] for g in range(ng)]
    for t in range(SSM_GROUP):
        o_ref[pl.ds(t, kb, stride=SSM_GROUP), :] = jnp.concatenate(
            [y[:, t * SSM_GROUP:(t + 1) * SSM_GROUP] for y in groups], axis=1)


def _s5_relayout(x, nb, seq, to_groups):
    c = SSM_GROUP
    nc = seq // c
    cw = c * SSM_GROUP
    ng = 128 // SSM_GROUP
    kb = max(d for d in (64, 32, 16, 8) if nc % d == 0)
    tok = pl.BlockSpec((kb * c, 128), lambda s, i, j: (s * (nc // kb) + i, j))
    grp = pl.BlockSpec((ng, None, kb, cw), lambda s, i, j: (j, s, i, 0))
    return pl.pallas_call(
        _s5_pack_kernel if to_groups else _s5_unpack_kernel,
        out_shape=jax.ShapeDtypeStruct((SSM_G, nb, nc, cw) if to_groups else (nb * seq, SSM_W), F32),
        grid=(nb, nc // kb, SSM_G // ng),
        in_specs=[tok if to_groups else grp],
        out_specs=grp if to_groups else tok,
        compiler_params=_params("parallel", "parallel", "parallel"),
        name="s5_pack" if to_groups else "s5_unpack",
    )(x)


def _s5_prompt(su, mats, nb, seq):
    assert nb == 2
    c = SSM_GROUP
    nc = seq // c
    cw = c * SSM_GROUP
    m, w_re, w_im, v_re, v_im, a_re, a_im = mats
    u = _s5_relayout(su, nb, seq, True)

    gspec = lambda shape: pl.BlockSpec((None,) + shape, lambda g: (g,) + (0,) * len(shape))
    y, hre, him = pl.pallas_call(
        _s5_prompt_kernel,
        out_shape=(jax.ShapeDtypeStruct((SSM_G, nb, nc, cw), F32),
                   jax.ShapeDtypeStruct((SSM_G, 1, nb * SSM_P), F32),
                   jax.ShapeDtypeStruct((SSM_G, 1, nb * SSM_P), F32)),
        grid=(SSM_G,),
        in_specs=[gspec((nb, nc, cw)), gspec((cw, cw)),
                  gspec((cw, SSM_P)), gspec((cw, SSM_P)), gspec((SSM_P, cw)), gspec((SSM_P, cw)),
                  gspec((1, nb * SSM_P)), gspec((1, nb * SSM_P))],
        out_specs=(gspec((nb, nc, cw)), gspec((1, nb * SSM_P)), gspec((1, nb * SSM_P))),
        scratch_shapes=[pltpu.VMEM((nc, nb * SSM_P), F32) for _ in range(4)],
        compiler_params=_params("parallel"),
        name="s5_prompt",
    )(u, m, w_re, w_im, v_re, v_im, jnp.tile(a_re, (1, nb))[:, None, :], jnp.tile(a_im, (1, nb))[:, None, :])
    y = _s5_relayout(y, nb, seq, False)
    unpack = lambda h: h.reshape(SSM_G, nb, SSM_P).transpose(1, 0, 2)
    return y, unpack(hre), unpack(him)


def _s5_sample_kernel(u_ref, h0re_ref, h0im_ref, m_ref, wre_ref, wim_ref, vre_ref, vim_ref, are_ref, aim_ref,
                      y_ref, hre_ref, him_ref):
    u = u_ref[...]
    hr = h0re_ref[...]
    hi = h0im_ref[...]
    ar = are_ref[...]
    ai = aim_ref[...]
    hre_ref[...] = ar * hr - ai * hi + jnp.dot(u, wre_ref[...], precision=HI, preferred_element_type=F32)
    him_ref[...] = ar * hi + ai * hr + jnp.dot(u, wim_ref[...], precision=HI, preferred_element_type=F32)
    y_ref[...] = (jnp.dot(u, m_ref[...], precision=HI, preferred_element_type=F32)
                  + jnp.dot(hr, vre_ref[...], precision=HI, preferred_element_type=F32)
                  + jnp.dot(hi, vim_ref[...], precision=HI, preferred_element_type=F32))


def _s5_sample(su, h0_re, h0_im, mats, n_new):
    ns, r, _ = su.shape
    cw = r * SSM_GROUP
    m, w_re, w_im, v_re, v_im, a_re, a_im = mats
    u = su.reshape(ns, r, SSM_G, SSM_GROUP).transpose(2, 0, 1, 3).reshape(SSM_G, ns, cw)
    gspec = lambda shape: pl.BlockSpec((None,) + shape, lambda g: (g,) + (0,) * len(shape))
    y, hre, him = pl.pallas_call(
        _s5_sample_kernel,
        out_shape=(jax.ShapeDtypeStruct((SSM_G, ns, cw), F32),
                   jax.ShapeDtypeStruct((SSM_G, ns, SSM_P), F32),
                   jax.ShapeDtypeStruct((SSM_G, ns, SSM_P), F32)),
        grid=(SSM_G,),
        in_specs=[gspec((ns, cw)), gspec((ns, SSM_P)), gspec((ns, SSM_P)), gspec((cw, cw)),
                  gspec((cw, SSM_P)), gspec((cw, SSM_P)), gspec((SSM_P, cw)), gspec((SSM_P, cw)),
                  gspec((1, SSM_P)), gspec((1, SSM_P))],
        out_specs=(gspec((ns, cw)), gspec((ns, SSM_P)), gspec((ns, SSM_P))),
        compiler_params=_params("parallel"),
        name="s5_sample",
    )(u, h0_re.transpose(1, 0, 2), h0_im.transpose(1, 0, 2), m, w_re, w_im, v_re, v_im,
      a_re[:, None, :], a_im[:, None, :])
    y = y.reshape(SSM_G, ns, r, SSM_GROUP)[:, :, :n_new].transpose(1, 2, 0, 3).reshape(ns * n_new, SSM_W)
    return y, hre.transpose(1, 0, 2), him.transpose(1, 0, 2)


def _glu_kernel(y_ref, w_ref, b_ref, g_ref, o_ref):
    sy = _gelu(y_ref[...])
    gate = _sigmoid(jnp.dot(sy.astype(BF16), w_ref[...], preferred_element_type=F32) + b_ref[...])
    o_ref[...] = _rms_norm(sy * gate, g_ref[...])


def _s5_glu(y, w_glu, b_glu, norm_g, tm):
    t = y.shape[0]
    row = lambda: pl.BlockSpec((1, SSM_W), lambda i: (0, 0))
    return pl.pallas_call(
        _glu_kernel,
        out_shape=jax.ShapeDtypeStruct((t, SSM_W), F32),
        grid=(t // tm,),
        in_specs=[pl.BlockSpec((tm, SSM_W), lambda i: (i, 0)),
                  pl.BlockSpec((SSM_W, SSM_W), lambda i: (0, 0)), row(), row()],
        out_specs=pl.BlockSpec((tm, SSM_W), lambda i: (i, 0)),
        compiler_params=_params("parallel"),
        name="s5_glu",
    )(y, w_glu.astype(BF16), b_glu.reshape(1, SSM_W), norm_g.reshape(1, SSM_W))


def _outproj_kernel(ro_ref, fo_ref, so_ref, x_ref, wr_ref, wf_ref, ws_ref, g_ref, b_ref, o_ref):
    mix = jnp.dot(ro_ref[...].astype(BF16), wr_ref[...], preferred_element_type=F32)
    mix += jnp.dot(fo_ref[...].astype(BF16), wf_ref[...], preferred_element_type=F32)
    mix += jnp.dot(so_ref[...].astype(BF16), ws_ref[...], preferred_element_type=F32)
    o_ref[...] = _layer_norm(DEEPNORM_ALPHA * x_ref[...] + mix, g_ref[...], b_ref[...])


def _outproj_ln(ro, fo, so, x, w_out, ln_g, ln_b, tm):
    t = x.shape[0]
    w = w_out.astype(BF16)
    tile = lambda n: pl.BlockSpec((tm, n), lambda i: (i, 0))
    full = lambda r, n: pl.BlockSpec((r, n), lambda i: (0, 0))
    return pl.pallas_call(
        _outproj_kernel,
        out_shape=jax.ShapeDtypeStruct((t, D_MODEL), F32),
        grid=(t // tm,),
        in_specs=[tile(RET_W), tile(FOX_W), tile(SSM_W), tile(D_MODEL),
                  full(RET_W, D_MODEL), full(FOX_W, D_MODEL), full(SSM_W, D_MODEL),
                  full(1, D_MODEL), full(1, D_MODEL)],
        out_specs=tile(D_MODEL),
        compiler_params=_params("parallel"),
        name="outproj_ln",
    )(ro, fo, so, x, w[:RET_W], w[RET_W:RET_W + FOX_W], w[RET_W + FOX_W:],
      ln_g.reshape(1, D_MODEL), ln_b.reshape(1, D_MODEL))


def _topk_rows(problems, k):
    def body(it, _):
        for s_ref, val_ref, idx_ref in problems:
            n = s_ref.shape[0]
            iota = lax.broadcasted_iota(jnp.int32, s_ref.shape, 0)
            s = s_ref[...]
            m = jnp.max(s, axis=0, keepdims=True)
            idx = jnp.min(jnp.where(s == m, iota, n), axis=0, keepdims=True)
            val_ref[pl.ds(it, 1), :] = m
            idx_ref[pl.ds(it, 1), :] = idx
            s_ref[...] = jnp.where(iota == idx, -jnp.inf, s)
        return 0

    lax.fori_loop(0, k, body, 0)


CAND_COUNTS = [PEER_TOPK // (a + 1) for a in range(PEER_TOPK)]
CAND_STARTS = [sum(CAND_COUNTS[:a]) for a in range(PEER_TOPK)]
N_CAND = sum(CAND_COUNTS)
N_CAND_PAD = -(-N_CAND // 8) * 8


def _peer_topk_kernel(q_ref, sk_ref, i_ref, j_ref, gate_ref, s0_sc, s1_sc, cand_sc, v0_sc, v1_sc, x0_sc, x1_sc,
                      top_sc, sel_sc):
    kk = PEER_TOPK
    s = lax.dot_general(sk_ref[...], q_ref[...], NT_DIMS, precision=HI, preferred_element_type=F32)
    s0_sc[...] = s[:PEER_NKEYS]
    s1_sc[...] = s[PEER_NKEYS:]
    _topk_rows([(s0_sc, v0_sc, x0_sc), (s1_sc, v1_sc, x1_sc)], kk)
    v1 = v1_sc[...]
    for a in range(kk):
        cand_sc[CAND_STARTS[a]:CAND_STARTS[a] + CAND_COUNTS[a], :] = v0_sc[a:a + 1, :] + v1[:CAND_COUNTS[a]]
    if N_CAND_PAD > N_CAND:
        cand_sc[N_CAND:, :] = jnp.full((N_CAND_PAD - N_CAND, cand_sc.shape[1]), -jnp.inf, F32)
    _topk_rows([(cand_sc, top_sc, sel_sc)], kk)
    sel = sel_sc[...]
    a_idx = jnp.zeros_like(sel)
    start = jnp.zeros_like(sel)
    for a in range(1, kk):
        past = sel >= CAND_STARTS[a]
        a_idx = jnp.where(past, a, a_idx)
        start = jnp.where(past, CAND_STARTS[a], start)
    b_idx = sel - start
    ii = jnp.zeros_like(sel)
    jj = jnp.zeros_like(sel)
    for a in range(kk):
        ii = jnp.where(a_idx == a, x0_sc[a:a + 1, :], ii)
        jj = jnp.where(b_idx == a, x1_sc[a:a + 1, :], jj)
    top = top_sc[...]
    e = jnp.exp(top - top[0:1, :])
    i_ref[...] = ii
    j_ref[...] = jj
    gate_ref[...] = e / jnp.sum(e, axis=0, keepdims=True)


def _peer_topk(q, sub_keys, tm):
    t = q.shape[0]
    half = PEER_DK // 2
    sk = jnp.zeros((PEER_HEADS, 2 * PEER_NKEYS, PEER_DK), F32)
    sk = sk.at[:, :PEER_NKEYS, :half].set(sub_keys[:, 0]).at[:, PEER_NKEYS:, half:].set(sub_keys[:, 1])
    kk = PEER_TOPK
    slot = lambda: pl.BlockSpec((kk, tm), lambda i, h: (h, i))
    return pl.pallas_call(
        _peer_topk_kernel,
        out_shape=(jax.ShapeDtypeStruct((PEER_HEADS * kk, t), jnp.int32),
                   jax.ShapeDtypeStruct((PEER_HEADS * kk, t), jnp.int32),
                   jax.ShapeDtypeStruct((PEER_HEADS * kk, t), F32)),
        grid=(t // tm, PEER_HEADS),
        in_specs=[pl.BlockSpec((tm, PEER_DK), lambda i, h: (i, h)),
                  pl.BlockSpec((None, 2 * PEER_NKEYS, PEER_DK), lambda i, h: (h, 0, 0))],
        out_specs=(slot(), slot(), slot()),
        scratch_shapes=[pltpu.VMEM((PEER_NKEYS, tm), F32), pltpu.VMEM((PEER_NKEYS, tm), F32),
                        pltpu.VMEM((N_CAND_PAD, tm), F32),
                        pltpu.VMEM((kk, tm), F32), pltpu.VMEM((kk, tm), F32),
                        pltpu.VMEM((kk, tm), jnp.int32), pltpu.VMEM((kk, tm), jnp.int32),
                        pltpu.VMEM((kk, tm), F32), pltpu.VMEM((kk, tm), jnp.int32)],
        compiler_params=_params("parallel", "arbitrary"),
        name="peer_topk",
    )(q, sk)


G_TOKENS = 128
G_ROW_STRIDE = G_TOKENS + 8
G_UNROLL = 32


def _peer_gates_kernel(i_ref, j_ref, gate_ref, o_ref, g_sc):
    tb, ns = i_ref.shape
    nk = PEER_NKEYS
    io = lax.broadcasted_iota(jnp.int32, (nk, ns), 0)

    def body(tt, carry):
        for u in range(G_UNROLL):
            t = tt * G_UNROLL + u
            irow = i_ref[pl.ds(t, 1), :]
            jrow = j_ref[pl.ds(t, 1), :]
            grow = gate_ref[pl.ds(t, 1), :]
            pt = jnp.where(io == irow, grow, 0.0).astype(BF16)
            qt = jnp.where(io == jrow, 1.0, 0.0).astype(BF16)
            g = lax.dot_general(pt, qt, NT_DIMS, preferred_element_type=F32)
            g_sc[pl.ds(t, nk, stride=G_ROW_STRIDE), :] = g
        return carry

    lax.fori_loop(0, tb // G_UNROLL, body, 0)
    for i in range(nk):
        o_ref[:, i * nk:(i + 1) * nk] = g_sc[i * G_ROW_STRIDE:i * G_ROW_STRIDE + tb, :].astype(BF16)


def _peer_gates(ii, jj, gate):
    t = ii.shape[0]
    tb = G_TOKENS
    ns = PEER_HEADS * PEER_TOPK
    tok = lambda: pl.BlockSpec((tb, ns), lambda i: (i, 0))
    return pl.pallas_call(
        _peer_gates_kernel,
        out_shape=jax.ShapeDtypeStruct((t, PEER_N), BF16),
        grid=(t // tb,),
        in_specs=[tok(), tok(), tok()],
        out_specs=pl.BlockSpec((tb, PEER_N), lambda i: (i, 0)),
        scratch_shapes=[pltpu.VMEM((PEER_NKEYS * G_ROW_STRIDE, PEER_NKEYS), F32)],
        compiler_params=_params("parallel"),
        name="peer_gates",
    )(ii, jj, gate)


def _peer_experts_kernel(x_ref, g_ref, u_ref, v_ref, lg_ref, lb_ref, o_ref, xb_sc, acc_sc):
    e = pl.program_id(1)

    @pl.when(e == 0)
    def _():
        xb_sc[...] = x_ref[...].astype(BF16)
        acc_sc[...] = jnp.zeros_like(acc_sc)

    h = lax.dot_general(xb_sc[...], u_ref[...], NT_DIMS, preferred_element_type=F32)
    w = (g_ref[...].astype(F32) * _gelu(h)).astype(BF16)
    acc_sc[...] += jnp.dot(w, v_ref[...], preferred_element_type=F32)

    @pl.when(e == pl.num_programs(1) - 1)
    def _():
        o_ref[...] = _layer_norm(DEEPNORM_ALPHA * x_ref[...] + acc_sc[...], lg_ref[...], lb_ref[...])


def _peer_experts_ln(x, g, u_tabs, v_tabs, layer, ln_g, ln_b, tm, nb):
    t = x.shape[0]
    return pl.pallas_call(
        _peer_experts_kernel,
        out_shape=jax.ShapeDtypeStruct((t, D_MODEL), F32),
        grid=(t // tm, PEER_N // nb),
        in_specs=[pl.BlockSpec((tm, D_MODEL), lambda i, e: (i, 0)),
                  pl.BlockSpec((tm, nb), lambda i, e: (i, e)),
                  pl.BlockSpec((None, nb, D_MODEL), lambda i, e: (layer, e, 0)),
                  pl.BlockSpec((None, nb, D_MODEL), lambda i, e: (layer, e, 0)),
                  pl.BlockSpec((1, D_MODEL), lambda i, e: (0, 0)),
                  pl.BlockSpec((1, D_MODEL), lambda i, e: (0, 0))],
        out_specs=pl.BlockSpec((tm, D_MODEL), lambda i, e: (i, 0)),
        scratch_shapes=[pltpu.VMEM((tm, D_MODEL), BF16), pltpu.VMEM((tm, D_MODEL), F32)],
        compiler_params=_params("parallel", "arbitrary"),
        name="peer_experts_ln",
    )(x, g, u_tabs, v_tabs, ln_g.reshape(1, D_MODEL), ln_b.reshape(1, D_MODEL))


def _ple_kernel(x_ref, xn_ref, p_ref, wg_ref, bg_ref, we_ref, o_ref):
    gate = _sigmoid(jnp.dot(x_ref[...].astype(BF16), wg_ref[...], preferred_element_type=F32) + bg_ref[...])
    emb = jnp.dot(p_ref[...].astype(BF16), we_ref[...], preferred_element_type=F32)
    o_ref[...] = xn_ref[...] + gate * emb


def _ple(x, p, w_pg, b_pg, w_pe, tm, tn):
    t = x.shape[0]
    return pl.pallas_call(
        _ple_kernel,
        out_shape=jax.ShapeDtypeStruct((t, D_MODEL), F32),
        grid=(t // tm, D_MODEL // tn),
        in_specs=[pl.BlockSpec((tm, D_MODEL), lambda i, j: (i, 0)),
                  pl.BlockSpec((tm, tn), lambda i, j: (i, j)),
                  pl.BlockSpec((tm, PLE_DIM), lambda i, j: (i, 0)),
                  pl.BlockSpec((D_MODEL, tn), lambda i, j: (0, j)),
                  pl.BlockSpec((1, tn), lambda i, j: (0, j)),
                  pl.BlockSpec((PLE_DIM, tn), lambda i, j: (0, j))],
        out_specs=pl.BlockSpec((tm, tn), lambda i, j: (i, j)),
        compiler_params=_params("parallel", "arbitrary"),
        name="ple_gate",
    )(x, x, p, w_pg.astype(BF16), b_pg.reshape(1, D_MODEL), w_pe.astype(BF16))


def _split_bf16(w):
    hi = w.astype(BF16)
    return hi, (w - hi.astype(F32)).astype(BF16)


def _layer(x, p, lw, caches, stacked, page_table, layer, nb, seq, ns, n_new, tiles):
    t_all = x.shape[0]
    tp = nb * seq
    r = 8
    cache_k, cache_v, lf_sums, state_ret, state_re, state_im = caches
    past_len = page_table.shape[1] * (cache_k.shape[2] // H_FOX)

    u_tabs, v_tabs, w_in_bf16, w_in = stacked
    z = _in_proj(x, w_in_bf16, layer, tiles['tm_in'], 512)
    lf = _forget_gates(x, w_in[layer, :, Z_FF:Z_SU], lw['b_fox_f'], tiles['tm'])
    zs = jnp.pad(z[tp:].reshape(ns, n_new, Z_W), ((0, 0), (0, r - n_new), (0, 0)))

    ro, ret_p = _retention_prompt(z, lw['ret_gn_g'], nb, seq, t_all)
    ro_s, ret_s = _retention_sample(zs, state_ret, layer, lw['ret_gn_g'], past_len, n_new)
    ro = ro.at[tp:].set(ro_s[:, :n_new].reshape(ns * n_new, RET_W))

    fo = _fox_prompt(z, _forget_cumsum(lf, nb, seq), lw['fox_norm_g'], nb, seq, t_all, tiles['attn'])
    lf_s = lf[:, tp:].T.reshape(ns, n_new, H_FOX)
    lfn = jnp.pad(lf_s, ((0, 0), (0, r - n_new), (0, 0))).reshape(ns, 1, r * H_FOX)
    fo_s = _fox_sample(z[tp:, Z_FQ:Z_FQ + FOX_W].reshape(ns, n_new * H_FOX, DH_FOX),
                       zs[:, :, Z_FK:Z_FK + FOX_W].reshape(ns, r * H_FOX, DH_FOX),
                       zs[:, :, Z_FV:Z_FV + FOX_W].reshape(ns, r * H_FOX, DH_FOX),
                       lfn, cache_k, cache_v, lf_sums, page_table, layer, lw['fox_norm_g'])
    fo = fo.at[tp:].set(fo_s.reshape(ns * n_new, FOX_W))

    ssm = (lw['ssm_a_re'], lw['ssm_a_im'], lw['ssm_b_re'], lw['ssm_b_im'], lw['ssm_c_re'], lw['ssm_c_im'],
           lw['ssm_d'], lw['ssm_log_dt'])
    y_p, hre_p, him_p = _s5_prompt(z[:tp, Z_SU:], _s5_mats(*ssm, SSM_GROUP, SSM_GROUP), nb, seq)
    y_s, hre_s, him_s = _s5_sample(zs[:, :, Z_SU:], state_re[layer], state_im[layer], _s5_mats(*ssm, r, n_new), n_new)
    so = _s5_glu(jnp.concatenate([y_p, y_s], 0), lw['w_glu'], lw['b_glu'], lw['ssm_norm_g'], tiles['tm'])

    x1 = _outproj_ln(ro, fo, so, x, lw['w_out'], lw['ln1_g'], lw['ln1_b'], tiles['tm_ln'])

    q = _matmul(x1, _split_bf16(lw['peer_w_q']), tiles['tm'], 512, "peer_query")
    ii, jj, gate = _peer_topk(q, lw['peer_sub_keys'], tiles['tm'])
    g = _peer_gates(ii.T, jj.T, gate.T)
    x2 = _peer_experts_ln(x1, g, u_tabs, v_tabs, layer, lw['ln2_g'], lw['ln2_b'], tiles['tm'], 512)

    x3 = _ple(x2, p, lw['w_pg'], lw['b_pg'], lw['w_pe'], tiles['tm'], 512)

    st_p = (z[:tp, Z_FK:Z_FK + FOX_W].reshape(nb, seq, H_FOX, DH_FOX),
            z[:tp, Z_FV:Z_FV + FOX_W].reshape(nb, seq, H_FOX, DH_FOX),
            lf[:, :tp].T.reshape(nb, seq, H_FOX), ret_p, hre_p, him_p)
    st_s = (z[tp:, Z_FK:Z_FK + FOX_W].reshape(ns, n_new, H_FOX, DH_FOX),
            z[tp:, Z_FV:Z_FV + FOX_W].reshape(ns, n_new, H_FOX, DH_FOX),
            lf_s, ret_s, hre_s, him_s)
    return x3, st_p, st_s


def kernel(x_prompt, x_sample, p_prompt, p_sample, cache_fox_k, cache_fox_v, cache_fox_lf, state_ret, state_ssm_re, state_ssm_im, page_table, w_in, w_out, ret_gn_g, b_fox_f, fox_norm_g, ssm_a_re, ssm_a_im, ssm_b_re, ssm_b_im, ssm_c_re, ssm_c_im, ssm_d, ssm_log_dt, w_glu, b_glu, ssm_norm_g, ln1_g, ln1_b, ln2_g, ln2_b, peer_w_q, peer_sub_keys, peer_u, peer_v, w_pe, w_pg, b_pg):
    nb, seq, d = x_prompt.shape
    ns, n_new, _ = x_sample.shape
    tp, ts = nb * seq, ns * n_new
    t_all = tp + ts
    tm = 640 if t_all % 640 == 0 else 128
    tiles = dict(tm=tm, tm_ln=tm // 2 if tm == 640 else tm, tm_in=832 if t_all % 832 == 0 else tm,
                 attn=256 if seq % 256 == 0 else 128)
    x = jnp.concatenate([x_prompt.reshape(tp, d), x_sample.reshape(ts, d)], 0)
    depth, n_pool, pg, nh, dh = cache_fox_k.shape
    caches = (cache_fox_k.reshape(depth, n_pool, pg * nh, dh), cache_fox_v.reshape(depth, n_pool, pg * nh, dh),
              _lf_prep(cache_fox_lf), state_ret, state_ssm_re, state_ssm_im)
    weights = dict(w_out=w_out, ret_gn_g=ret_gn_g, b_fox_f=b_fox_f, fox_norm_g=fox_norm_g,
                   ssm_a_re=ssm_a_re, ssm_a_im=ssm_a_im, ssm_b_re=ssm_b_re, ssm_b_im=ssm_b_im,
                   ssm_c_re=ssm_c_re, ssm_c_im=ssm_c_im, ssm_d=ssm_d, ssm_log_dt=ssm_log_dt,
                   w_glu=w_glu, b_glu=b_glu, ssm_norm_g=ssm_norm_g, ln1_g=ln1_g, ln1_b=ln1_b,
                   ln2_g=ln2_g, ln2_b=ln2_b, peer_w_q=peer_w_q, peer_sub_keys=peer_sub_keys,
                   w_pe=w_pe, w_pg=w_pg, b_pg=b_pg)
    stacked = (peer_u.astype(BF16), peer_v.astype(BF16), w_in.astype(BF16), w_in)
    sts_p, sts_s = [], []
    for l in range(DEPTH):
        lw = {k: v[l] for k, v in weights.items()}
        p = jnp.concatenate([p_prompt[l].reshape(tp, PLE_DIM), p_sample[l].reshape(ts, PLE_DIM)], 0)
        x, st_p, st_s = _layer(x, p, lw, caches, stacked, page_table, l, nb, seq, ns, n_new, tiles)
        sts_p.append(st_p)
        sts_s.append(st_s)
    stack = lambda sts, i: jnp.stack([s[i] for s in sts])
    return ((x[:tp].reshape(nb, seq, d), x[tp:].reshape(ns, n_new, d))
            + tuple(stack(sts_p, i) for i in range(6)) + tuple(stack(sts_s, i) for i in range(6)))
```

```python
import functools
import math

import jax
import jax.numpy as jnp
from jax import lax
from jax.experimental import pallas as pl
from jax.experimental.pallas import tpu as pltpu

F32 = jnp.float32
BF16 = jnp.bfloat16
HI = lax.Precision.HIGHEST

D_MODEL = 2048
DEPTH = 2
H_RET, DH_RET = 4, 128
RET_W = H_RET * DH_RET
H_FOX, DH_FOX = 8, 128
FOX_W = H_FOX * DH_FOX
SSM_W, SSM_GROUP = 512, 16
SSM_G = SSM_W // SSM_GROUP
SSM_P = 64
PEER_HEADS, PEER_NKEYS, PEER_DK, PEER_TOPK = 8, 128, 128, 16
PEER_N = PEER_NKEYS * PEER_NKEYS
PLE_DIM = 256
BLOCK = 128
ROPE_BASE = 10000.0
DEEPNORM_ALPHA = (2 * DEPTH) ** 0.25
EPS = 1e-5

Z_FQ, Z_FK, Z_FV, Z_FF = 4 * RET_W, 4 * RET_W + FOX_W, 4 * RET_W + 2 * FOX_W, 4 * RET_W + 3 * FOX_W
Z_SU = Z_FF + H_FOX
Z_W = Z_SU + SSM_W

VMEM_LIMIT = 56 * 1024 * 1024
NT_DIMS = (((1,), (1,)), ((), ()))
TN_DIMS = (((0,), (0,)), ((), ()))


def _params(*sem):
    return pltpu.CompilerParams(dimension_semantics=sem, vmem_limit_bytes=VMEM_LIMIT)


def _gelu(x):
    return 0.5 * x * (1.0 + jnp.tanh(math.sqrt(2.0 / math.pi) * (x + 0.044715 * (x * x * x))))


def _sigmoid(x):
    return 1.0 / (1.0 + jnp.exp(-x))


def _log_sigmoid(x):
    return jnp.minimum(x, 0.0) - jnp.log1p(jnp.exp(-jnp.abs(x)))


def _layer_norm(y, g, b):
    mu = jnp.mean(y, -1, keepdims=True)
    var = jnp.mean(jnp.square(y - mu), -1, keepdims=True)
    return (y - mu) * lax.rsqrt(var + EPS) * g + b


def _rms_norm(y, g):
    return y * lax.rsqrt(jnp.mean(y * y, -1, keepdims=True) + EPS) * g


def _mm_kernel(x_ref, w_ref, o_ref):
    o_ref[...] = jnp.dot(x_ref[...].astype(BF16), w_ref[...], preferred_element_type=F32)


def _mm3_kernel(x_ref, wh_ref, wl_ref, o_ref):
    x = x_ref[...]
    xh = x.astype(BF16)
    xl = (x - xh.astype(F32)).astype(BF16)
    wh = wh_ref[...]
    acc = jnp.dot(xh, wh, preferred_element_type=F32)
    acc += jnp.dot(xl, wh, preferred_element_type=F32)
    acc += jnp.dot(xh, wl_ref[...], preferred_element_type=F32)
    o_ref[...] = acc


def _matmul(x, ws, tm, tn, name):
    t, k = x.shape
    n = ws[0].shape[1]
    kern = _mm_kernel if len(ws) == 1 else _mm3_kernel
    return pl.pallas_call(
        kern,
        out_shape=jax.ShapeDtypeStruct((t, n), F32),
        grid=(t // tm, n // tn),
        in_specs=[pl.BlockSpec((tm, k), lambda i, j: (i, 0))]
        + [pl.BlockSpec((k, tn), lambda i, j: (0, j)) for _ in ws],
        out_specs=pl.BlockSpec((tm, tn), lambda i, j: (i, j)),
        compiler_params=_params("parallel", "arbitrary"),
        name=name,
    )(x, *ws)


def _mm_w32_kernel(x_ref, w_ref, o_ref, xb_sc):
    @pl.when(pl.program_id(1) == 0)
    def _():
        xb_sc[...] = x_ref[...].astype(BF16)

    o_ref[...] = jnp.dot(xb_sc[...], w_ref[...], preferred_element_type=F32)


def _in_proj(x, w_all, layer, tm, tn):
    t, k = x.shape
    n = w_all.shape[2]
    return pl.pallas_call(
        _mm_w32_kernel,
        out_shape=jax.ShapeDtypeStruct((t, n), F32),
        grid=(t // tm, pl.cdiv(n, tn)),
        in_specs=[pl.BlockSpec((tm, k), lambda i, j: (i, 0)),
                  pl.BlockSpec((None, k, tn), lambda i, j: (layer, 0, j))],
        out_specs=pl.BlockSpec((tm, tn), lambda i, j: (i, j)),
        scratch_shapes=[pltpu.VMEM((tm, k), BF16)],
        compiler_params=_params("parallel", "arbitrary"),
        name="in_proj",
    )(x, w_all)


def _gate_kernel(x_ref, w_ref, b_ref, lf_ref):
    f = lax.dot_general(w_ref[...], x_ref[...], NT_DIMS, precision=HI, preferred_element_type=F32)
    lf_ref[...] = _log_sigmoid(f + b_ref[...])


def _forget_gates(x, w_ff, b_ff, tm):
    t, k = x.shape
    return pl.pallas_call(
        _gate_kernel,
        out_shape=jax.ShapeDtypeStruct((H_FOX, t), F32),
        grid=(t // tm,),
        in_specs=[
            pl.BlockSpec((tm, k), lambda i: (i, 0)),
            pl.BlockSpec((H_FOX, k), lambda i: (0, 0)),
            pl.BlockSpec((H_FOX, 1), lambda i: (0, 0)),
        ],
        out_specs=pl.BlockSpec((H_FOX, tm), lambda i: (0, i)),
        compiler_params=_params("parallel"),
        name="forget_gates",
    )(x, w_ff.T, b_ff.reshape(H_FOX, 1))


def _cumsum_kernel(lf_ref, c_ref, carry):
    @pl.when(pl.program_id(1) == 0)
    def _():
        carry[...] = jnp.zeros_like(carry)

    n = lf_ref.shape[1]
    r = lax.broadcasted_iota(jnp.int32, (n, n), 0)
    c = lax.broadcasted_iota(jnp.int32, (n, n), 1)
    upper = (r <= c).astype(F32)
    cs = jnp.dot(lf_ref[...], upper, precision=HI, preferred_element_type=F32) + carry[...]
    c_ref[...] = cs
    carry[...] = cs[:, n - 1:n]


def _forget_cumsum(lf, nb, seq):
    n = 256 if seq % 256 == 0 else 128
    nc = seq // n
    return pl.pallas_call(
        _cumsum_kernel,
        out_shape=jax.ShapeDtypeStruct((H_FOX, nb * seq), F32),
        grid=(nb, nc),
        in_specs=[pl.BlockSpec((H_FOX, n), lambda b, i: (0, b * nc + i))],
        out_specs=pl.BlockSpec((H_FOX, n), lambda b, i: (0, b * nc + i)),
        scratch_shapes=[pltpu.VMEM((H_FOX, 1), F32)],
        compiler_params=_params("parallel", "arbitrary"),
        name="forget_cumsum",
    )(lf)


def _fox_prompt_kernel(q_ref, k_ref, v_ref, c_ref, g_ref, o_ref, qs_sc, m_sc, l_sc, acc_sc, *, scale):
    qi = pl.program_id(1)
    ki = pl.program_id(2)
    bq = q_ref.shape[0]
    bk = k_ref.shape[0]
    log2e = math.log2(math.e)

    @pl.when(ki == 0)
    def _():
        qs_sc[...] = (q_ref[...] * (scale * log2e)).astype(BF16)
        m_sc[...] = jnp.full_like(m_sc, -jnp.inf)
        l_sc[...] = jnp.zeros_like(l_sc)
        acc_sc[...] = jnp.zeros_like(acc_sc)

    last = (qi * bq + bq - 1) // bk

    def attend(diagonal):
        if diagonal:
            causal = (ki * bk + lax.broadcasted_iota(jnp.int32, (bq, bk), 1)
                      <= qi * bq + lax.broadcasted_iota(jnp.int32, (bq, bk), 0))
        c2 = c_ref[...] * log2e
        for h in range(H_FOX):
            sl = slice(h * DH_FOX, (h + 1) * DH_FOX)
            k = k_ref[:, sl].astype(BF16)
            s = lax.dot_general(qs_sc[:, sl], k, NT_DIMS, preferred_element_type=F32) - c2[h:h + 1, :]
            if diagonal:
                s = jnp.where(causal, s, -jnp.inf)
            m_prev = m_sc[h]
            m_new = jnp.maximum(m_prev, jnp.max(s, axis=1, keepdims=True))
            alpha = jnp.exp2(m_prev - m_new)
            p = jnp.exp2(s - jnp.tile(m_new, (1, bk // 128)))
            l_sc[h] = alpha * l_sc[h] + jnp.sum(p, axis=1, keepdims=True)
            acc_sc[:, sl] = alpha * acc_sc[:, sl] + jnp.dot(
                p.astype(BF16), v_ref[:, sl].astype(BF16), preferred_element_type=F32)
            m_sc[h] = m_new

    @pl.when(ki < last)
    def _():
        attend(False)

    @pl.when(ki == last)
    def _():
        attend(True)
        for h in range(H_FOX):
            sl = slice(h * DH_FOX, (h + 1) * DH_FOX)
            acc_sc[:, sl] = acc_sc[:, sl] / l_sc[h]
        o_ref[...] = _rms_norm(acc_sc[...], g_ref[...])


def _fox_prompt(z, c, norm_g, nb, seq, t_all, blk):
    bk = blk
    nq, nk = seq // blk, seq // bk
    cq, ck, cv = Z_FQ // FOX_W, Z_FK // FOX_W, Z_FV // FOX_W
    kern = functools.partial(_fox_prompt_kernel, scale=DH_FOX ** -0.5)

    def key_block(b, i, j):
        return b * nk + jnp.minimum(j, (i * blk + blk - 1) // bk)

    return pl.pallas_call(
        kern,
        out_shape=jax.ShapeDtypeStruct((t_all, FOX_W), F32),
        grid=(nb, nq, nk),
        in_specs=[
            pl.BlockSpec((blk, FOX_W), lambda b, i, j: (b * nq + i, cq)),
            pl.BlockSpec((bk, FOX_W), lambda b, i, j: (key_block(b, i, j), ck)),
            pl.BlockSpec((bk, FOX_W), lambda b, i, j: (key_block(b, i, j), cv)),
            pl.BlockSpec((H_FOX, bk), lambda b, i, j: (0, key_block(b, i, j))),
            pl.BlockSpec((1, FOX_W), lambda b, i, j: (0, 0)),
        ],
        out_specs=pl.BlockSpec((blk, FOX_W), lambda b, i, j: (b * nq + i, 0)),
        scratch_shapes=[
            pltpu.VMEM((blk, FOX_W), BF16),
            pltpu.VMEM((H_FOX, blk, 128), F32),
            pltpu.VMEM((H_FOX, blk, 128), F32),
            pltpu.VMEM((blk, FOX_W), F32),
        ],
        compiler_params=_params("parallel", "parallel", "arbitrary"),
        name="fox_prompt",
    )(z, z, z, c, norm_g.reshape(1, FOX_W))


def _lf_prep_kernel(lf_ref, w_ref, o_ref):
    x = lf_ref[...]
    p1 = x.astype(BF16)
    r1 = x - p1.astype(F32)
    p2 = r1.astype(BF16)
    p3 = (r1 - p2.astype(F32)).astype(BF16)
    w = w_ref[...]
    acc = jnp.dot(p3, w, preferred_element_type=F32)
    acc += jnp.dot(p2, w, preferred_element_type=F32)
    acc += jnp.dot(p1, w, preferred_element_type=F32)
    o_ref[...] = acc


def _lf_prep(cache_lf):
    depth, n_pool, pg, nh = cache_lf.shape
    c = pg * nh
    rows = depth * n_pool
    tm = max(d for d in (256, 128, 64, 32, 16, 8) if rows % d == 0)
    src = jnp.arange(c)[:, None]
    dst = jnp.arange(c)[None, :]
    same_head = (src % nh) == (dst % nh)
    w = jnp.concatenate([same_head & (src // nh > dst // nh), same_head], axis=1).astype(BF16)
    return pl.pallas_call(
        _lf_prep_kernel,
        out_shape=jax.ShapeDtypeStruct((rows, 2 * c), F32),
        grid=(rows // tm,),
        in_specs=[pl.BlockSpec((tm, c), lambda i: (i, 0)), pl.BlockSpec((c, 2 * c), lambda i: (0, 0))],
        out_specs=pl.BlockSpec((tm, 2 * c), lambda i: (i, 0)),
        compiler_params=_params("parallel"),
        name="lf_prep",
    )(cache_lf.reshape(rows, c), w)


def _fox_sample_kernel(pt_ref, q_ref, kn_ref, vn_ref, lfn_ref, g_ref, *rest, scale, npg, page_of):
    k_refs, v_refs, x_refs = rest[:npg], rest[npg:2 * npg], rest[2 * npg:3 * npg]
    o_ref, m_sc, l_sc, acc_sc, carry_sc = rest[3 * npg:]
    step = pl.program_id(1)
    nr = q_ref.shape[0]
    ncol = k_refs[0].shape[0]
    head_mask = H_FOX - 1
    head_bits = H_FOX.bit_length() - 1

    @pl.when(step == 0)
    def _():
        m_sc[...] = jnp.full_like(m_sc, -jnp.inf)
        l_sc[...] = jnp.zeros_like(l_sc)
        acc_sc[...] = jnp.zeros_like(acc_sc)
        carry_sc[...] = jnp.zeros_like(carry_sc)

    def equal_heads(ncol):
        rows = lax.broadcasted_iota(jnp.int32, (nr, ncol), 0)
        cols = lax.broadcasted_iota(jnp.int32, (nr, ncol), 1)
        return rows, cols, (rows & head_mask) == (cols & head_mask)

    q = q_ref[...].astype(BF16)
    _, _, same_head = equal_heads(ncol)
    carry = carry_sc[...]
    m_prev = m_sc[...]
    m_new = m_prev
    scores = []
    for i in range(npg):
        x = x_refs[i][pl.ds(page_of(pl.program_id(0), step, i, pt_ref) % 8, 1), :]
        bias = x[:, :ncol] + carry
        carry = carry + x[:, ncol:]
        s = lax.dot_general(q, k_refs[i][...].astype(BF16), NT_DIMS, preferred_element_type=F32) * scale + bias
        s = jnp.where(same_head, s, -jnp.inf)
        m_new = jnp.maximum(m_new, jnp.max(s, axis=1, keepdims=True))
        scores.append(s)
    carry_sc[...] = carry
    alpha = jnp.exp(m_prev - m_new)
    l_new = alpha * l_sc[...]
    acc = alpha * acc_sc[...]
    for i in range(npg):
        p = jnp.exp(scores[i] - m_new)
        l_new += jnp.sum(p, axis=1, keepdims=True)
        acc += jnp.dot(p.astype(BF16), v_refs[i][...].astype(BF16), preferred_element_type=F32)
    m_sc[...] = m_new
    l_sc[...] = l_new
    acc_sc[...] = acc

    @pl.when(step == pl.num_programs(1) - 1)
    def _():
        nc = kn_ref.shape[0]
        rows, cols, same = equal_heads(nc)
        causal = (cols >> head_bits) <= (rows >> head_bits)
        r2 = lax.broadcasted_iota(jnp.int32, (nc, nc), 0)
        c2 = lax.broadcasted_iota(jnp.int32, (nc, nc), 1)
        prefix = jnp.where((r2 & head_mask) == (c2 & head_mask), jnp.where(r2 <= c2, 1.0, 0.0), 0.0)
        cn = jnp.dot(jnp.broadcast_to(lfn_ref[...], (8, nc)), prefix, precision=HI,
                     preferred_element_type=F32)[0:1]
        s = lax.dot_general(q, kn_ref[...].astype(BF16), NT_DIMS, preferred_element_type=F32) * scale - cn
        s = jnp.where(same, jnp.where(causal, s, -jnp.inf), -jnp.inf)
        m_fin = jnp.maximum(m_new, jnp.max(s, axis=1, keepdims=True))
        a_fin = jnp.exp(m_new - m_fin)
        p = jnp.exp(s - m_fin)
        l_fin = a_fin * l_new + jnp.sum(p, axis=1, keepdims=True)
        out = (a_fin * acc + jnp.dot(p.astype(BF16), vn_ref[...].astype(BF16),
                                     preferred_element_type=F32)) / l_fin
        r3 = lax.broadcasted_iota(jnp.int32, (nr, nr), 0)
        c3 = lax.broadcasted_iota(jnp.int32, (nr, nr), 1)
        same_token = jnp.where((r3 >> head_bits) == (c3 >> head_bits), 1.0, 0.0)
        ssq = jnp.sum(jnp.dot(same_token, out * out, precision=HI, preferred_element_type=F32),
                      axis=1, keepdims=True)
        o_ref[...] = out * lax.rsqrt(ssq / (H_FOX * DH_FOX) + EPS) * g_ref[...]


def _fox_sample(q, kn, vn, lfn, cache_k, cache_v, lf_sums, page_table, layer, norm_g):
    ns, nr, _ = q.shape
    nc = kn.shape[1]
    n_pages = page_table.shape[1]
    n_pool, cols = cache_k.shape[1], cache_k.shape[2]
    assert n_pool % 8 == 0 and lf_sums.shape == (cache_k.shape[0] * n_pool, 2 * cols)
    npg = max(d for d in range(1, 9) if n_pages % d == 0)
    g = jnp.tile(norm_g.reshape(H_FOX, DH_FOX), (nr // H_FOX, 1))

    def page_of(b, s, i, pt):
        return pt[b * n_pages + (n_pages - 1 - (s * npg + i))]

    kern = functools.partial(_fox_sample_kernel, scale=DH_FOX ** -0.5, npg=npg, page_of=page_of)

    def paged(i):
        return pl.BlockSpec((None, None, cols, DH_FOX), lambda b, s, pt: (layer, page_of(b, s, i, pt), 0, 0))

    def paged_sums(i):
        return pl.BlockSpec((8, 2 * cols), lambda b, s, pt: ((layer * n_pool + page_of(b, s, i, pt)) // 8, 0))

    grid_spec = pltpu.PrefetchScalarGridSpec(
        num_scalar_prefetch=1,
        grid=(ns, n_pages // npg),
        in_specs=[
            pl.BlockSpec((None, nr, DH_FOX), lambda b, s, pt: (b, 0, 0)),
            pl.BlockSpec((None, nc, DH_FOX), lambda b, s, pt: (b, 0, 0)),
            pl.BlockSpec((None, nc, DH_FOX), lambda b, s, pt: (b, 0, 0)),
            pl.BlockSpec((None, 1, nc), lambda b, s, pt: (b, 0, 0)),
            pl.BlockSpec((nr, DH_FOX), lambda b, s, pt: (0, 0)),
        ] + [paged(i) for i in range(npg)] * 2 + [paged_sums(i) for i in range(npg)],
        out_specs=pl.BlockSpec((None, nr, DH_FOX), lambda b, s, pt: (b, 0, 0)),
        scratch_shapes=[
            pltpu.VMEM((nr, 1), F32),
            pltpu.VMEM((nr, 1), F32),
            pltpu.VMEM((nr, DH_FOX), F32),
            pltpu.VMEM((1, cols), F32),
        ],
    )
    return pl.pallas_call(
        kern,
        out_shape=jax.ShapeDtypeStruct((ns, nr, DH_FOX), F32),
        grid_spec=grid_spec,
        compiler_params=_params("parallel", "arbitrary"),
        name="fox_sample",
    )(page_table.reshape(-1), q, kn, vn, lfn, g,
      *([cache_k] * npg), *([cache_v] * npg), *([lf_sums] * npg))


def _ret_kernel(*refs, has_s0):
    if has_s0:
        (q_ref, k_ref, v_ref, gt_ref, cos_ref, sin_ref, dm_ref, xi_ref, zt_ref, gc_ref, gn_ref, s0_ref,
         o_ref, s_ref) = refs
    else:
        (q_ref, k_ref, v_ref, gt_ref, cos_ref, sin_ref, dm_ref, xi_ref, zt_ref, gc_ref, gn_ref,
         o_ref, s_ref) = refs

    @pl.when(pl.program_id(1) == 0)
    def _():
        if has_s0:
            s_ref[...] = s0_ref[...]
        else:
            s_ref[...] = jnp.zeros_like(s_ref)

    cos = cos_ref[...]
    sin = sin_ref[...]

    def rot(x):
        return x * cos + pltpu.roll(x, DH_RET // 2, 1) * sin

    for h in range(H_RET):
        sl = slice(h * DH_RET, (h + 1) * DH_RET)
        q = rot(q_ref[:, sl])
        k = rot(k_ref[:, sl]) * DH_RET ** -0.5
        v = v_ref[:, sl]
        s = s_ref[h]
        att = lax.dot_general(q, k, NT_DIMS, preferred_element_type=F32) * dm_ref[h]
        o = jnp.dot(att, v, preferred_element_type=F32) + jnp.dot(q * xi_ref[h], s, preferred_element_type=F32)
        s_ref[h] = gc_ref[h] * s + lax.dot_general(k * zt_ref[h], v, TN_DIMS, preferred_element_type=F32)
        mu = jnp.mean(o, -1, keepdims=True)
        var = jnp.mean(jnp.square(o - mu), -1, keepdims=True)
        on = (o - mu) * lax.rsqrt(var + EPS)
        gt = gt_ref[:, sl]
        o_ref[:, sl] = on * gn_ref[:, sl] * (gt * _sigmoid(gt))


def _ret_consts(c, c_eff):
    lg = jnp.log1p(-jnp.exp2(-5.0 - jnp.arange(H_RET, dtype=F32)))
    idx = jnp.arange(c, dtype=F32)
    rel = idx[:, None] - idx[None, :]
    dmask = jnp.where(rel >= 0, jnp.exp(lg[:, None, None] * jnp.maximum(rel, 0.0)), 0.0)
    xi = jnp.exp(lg[:, None] * (idx + 1.0))[..., None]
    zeta = jnp.exp(lg[:, None] * (c_eff - 1.0 - idx))[..., None]
    g_c = jnp.broadcast_to(jnp.exp(lg * c_eff)[:, None, None], (H_RET, 1, DH_RET))
    return dmask, xi, zeta, g_c


def _rope_tables(pos):
    inv = ROPE_BASE ** (-jnp.arange(0, DH_RET, 2, dtype=F32) / DH_RET)
    ang = pos.astype(F32)[:, None] * inv[None, :]
    cos, sin = jnp.cos(ang), jnp.sin(ang)
    return jnp.concatenate([cos, cos], -1), jnp.concatenate([-sin, sin], -1)


def _retention_prompt(z, gn_g, nb, seq, t_all):
    c = BLOCK
    nc = seq // c
    dmask, xi, zeta, g_c = _ret_consts(c, c)
    cos, sin = _rope_tables(jnp.arange(seq))

    def zspec(col):
        return pl.BlockSpec((c, RET_W), lambda b, i: (b * nc + i, col))

    full = lambda shape: pl.BlockSpec(shape, lambda b, i: (0,) * len(shape))
    return pl.pallas_call(
        functools.partial(_ret_kernel, has_s0=False),
        out_shape=(jax.ShapeDtypeStruct((t_all, RET_W), F32),
                   jax.ShapeDtypeStruct((nb, H_RET, DH_RET, DH_RET), F32)),
        grid=(nb, nc),
        in_specs=[zspec(0), zspec(1), zspec(2), zspec(3),
                  pl.BlockSpec((c, DH_RET), lambda b, i: (i, 0)),
                  pl.BlockSpec((c, DH_RET), lambda b, i: (i, 0)),
                  full((H_RET, c, c)), full((H_RET, c, 1)), full((H_RET, c, 1)), full((H_RET, 1, DH_RET)),
                  full((1, RET_W))],
        out_specs=(pl.BlockSpec((c, RET_W), lambda b, i: (b * nc + i, 0)),
                   pl.BlockSpec((None, H_RET, DH_RET, DH_RET), lambda b, i: (b, 0, 0, 0))),
        compiler_params=_params("parallel", "arbitrary"),
        name="retention_prompt",
    )(z, z, z, z, cos, sin, dmask, xi, zeta, g_c, gn_g.reshape(1, RET_W))


def _retention_sample(zs, state_ret, layer, gn_g, pos0, n_new):
    ns, r, _ = zs.shape
    dmask, xi, zeta, g_c = _ret_consts(r, n_new)
    cos, sin = _rope_tables(pos0 + jnp.arange(r))

    def zspec(col):
        return pl.BlockSpec((None, r, RET_W), lambda b, i: (b, 0, col))

    full = lambda shape: pl.BlockSpec(shape, lambda b, i: (0,) * len(shape))
    return pl.pallas_call(
        functools.partial(_ret_kernel, has_s0=True),
        out_shape=(jax.ShapeDtypeStruct((ns, r, RET_W), F32),
                   jax.ShapeDtypeStruct((ns, H_RET, DH_RET, DH_RET), F32)),
        grid=(ns, 1),
        in_specs=[zspec(0), zspec(1), zspec(2), zspec(3),
                  full((r, DH_RET)), full((r, DH_RET)),
                  full((H_RET, r, r)), full((H_RET, r, 1)), full((H_RET, r, 1)), full((H_RET, 1, DH_RET)),
                  full((1, RET_W)),
                  pl.BlockSpec((None, None, H_RET, DH_RET, DH_RET), lambda b, i: (layer, b, 0, 0, 0))],
        out_specs=(pl.BlockSpec((None, r, RET_W), lambda b, i: (b, 0, 0)),
                   pl.BlockSpec((None, H_RET, DH_RET, DH_RET), lambda b, i: (b, 0, 0, 0))),
        compiler_params=_params("parallel", "arbitrary"),
        name="retention_sample",
    )(zs, zs, zs, zs, cos, sin, dmask, xi, zeta, g_c, gn_g.reshape(1, RET_W), state_ret)


def _s5_mats(a_re, a_im, b_re, b_im, c_re, c_im, d_skip, log_dt, c, c_eff):
    lam = lax.complex(a_re, a_im)
    lam_dt = lam * jnp.exp(log_dt)[:, None]
    lam_bar = jnp.exp(lam_dt)
    b_bar = ((lam_bar - 1.0) / lam)[..., None] * lax.complex(b_re, b_im)
    c_mat = lax.complex(c_re, c_im)
    g, p, cg = b_bar.shape

    def power(n):
        return jnp.exp(lam_dt[None] * jnp.asarray(n, F32)[:, None, None])

    lags = jnp.real(jnp.einsum('gop,jgp,gpi->jgoi', c_mat, power(jnp.arange(c)), b_bar, precision=HI))
    sig = jnp.arange(c)[:, None, None]
    tau = jnp.arange(c)[None, :, None]
    lag_of = (tau - sig == jnp.arange(c)[None, None, :]).astype(F32)
    m = jnp.einsum('stj,jgoi->gsito', lag_of, lags, precision=HI)
    m = m + (jnp.eye(c)[None, :, None, :, None] * jnp.eye(cg)[None, None, :, None, :]
             * d_skip.reshape(g, 1, cg, 1, 1))
    m = m.reshape(g, c * cg, c * cg)
    w = power(c_eff - 1.0 - jnp.arange(c, dtype=F32))[:, :, :, None] * b_bar[None]
    w = jnp.where((jnp.arange(c) < c_eff)[:, None, None, None], w, 0.0)
    w = w.transpose(1, 0, 3, 2).reshape(g, c * cg, p)
    v = power(jnp.arange(c, dtype=F32) + 1.0)[:, :, None, :] * c_mat[None]
    v = v.transpose(1, 3, 0, 2).reshape(g, p, c * cg)
    a = power(jnp.asarray([c_eff], F32))[0]
    return (m.astype(F32), jnp.real(w), jnp.imag(w), jnp.real(v), -jnp.imag(v), jnp.real(a), jnp.imag(a))


def _s5_prompt_kernel(u_ref, m_ref, wre_ref, wim_ref, vre_ref, vim_ref, are_ref, aim_ref,
                      y_ref, hre_ref, him_ref, ere_sc, eim_sc, hsre_sc, hsim_sc):
    ns, nc, _ = u_ref.shape
    p = wre_ref.shape[1]
    ere_sc[...] = jnp.concatenate(
        [jnp.dot(u_ref[s], wre_ref[...], precision=HI, preferred_element_type=F32) for s in range(ns)], axis=1)
    eim_sc[...] = jnp.concatenate(
        [jnp.dot(u_ref[s], wim_ref[...], precision=HI, preferred_element_type=F32) for s in range(ns)], axis=1)
    ar = are_ref[...]
    ai = aim_ref[...]

    def body(k, carry):
        hr, hi = carry
        hsre_sc[pl.ds(k, 1), :] = hr
        hsim_sc[pl.ds(k, 1), :] = hi
        er = ere_sc[pl.ds(k, 1), :]
        ei = eim_sc[pl.ds(k, 1), :]
        return ar * hr - ai * hi + er, ar * hi + ai * hr + ei

    zero = jnp.zeros((1, 2 * SSM_P), F32)
    hr, hi = lax.fori_loop(0, nc, body, (zero, zero))
    hre_ref[...] = hr
    him_ref[...] = hi
    hsre = hsre_sc[...]
    hsim = hsim_sc[...]
    for s in range(ns):
        y_ref[s] = (jnp.dot(u_ref[s], m_ref[...], preferred_element_type=F32)
                    + jnp.dot(hsre[:, s * p:(s + 1) * p], vre_ref[...], preferred_element_type=F32)
                    + jnp.dot(hsim[:, s * p:(s + 1) * p], vim_ref[...], preferred_element_type=F32))


def _s5_pack_kernel(su_ref, u_ref):
    ng, kb = u_ref.shape[0], u_ref.shape[1]
    rows = [su_ref[pl.ds(t, kb, stride=SSM_GROUP), :] for t in range(SSM_GROUP)]
    for g in range(ng):
        u_ref[g] = jnp.concatenate([r[:, g * SSM_GROUP:(g + 1) * SSM_GROUP] for r in rows], axis=1)


def _s5_unpack_kernel(y_ref, o_ref):
    ng, kb = y_ref.shape[0], y_ref.shape[1]
    groups = [y_ref[g] for g in range(ng)]
    for t in range(SSM_GROUP):
        o_ref[pl.ds(t, kb, stride=SSM_GROUP), :] = jnp.concatenate(
            [y[:, t * SSM_GROUP:(t + 1) * SSM_GROUP] for y in groups], axis=1)


def _s5_relayout(x, nb, seq, to_groups):
    c = SSM_GROUP
    nc = seq // c
    cw = c * SSM_GROUP
    ng = 128 // SSM_GROUP
    kb = max(d for d in (64, 32, 16, 8) if nc % d == 0)
    tok = pl.BlockSpec((kb * c, 128), lambda s, i, j: (s * (nc // kb) + i, j))
    grp = pl.BlockSpec((ng, None, kb, cw), lambda s, i, j: (j, s, i, 0))
    return pl.pallas_call(
        _s5_pack_kernel if to_groups else _s5_unpack_kernel,
        out_shape=jax.ShapeDtypeStruct((SSM_G, nb, nc, cw) if to_groups else (nb * seq, SSM_W), F32),
        grid=(nb, nc // kb, SSM_G // ng),
        in_specs=[tok if to_groups else grp],
        out_specs=grp if to_groups else tok,
        compiler_params=_params("parallel", "parallel", "parallel"),
        name="s5_pack" if to_groups else "s5_unpack",
    )(x)


def _s5_prompt(su, mats, nb, seq):
    assert nb == 2
    c = SSM_GROUP
    nc = seq // c
    cw = c * SSM_GROUP
    m, w_re, w_im, v_re, v_im, a_re, a_im = mats
    u = _s5_relayout(su, nb, seq, True)

    gspec = lambda shape: pl.BlockSpec((None,) + shape, lambda g: (g,) + (0,) * len(shape))
    y, hre, him = pl.pallas_call(
        _s5_prompt_kernel,
        out_shape=(jax.ShapeDtypeStruct((SSM_G, nb, nc, cw), F32),
                   jax.ShapeDtypeStruct((SSM_G, 1, nb * SSM_P), F32),
                   jax.ShapeDtypeStruct((SSM_G, 1, nb * SSM_P), F32)),
        grid=(SSM_G,),
        in_specs=[gspec((nb, nc, cw)), gspec((cw, cw)),
                  gspec((cw, SSM_P)), gspec((cw, SSM_P)), gspec((SSM_P, cw)), gspec((SSM_P, cw)),
                  gspec((1, nb * SSM_P)), gspec((1, nb * SSM_P))],
        out_specs=(gspec((nb, nc, cw)), gspec((1, nb * SSM_P)), gspec((1, nb * SSM_P))),
        scratch_shapes=[pltpu.VMEM((nc, nb * SSM_P), F32) for _ in range(4)],
        compiler_params=_params("parallel"),
        name="s5_prompt",
    )(u, m, w_re, w_im, v_re, v_im, jnp.tile(a_re, (1, nb))[:, None, :], jnp.tile(a_im, (1, nb))[:, None, :])
    y = _s5_relayout(y, nb, seq, False)
    unpack = lambda h: h.reshape(SSM_G, nb, SSM_P).transpose(1, 0, 2)
    return y, unpack(hre), unpack(him)


def _s5_sample_kernel(u_ref, h0re_ref, h0im_ref, m_ref, wre_ref, wim_ref, vre_ref, vim_ref, are_ref, aim_ref,
                      y_ref, hre_ref, him_ref):
    u = u_ref[...]
    hr = h0re_ref[...]
    hi = h0im_ref[...]
    ar = are_ref[...]
    ai = aim_ref[...]
    hre_ref[...] = ar * hr - ai * hi + jnp.dot(u, wre_ref[...], precision=HI, preferred_element_type=F32)
    him_ref[...] = ar * hi + ai * hr + jnp.dot(u, wim_ref[...], precision=HI, preferred_element_type=F32)
    y_ref[...] = (jnp.dot(u, m_ref[...], precision=HI, preferred_element_type=F32)
                  + jnp.dot(hr, vre_ref[...], precision=HI, preferred_element_type=F32)
                  + jnp.dot(hi, vim_ref[...], precision=HI, preferred_element_type=F32))


def _s5_sample(su, h0_re, h0_im, mats, n_new):
    ns, r, _ = su.shape
    cw = r * SSM_GROUP
    m, w_re, w_im, v_re, v_im, a_re, a_im = mats
    u = su.reshape(ns, r, SSM_G, SSM_GROUP).transpose(2, 0, 1, 3).reshape(SSM_G, ns, cw)
    gspec = lambda shape: pl.BlockSpec((None,) + shape, lambda g: (g,) + (0,) * len(shape))
    y, hre, him = pl.pallas_call(
        _s5_sample_kernel,
        out_shape=(jax.ShapeDtypeStruct((SSM_G, ns, cw), F32),
                   jax.ShapeDtypeStruct((SSM_G, ns, SSM_P), F32),
                   jax.ShapeDtypeStruct((SSM_G, ns, SSM_P), F32)),
        grid=(SSM_G,),
        in_specs=[gspec((ns, cw)), gspec((ns, SSM_P)), gspec((ns, SSM_P)), gspec((cw, cw)),
                  gspec((cw, SSM_P)), gspec((cw, SSM_P)), gspec((SSM_P, cw)), gspec((SSM_P, cw)),
                  gspec((1, SSM_P)), gspec((1, SSM_P))],
        out_specs=(gspec((ns, cw)), gspec((ns, SSM_P)), gspec((ns, SSM_P))),
        compiler_params=_params("parallel"),
        name="s5_sample",
    )(u, h0_re.transpose(1, 0, 2), h0_im.transpose(1, 0, 2), m, w_re, w_im, v_re, v_im,
      a_re[:, None, :], a_im[:, None, :])
    y = y.reshape(SSM_G, ns, r, SSM_GROUP)[:, :, :n_new].transpose(1, 2, 0, 3).reshape(ns * n_new, SSM_W)
    return y, hre.transpose(1, 0, 2), him.transpose(1, 0, 2)


def _glu_kernel(y_ref, w_ref, b_ref, g_ref, o_ref):
    sy = _gelu(y_ref[...])
    gate = _sigmoid(jnp.dot(sy.astype(BF16), w_ref[...], preferred_element_type=F32) + b_ref[...])
    o_ref[...] = _rms_norm(sy * gate, g_ref[...])


def _s5_glu(y, w_glu, b_glu, norm_g, tm):
    t = y.shape[0]
    row = lambda: pl.BlockSpec((1, SSM_W), lambda i: (0, 0))
    return pl.pallas_call(
        _glu_kernel,
        out_shape=jax.ShapeDtypeStruct((t, SSM_W), F32),
        grid=(t // tm,),
        in_specs=[pl.BlockSpec((tm, SSM_W), lambda i: (i, 0)),
                  pl.BlockSpec((SSM_W, SSM_W), lambda i: (0, 0)), row(), row()],
        out_specs=pl.BlockSpec((tm, SSM_W), lambda i: (i, 0)),
        compiler_params=_params("parallel"),
        name="s5_glu",
    )(y, w_glu.astype(BF16), b_glu.reshape(1, SSM_W), norm_g.reshape(1, SSM_W))


def _outproj_kernel(ro_ref, fo_ref, so_ref, x_ref, wr_ref, wf_ref, ws_ref, g_ref, b_ref, o_ref):
    mix = jnp.dot(ro_ref[...].astype(BF16), wr_ref[...], preferred_element_type=F32)
    mix += jnp.dot(fo_ref[...].astype(BF16), wf_ref[...], preferred_element_type=F32)
    mix += jnp.dot(so_ref[...].astype(BF16), ws_ref[...], preferred_element_type=F32)
    o_ref[...] = _layer_norm(DEEPNORM_ALPHA * x_ref[...] + mix, g_ref[...], b_ref[...])


def _outproj_ln(ro, fo, so, x, w_out, ln_g, ln_b, tm):
    t = x.shape[0]
    w = w_out.astype(BF16)
    tile = lambda n: pl.BlockSpec((tm, n), lambda i: (i, 0))
    full = lambda r, n: pl.BlockSpec((r, n), lambda i: (0, 0))
    return pl.pallas_call(
        _outproj_kernel,
        out_shape=jax.ShapeDtypeStruct((t, D_MODEL), F32),
        grid=(t // tm,),
        in_specs=[tile(RET_W), tile(FOX_W), tile(SSM_W), tile(D_MODEL),
                  full(RET_W, D_MODEL), full(FOX_W, D_MODEL), full(SSM_W, D_MODEL),
                  full(1, D_MODEL), full(1, D_MODEL)],
        out_specs=tile(D_MODEL),
        compiler_params=_params("parallel"),
        name="outproj_ln",
    )(ro, fo, so, x, w[:RET_W], w[RET_W:RET_W + FOX_W], w[RET_W + FOX_W:],
      ln_g.reshape(1, D_MODEL), ln_b.reshape(1, D_MODEL))


def _topk_rows(problems, k):
    def body(it, _):
        for s_ref, val_ref, idx_ref in problems:
            n = s_ref.shape[0]
            iota = lax.broadcasted_iota(jnp.int32, s_ref.shape, 0)
            s = s_ref[...]
            m = jnp.max(s, axis=0, keepdims=True)
            idx = jnp.min(jnp.where(s == m, iota, n), axis=0, keepdims=True)
            val_ref[pl.ds(it, 1), :] = m
            idx_ref[pl.ds(it, 1), :] = idx
            s_ref[...] = jnp.where(iota == idx, -jnp.inf, s)
        return 0

    lax.fori_loop(0, k, body, 0)


CAND_COUNTS = [PEER_TOPK // (a + 1) for a in range(PEER_TOPK)]
CAND_STARTS = [sum(CAND_COUNTS[:a]) for a in range(PEER_TOPK)]
N_CAND = sum(CAND_COUNTS)
N_CAND_PAD = -(-N_CAND // 8) * 8


def _peer_topk_kernel(q_ref, sk_ref, i_ref, j_ref, gate_ref, s0_sc, s1_sc, cand_sc, v0_sc, v1_sc, x0_sc, x1_sc,
                      top_sc, sel_sc):
    kk = PEER_TOPK
    s = lax.dot_general(sk_ref[...], q_ref[...], NT_DIMS, precision=HI, preferred_element_type=F32)
    s0_sc[...] = s[:PEER_NKEYS]
    s1_sc[...] = s[PEER_NKEYS:]
    _topk_rows([(s0_sc, v0_sc, x0_sc), (s1_sc, v1_sc, x1_sc)], kk)
    v1 = v1_sc[...]
    for a in range(kk):
        cand_sc[CAND_STARTS[a]:CAND_STARTS[a] + CAND_COUNTS[a], :] = v0_sc[a:a + 1, :] + v1[:CAND_COUNTS[a]]
    if N_CAND_PAD > N_CAND:
        cand_sc[N_CAND:, :] = jnp.full((N_CAND_PAD - N_CAND, cand_sc.shape[1]), -jnp.inf, F32)
    _topk_rows([(cand_sc, top_sc, sel_sc)], kk)
    sel = sel_sc[...]
    a_idx = jnp.zeros_like(sel)
    start = jnp.zeros_like(sel)
    for a in range(1, kk):
        past = sel >= CAND_STARTS[a]
        a_idx = jnp.where(past, a, a_idx)
        start = jnp.where(past, CAND_STARTS[a], start)
    b_idx = sel - start
    ii = jnp.zeros_like(sel)
    jj = jnp.zeros_like(sel)
    for a in range(kk):
        ii = jnp.where(a_idx == a, x0_sc[a:a + 1, :], ii)
        jj = jnp.where(b_idx == a, x1_sc[a:a + 1, :], jj)
    top = top_sc[...]
    e = jnp.exp(top - top[0:1, :])
    i_ref[...] = ii
    j_ref[...] = jj
    gate_ref[...] = e / jnp.sum(e, axis=0, keepdims=True)


def _peer_topk(q, sub_keys, tm):
    t = q.shape[0]
    half = PEER_DK // 2
    sk = jnp.zeros((PEER_HEADS, 2 * PEER_NKEYS, PEER_DK), F32)
    sk = sk.at[:, :PEER_NKEYS, :half].set(sub_keys[:, 0]).at[:, PEER_NKEYS:, half:].set(sub_keys[:, 1])
    kk = PEER_TOPK
    slot = lambda: pl.BlockSpec((kk, tm), lambda i, h: (h, i))
    return pl.pallas_call(
        _peer_topk_kernel,
        out_shape=(jax.ShapeDtypeStruct((PEER_HEADS * kk, t), jnp.int32),
                   jax.ShapeDtypeStruct((PEER_HEADS * kk, t), jnp.int32),
                   jax.ShapeDtypeStruct((PEER_HEADS * kk, t), F32)),
        grid=(t // tm, PEER_HEADS),
        in_specs=[pl.BlockSpec((tm, PEER_DK), lambda i, h: (i, h)),
                  pl.BlockSpec((None, 2 * PEER_NKEYS, PEER_DK), lambda i, h: (h, 0, 0))],
        out_specs=(slot(), slot(), slot()),
        scratch_shapes=[pltpu.VMEM((PEER_NKEYS, tm), F32), pltpu.VMEM((PEER_NKEYS, tm), F32),
                        pltpu.VMEM((N_CAND_PAD, tm), F32),
                        pltpu.VMEM((kk, tm), F32), pltpu.VMEM((kk, tm), F32),
                        pltpu.VMEM((kk, tm), jnp.int32), pltpu.VMEM((kk, tm), jnp.int32),
                        pltpu.VMEM((kk, tm), F32), pltpu.VMEM((kk, tm), jnp.int32)],
        compiler_params=_params("parallel", "arbitrary"),
        name="peer_topk",
    )(q, sk)


G_TOKENS = 128
G_ROW_STRIDE = G_TOKENS + 8
G_UNROLL = 32


def _peer_gates_kernel(i_ref, j_ref, gate_ref, o_ref, g_sc):
    tb, ns = i_ref.shape
    nk = PEER_NKEYS
    io = lax.broadcasted_iota(jnp.int32, (nk, ns), 0)

    def body(tt, carry):
        for u in range(G_UNROLL):
            t = tt * G_UNROLL + u
            irow = i_ref[pl.ds(t, 1), :]
            jrow = j_ref[pl.ds(t, 1), :]
            grow = gate_ref[pl.ds(t, 1), :]
            pt = jnp.where(io == irow, grow, 0.0).astype(BF16)
            qt = jnp.where(io == jrow, 1.0, 0.0).astype(BF16)
            g = lax.dot_general(pt, qt, NT_DIMS, preferred_element_type=F32)
            g_sc[pl.ds(t, nk, stride=G_ROW_STRIDE), :] = g
        return carry

    lax.fori_loop(0, tb // G_UNROLL, body, 0)
    for i in range(nk):
        o_ref[:, i * nk:(i + 1) * nk] = g_sc[i * G_ROW_STRIDE:i * G_ROW_STRIDE + tb, :].astype(BF16)


def _peer_gates(ii, jj, gate):
    t = ii.shape[0]
    tb = G_TOKENS
    ns = PEER_HEADS * PEER_TOPK
    tok = lambda: pl.BlockSpec((tb, ns), lambda i: (i, 0))
    return pl.pallas_call(
        _peer_gates_kernel,
        out_shape=jax.ShapeDtypeStruct((t, PEER_N), BF16),
        grid=(t // tb,),
        in_specs=[tok(), tok(), tok()],
        out_specs=pl.BlockSpec((tb, PEER_N), lambda i: (i, 0)),
        scratch_shapes=[pltpu.VMEM((PEER_NKEYS * G_ROW_STRIDE, PEER_NKEYS), F32)],
        compiler_params=_params("parallel"),
        name="peer_gates",
    )(ii, jj, gate)


def _peer_experts_kernel(x_ref, g_ref, u_ref, v_ref, lg_ref, lb_ref, o_ref, xb_sc, acc_sc):
    e = pl.program_id(1)

    @pl.when(e == 0)
    def _():
        xb_sc[...] = x_ref[...].astype(BF16)
        acc_sc[...] = jnp.zeros_like(acc_sc)

    h = lax.dot_general(xb_sc[...], u_ref[...], NT_DIMS, preferred_element_type=F32)
    w = (g_ref[...].astype(F32) * _gelu(h)).astype(BF16)
    acc_sc[...] += jnp.dot(w, v_ref[...], preferred_element_type=F32)

    @pl.when(e == pl.num_programs(1) - 1)
    def _():
        o_ref[...] = _layer_norm(DEEPNORM_ALPHA * x_ref[...] + acc_sc[...], lg_ref[...], lb_ref[...])


def _peer_experts_ln(x, g, u_tabs, v_tabs, layer, ln_g, ln_b, tm, nb):
    t = x.shape[0]
    return pl.pallas_call(
        _peer_experts_kernel,
        out_shape=jax.ShapeDtypeStruct((t, D_MODEL), F32),
        grid=(t // tm, PEER_N // nb),
        in_specs=[pl.BlockSpec((tm, D_MODEL), lambda i, e: (i, 0)),
                  pl.BlockSpec((tm, nb), lambda i, e: (i, e)),
                  pl.BlockSpec((None, nb, D_MODEL), lambda i, e: (layer, e, 0)),
                  pl.BlockSpec((None, nb, D_MODEL), lambda i, e: (layer, e, 0)),
                  pl.BlockSpec((1, D_MODEL), lambda i, e: (0, 0)),
                  pl.BlockSpec((1, D_MODEL), lambda i, e: (0, 0))],
        out_specs=pl.BlockSpec((tm, D_MODEL), lambda i, e: (i, 0)),
        scratch_shapes=[pltpu.VMEM((tm, D_MODEL), BF16), pltpu.VMEM((tm, D_MODEL), F32)],
        compiler_params=_params("parallel", "arbitrary"),
        name="peer_experts_ln",
    )(x, g, u_tabs, v_tabs, ln_g.reshape(1, D_MODEL), ln_b.reshape(1, D_MODEL))


def _ple_kernel(x_ref, xn_ref, p_ref, wg_ref, bg_ref, we_ref, o_ref):
    gate = _sigmoid(jnp.dot(x_ref[...].astype(BF16), wg_ref[...], preferred_element_type=F32) + bg_ref[...])
    emb = jnp.dot(p_ref[...].astype(BF16), we_ref[...], preferred_element_type=F32)
    o_ref[...] = xn_ref[...] + gate * emb


def _ple(x, p, w_pg, b_pg, w_pe, tm, tn):
    t = x.shape[0]
    return pl.pallas_call(
        _ple_kernel,
        out_shape=jax.ShapeDtypeStruct((t, D_MODEL), F32),
        grid=(t // tm, D_MODEL // tn),
        in_specs=[pl.BlockSpec((tm, D_MODEL), lambda i, j: (i, 0)),
                  pl.BlockSpec((tm, tn), lambda i, j: (i, j)),
                  pl.BlockSpec((tm, PLE_DIM), lambda i, j: (i, 0)),
                  pl.BlockSpec((D_MODEL, tn), lambda i, j: (0, j)),
                  pl.BlockSpec((1, tn), lambda i, j: (0, j)),
                  pl.BlockSpec((PLE_DIM, tn), lambda i, j: (0, j))],
        out_specs=pl.BlockSpec((tm, tn), lambda i, j: (i, j)),
        compiler_params=_params("parallel", "arbitrary"),
        name="ple_gate",
    )(x, x, p, w_pg.astype(BF16), b_pg.reshape(1, D_MODEL), w_pe.astype(BF16))


def _split_bf16(w):
    hi = w.astype(BF16)
    return hi, (w - hi.astype(F32)).astype(BF16)


def _layer(x, p, lw, caches, stacked, page_table, layer, nb, seq, ns, n_new, tiles):
    t_all = x.shape[0]
    tp = nb * seq
    r = 8
    cache_k, cache_v, lf_sums, state_ret, state_re, state_im = caches
    past_len = page_table.shape[1] * (cache_k.shape[2] // H_FOX)

    u_tabs, v_tabs, w_in_bf16, w_in = stacked
    z = _in_proj(x, w_in_bf16, layer, tiles['tm_in'], 512)
    lf = _forget_gates(x, w_in[layer, :, Z_FF:Z_SU], lw['b_fox_f'], tiles['tm'])
    zs = jnp.pad(z[tp:].reshape(ns, n_new, Z_W), ((0, 0), (0, r - n_new), (0, 0)))

    ro, ret_p = _retention_prompt(z, lw['ret_gn_g'], nb, seq, t_all)
    ro_s, ret_s = _retention_sample(zs, state_ret, layer, lw['ret_gn_g'], past_len, n_new)
    ro = ro.at[tp:].set(ro_s[:, :n_new].reshape(ns * n_new, RET_W))

    fo = _fox_prompt(z, _forget_cumsum(lf, nb, seq), lw['fox_norm_g'], nb, seq, t_all, tiles['attn'])
    lf_s = lf[:, tp:].T.reshape(ns, n_new, H_FOX)
    lfn = jnp.pad(lf_s, ((0, 0), (0, r - n_new), (0, 0))).reshape(ns, 1, r * H_FOX)
    fo_s = _fox_sample(z[tp:, Z_FQ:Z_FQ + FOX_W].reshape(ns, n_new * H_FOX, DH_FOX),
                       zs[:, :, Z_FK:Z_FK + FOX_W].reshape(ns, r * H_FOX, DH_FOX),
                       zs[:, :, Z_FV:Z_FV + FOX_W].reshape(ns, r * H_FOX, DH_FOX),
                       lfn, cache_k, cache_v, lf_sums, page_table, layer, lw['fox_norm_g'])
    fo = fo.at[tp:].set(fo_s.reshape(ns * n_new, FOX_W))

    ssm = (lw['ssm_a_re'], lw['ssm_a_im'], lw['ssm_b_re'], lw['ssm_b_im'], lw['ssm_c_re'], lw['ssm_c_im'],
           lw['ssm_d'], lw['ssm_log_dt'])
    y_p, hre_p, him_p = _s5_prompt(z[:tp, Z_SU:], _s5_mats(*ssm, SSM_GROUP, SSM_GROUP), nb, seq)
    y_s, hre_s, him_s = _s5_sample(zs[:, :, Z_SU:], state_re[layer], state_im[layer], _s5_mats(*ssm, r, n_new), n_new)
    so = _s5_glu(jnp.concatenate([y_p, y_s], 0), lw['w_glu'], lw['b_glu'], lw['ssm_norm_g'], tiles['tm'])

    x1 = _outproj_ln(ro, fo, so, x, lw['w_out'], lw['ln1_g'], lw['ln1_b'], tiles['tm_ln'])

    q = _matmul(x1, _split_bf16(lw['peer_w_q']), tiles['tm'], 512, "peer_query")
    ii, jj, gate = _peer_topk(q, lw['peer_sub_keys'], tiles['tm'])
    g = _peer_gates(ii.T, jj.T, gate.T)
    x2 = _peer_experts_ln(x1, g, u_tabs, v_tabs, layer, lw['ln2_g'], lw['ln2_b'], tiles['tm'], 512)

    x3 = _ple(x2, p, lw['w_pg'], lw['b_pg'], lw['w_pe'], tiles['tm'], 512)

    st_p = (z[:tp, Z_FK:Z_FK + FOX_W].reshape(nb, seq, H_FOX, DH_FOX),
            z[:tp, Z_FV:Z_FV + FOX_W].reshape(nb, seq, H_FOX, DH_FOX),
            lf[:, :tp].T.reshape(nb, seq, H_FOX), ret_p, hre_p, him_p)
    st_s = (z[tp:, Z_FK:Z_FK + FOX_W].reshape(ns, n_new, H_FOX, DH_FOX),
            z[tp:, Z_FV:Z_FV + FOX_W].reshape(ns, n_new, H_FOX, DH_FOX),
            lf_s, ret_s, hre_s, him_s)
    return x3, st_p, st_s


def kernel(x_prompt, x_sample, p_prompt, p_sample, cache_fox_k, cache_fox_v, cache_fox_lf, state_ret, state_ssm_re, state_ssm_im, page_table, w_in, w_out, ret_gn_g, b_fox_f, fox_norm_g, ssm_a_re, ssm_a_im, ssm_b_re, ssm_b_im, ssm_c_re, ssm_c_im, ssm_d, ssm_log_dt, w_glu, b_glu, ssm_norm_g, ln1_g, ln1_b, ln2_g, ln2_b, peer_w_q, peer_sub_keys, peer_u, peer_v, w_pe, w_pg, b_pg):
    nb, seq, d = x_prompt.shape
    ns, n_new, _ = x_sample.shape
    tp, ts = nb * seq, ns * n_new
    t_all = tp + ts
    tm = 640 if t_all % 640 == 0 else 128
    tiles = dict(tm=tm, tm_ln=tm // 2 if tm == 640 else tm, tm_in=832 if t_all % 832 == 0 else tm,
                 attn=512 if seq % 512 == 0 else (256 if seq % 256 == 0 else 128))
    x = jnp.concatenate([x_prompt.reshape(tp, d), x_sample.reshape(ts, d)], 0)
    depth, n_pool, pg, nh, dh = cache_fox_k.shape
    caches = (cache_fox_k.reshape(depth, n_pool, pg * nh, dh), cache_fox_v.reshape(depth, n_pool, pg * nh, dh),
              _lf_prep(cache_fox_lf), state_ret, state_ssm_re, state_ssm_im)
    weights = dict(w_out=w_out, ret_gn_g=ret_gn_g, b_fox_f=b_fox_f, fox_norm_g=fox_norm_g,
                   ssm_a_re=ssm_a_re, ssm_a_im=ssm_a_im, ssm_b_re=ssm_b_re, ssm_b_im=ssm_b_im,
                   ssm_c_re=ssm_c_re, ssm_c_im=ssm_c_im, ssm_d=ssm_d, ssm_log_dt=ssm_log_dt,
                   w_glu=w_glu, b_glu=b_glu, ssm_norm_g=ssm_norm_g, ln1_g=ln1_g, ln1_b=ln1_b,
                   ln2_g=ln2_g, ln2_b=ln2_b, peer_w_q=peer_w_q, peer_sub_keys=peer_sub_keys,
                   w_pe=w_pe, w_pg=w_pg, b_pg=b_pg)
    stacked = (peer_u.astype(BF16), peer_v.astype(BF16), w_in.astype(BF16), w_in)
    sts_p, sts_s = [], []
    for l in range(DEPTH):
        lw = {k: v[l] for k, v in weights.items()}
        p = jnp.concatenate([p_prompt[l].reshape(tp, PLE_DIM), p_sample[l].reshape(ts, PLE_DIM)], 0)
        x, st_p, st_s = _layer(x, p, lw, caches, stacked, page_table, l, nb, seq, ns, n_new, tiles)
        sts_p.append(st_p)
        sts_s.append(st_s)
    stack = lambda sts, i: jnp.stack([s[i] for s in sts])
    return ((x[:tp].reshape(nb, seq, d), x[tp:].reshape(ns, n_new, d))
            + tuple(stack(sts_p, i) for i in range(6)) + tuple(stack(sts_s, i) for i in range(6)))
```

```python
import functools
import math

import jax
import jax.numpy as jnp
from jax import lax
from jax.experimental import pallas as pl
from jax.experimental.pallas import tpu as pltpu

F32 = jnp.float32
BF16 = jnp.bfloat16
HI = lax.Precision.HIGHEST

D_MODEL = 2048
DEPTH = 2
H_RET, DH_RET = 4, 128
RET_W = H_RET * DH_RET
H_FOX, DH_FOX = 8, 128
FOX_W = H_FOX * DH_FOX
SSM_W, SSM_GROUP = 512, 16
SSM_G = SSM_W // SSM_GROUP
SSM_P = 64
PEER_HEADS, PEER_NKEYS, PEER_DK, PEER_TOPK = 8, 128, 128, 16
PEER_N = PEER_NKEYS * PEER_NKEYS
PLE_DIM = 256
BLOCK = 128
ROPE_BASE = 10000.0
DEEPNORM_ALPHA = (2 * DEPTH) ** 0.25
EPS = 1e-5

Z_FQ, Z_FK, Z_FV, Z_FF = 4 * RET_W, 4 * RET_W + FOX_W, 4 * RET_W + 2 * FOX_W, 4 * RET_W + 3 * FOX_W
Z_SU = Z_FF + H_FOX
Z_W = Z_SU + SSM_W

VMEM_LIMIT = 56 * 1024 * 1024
NT_DIMS = (((1,), (1,)), ((), ()))
TN_DIMS = (((0,), (0,)), ((), ()))


def _params(*sem):
    return pltpu.CompilerParams(dimension_semantics=sem, vmem_limit_bytes=VMEM_LIMIT)


def _gelu(x):
    return 0.5 * x * (1.0 + jnp.tanh(math.sqrt(2.0 / math.pi) * (x + 0.044715 * (x * x * x))))


def _sigmoid(x):
    return 1.0 / (1.0 + jnp.exp(-x))


def _log_sigmoid(x):
    return jnp.minimum(x, 0.0) - jnp.log1p(jnp.exp(-jnp.abs(x)))


def _layer_norm(y, g, b):
    mu = jnp.mean(y, -1, keepdims=True)
    var = jnp.mean(jnp.square(y - mu), -1, keepdims=True)
    return (y - mu) * lax.rsqrt(var + EPS) * g + b


def _rms_norm(y, g):
    return y * lax.rsqrt(jnp.mean(y * y, -1, keepdims=True) + EPS) * g


def _mm_kernel(x_ref, w_ref, o_ref):
    o_ref[...] = jnp.dot(x_ref[...].astype(BF16), w_ref[...], preferred_element_type=F32)


def _mm3_kernel(x_ref, wh_ref, wl_ref, o_ref):
    x = x_ref[...]
    xh = x.astype(BF16)
    xl = (x - xh.astype(F32)).astype(BF16)
    wh = wh_ref[...]
    acc = jnp.dot(xh, wh, preferred_element_type=F32)
    acc += jnp.dot(xl, wh, preferred_element_type=F32)
    acc += jnp.dot(xh, wl_ref[...], preferred_element_type=F32)
    o_ref[...] = acc


def _matmul(x, ws, tm, tn, name):
    t, k = x.shape
    n = ws[0].shape[1]
    kern = _mm_kernel if len(ws) == 1 else _mm3_kernel
    return pl.pallas_call(
        kern,
        out_shape=jax.ShapeDtypeStruct((t, n), F32),
        grid=(t // tm, n // tn),
        in_specs=[pl.BlockSpec((tm, k), lambda i, j: (i, 0))]
        + [pl.BlockSpec((k, tn), lambda i, j: (0, j)) for _ in ws],
        out_specs=pl.BlockSpec((tm, tn), lambda i, j: (i, j)),
        compiler_params=_params("parallel", "arbitrary"),
        name=name,
    )(x, *ws)


def _mm_w32_kernel(x_ref, w_ref, o_ref, xb_sc):
    @pl.when(pl.program_id(1) == 0)
    def _():
        xb_sc[...] = x_ref[...].astype(BF16)

    o_ref[...] = jnp.dot(xb_sc[...], w_ref[...], preferred_element_type=F32)


def _in_proj(x, w_all, layer, tm, tn):
    t, k = x.shape
    n = w_all.shape[2]
    return pl.pallas_call(
        _mm_w32_kernel,
        out_shape=jax.ShapeDtypeStruct((t, n), F32),
        grid=(t // tm, pl.cdiv(n, tn)),
        in_specs=[pl.BlockSpec((tm, k), lambda i, j: (i, 0)),
                  pl.BlockSpec((None, k, tn), lambda i, j: (layer, 0, j))],
        out_specs=pl.BlockSpec((tm, tn), lambda i, j: (i, j)),
        scratch_shapes=[pltpu.VMEM((tm, k), BF16)],
        compiler_params=_params("parallel", "arbitrary"),
        name="in_proj",
    )(x, w_all)


def _gate_kernel(x_ref, w_ref, b_ref, lf_ref):
    f = lax.dot_general(w_ref[...], x_ref[...], NT_DIMS, precision=HI, preferred_element_type=F32)
    lf_ref[...] = _log_sigmoid(f + b_ref[...])


def _forget_gates(x, w_ff, b_ff, tm):
    t, k = x.shape
    return pl.pallas_call(
        _gate_kernel,
        out_shape=jax.ShapeDtypeStruct((H_FOX, t), F32),
        grid=(t // tm,),
        in_specs=[
            pl.BlockSpec((tm, k), lambda i: (i, 0)),
            pl.BlockSpec((H_FOX, k), lambda i: (0, 0)),
            pl.BlockSpec((H_FOX, 1), lambda i: (0, 0)),
        ],
        out_specs=pl.BlockSpec((H_FOX, tm), lambda i: (0, i)),
        compiler_params=_params("parallel"),
        name="forget_gates",
    )(x, w_ff.T, b_ff.reshape(H_FOX, 1))


def _cumsum_kernel(lf_ref, c_ref, carry):
    @pl.when(pl.program_id(1) == 0)
    def _():
        carry[...] = jnp.zeros_like(carry)

    n = lf_ref.shape[1]
    r = lax.broadcasted_iota(jnp.int32, (n, n), 0)
    c = lax.broadcasted_iota(jnp.int32, (n, n), 1)
    upper = (r <= c).astype(F32)
    cs = jnp.dot(lf_ref[...], upper, precision=HI, preferred_element_type=F32) + carry[...]
    c_ref[...] = cs
    carry[...] = cs[:, n - 1:n]


def _forget_cumsum(lf, nb, seq):
    n = 256 if seq % 256 == 0 else 128
    nc = seq // n
    return pl.pallas_call(
        _cumsum_kernel,
        out_shape=jax.ShapeDtypeStruct((H_FOX, nb * seq), F32),
        grid=(nb, nc),
        in_specs=[pl.BlockSpec((H_FOX, n), lambda b, i: (0, b * nc + i))],
        out_specs=pl.BlockSpec((H_FOX, n), lambda b, i: (0, b * nc + i)),
        scratch_shapes=[pltpu.VMEM((H_FOX, 1), F32)],
        compiler_params=_params("parallel", "arbitrary"),
        name="forget_cumsum",
    )(lf)


def _fox_prompt_kernel(q_ref, k_ref, v_ref, c_ref, g_ref, o_ref, qs_sc, m_sc, l_sc, acc_sc, *, scale):
    qi = pl.program_id(1)
    ki = pl.program_id(2)
    bq = q_ref.shape[0]
    bk = k_ref.shape[0]
    log2e = math.log2(math.e)

    @pl.when(ki == 0)
    def _():
        qs_sc[...] = (q_ref[...] * (scale * log2e)).astype(BF16)
        m_sc[...] = jnp.full_like(m_sc, -jnp.inf)
        l_sc[...] = jnp.zeros_like(l_sc)
        acc_sc[...] = jnp.zeros_like(acc_sc)

    last = (qi * bq + bq - 1) // bk

    def attend(diagonal):
        if diagonal:
            causal = (ki * bk + lax.broadcasted_iota(jnp.int32, (bq, bk), 1)
                      <= qi * bq + lax.broadcasted_iota(jnp.int32, (bq, bk), 0))
        c2 = c_ref[...] * log2e
        for h in range(H_FOX):
            sl = slice(h * DH_FOX, (h + 1) * DH_FOX)
            k = k_ref[:, sl].astype(BF16)
            s = lax.dot_general(qs_sc[:, sl], k, NT_DIMS, preferred_element_type=F32) - c2[h:h + 1, :]
            if diagonal:
                s = jnp.where(causal, s, -jnp.inf)
            m_prev = m_sc[h]
            m_new = jnp.maximum(m_prev, jnp.max(s, axis=1, keepdims=True))
            alpha = jnp.exp2(m_prev - m_new)
            p = jnp.exp2(s - jnp.tile(m_new, (1, bk // 128)))
            l_sc[h] = alpha * l_sc[h] + jnp.sum(p, axis=1, keepdims=True)
            acc_sc[:, sl] = alpha * acc_sc[:, sl] + jnp.dot(
                p.astype(BF16), v_ref[:, sl].astype(BF16), preferred_element_type=F32)
            m_sc[h] = m_new

    @pl.when(ki < last)
    def _():
        attend(False)

    @pl.when(ki == last)
    def _():
        attend(True)
        for h in range(H_FOX):
            sl = slice(h * DH_FOX, (h + 1) * DH_FOX)
            acc_sc[:, sl] = acc_sc[:, sl] / l_sc[h]
        o_ref[...] = _rms_norm(acc_sc[...], g_ref[...])


def _fox_prompt(z, c, norm_g, nb, seq, t_all, blk):
    bk = blk
    nq, nk = seq // blk, seq // bk
    cq, ck, cv = Z_FQ // FOX_W, Z_FK // FOX_W, Z_FV // FOX_W
    kern = functools.partial(_fox_prompt_kernel, scale=DH_FOX ** -0.5)

    def key_block(b, i, j):
        return b * nk + jnp.minimum(j, (i * blk + blk - 1) // bk)

    return pl.pallas_call(
        kern,
        out_shape=jax.ShapeDtypeStruct((t_all, FOX_W), F32),
        grid=(nb, nq, nk),
        in_specs=[
            pl.BlockSpec((blk, FOX_W), lambda b, i, j: (b * nq + i, cq)),
            pl.BlockSpec((bk, FOX_W), lambda b, i, j: (key_block(b, i, j), ck)),
            pl.BlockSpec((bk, FOX_W), lambda b, i, j: (key_block(b, i, j), cv)),
            pl.BlockSpec((H_FOX, bk), lambda b, i, j: (0, key_block(b, i, j))),
            pl.BlockSpec((1, FOX_W), lambda b, i, j: (0, 0)),
        ],
        out_specs=pl.BlockSpec((blk, FOX_W), lambda b, i, j: (b * nq + i, 0)),
        scratch_shapes=[
            pltpu.VMEM((blk, FOX_W), BF16),
            pltpu.VMEM((H_FOX, blk, 128), F32),
            pltpu.VMEM((H_FOX, blk, 128), F32),
            pltpu.VMEM((blk, FOX_W), F32),
        ],
        compiler_params=_params("parallel", "parallel", "arbitrary"),
        name="fox_prompt",
    )(z, z, z, c, norm_g.reshape(1, FOX_W))


def _lf_prep_kernel(lf_ref, w_ref, o_ref):
    x = lf_ref[...]
    p1 = x.astype(BF16)
    r1 = x - p1.astype(F32)
    p2 = r1.astype(BF16)
    p3 = (r1 - p2.astype(F32)).astype(BF16)
    w = w_ref[...]
    acc = jnp.dot(p3, w, preferred_element_type=F32)
    acc += jnp.dot(p2, w, preferred_element_type=F32)
    acc += jnp.dot(p1, w, preferred_element_type=F32)
    o_ref[...] = acc


def _lf_prep(cache_lf):
    depth, n_pool, pg, nh = cache_lf.shape
    c = pg * nh
    rows = depth * n_pool
    tm = max(d for d in (256, 128, 64, 32, 16, 8) if rows % d == 0)
    src = jnp.arange(c)[:, None]
    dst = jnp.arange(c)[None, :]
    same_head = (src % nh) == (dst % nh)
    w = jnp.concatenate([same_head & (src // nh > dst // nh), same_head], axis=1).astype(BF16)
    return pl.pallas_call(
        _lf_prep_kernel,
        out_shape=jax.ShapeDtypeStruct((rows, 2 * c), F32),
        grid=(rows // tm,),
        in_specs=[pl.BlockSpec((tm, c), lambda i: (i, 0)), pl.BlockSpec((c, 2 * c), lambda i: (0, 0))],
        out_specs=pl.BlockSpec((tm, 2 * c), lambda i: (i, 0)),
        compiler_params=_params("parallel"),
        name="lf_prep",
    )(cache_lf.reshape(rows, c), w)


def _fox_sample_kernel(pt_ref, q_ref, kn_ref, vn_ref, lfn_ref, g_ref, *rest, scale, npg, page_of):
    k_refs, v_refs, x_refs = rest[:npg], rest[npg:2 * npg], rest[2 * npg:3 * npg]
    o_ref, m_sc, l_sc, acc_sc, carry_sc = rest[3 * npg:]
    step = pl.program_id(1)
    nr = q_ref.shape[0]
    ncol = k_refs[0].shape[0]
    head_mask = H_FOX - 1
    head_bits = H_FOX.bit_length() - 1

    @pl.when(step == 0)
    def _():
        m_sc[...] = jnp.full_like(m_sc, -jnp.inf)
        l_sc[...] = jnp.zeros_like(l_sc)
        acc_sc[...] = jnp.zeros_like(acc_sc)
        carry_sc[...] = jnp.zeros_like(carry_sc)

    def equal_heads(ncol):
        rows = lax.broadcasted_iota(jnp.int32, (nr, ncol), 0)
        cols = lax.broadcasted_iota(jnp.int32, (nr, ncol), 1)
        return rows, cols, (rows & head_mask) == (cols & head_mask)

    q = q_ref[...].astype(BF16)
    _, _, same_head = equal_heads(ncol)
    carry = carry_sc[...]
    m_prev = m_sc[...]
    m_new = m_prev
    scores = []
    for i in range(npg):
        x = x_refs[i][pl.ds(page_of(pl.program_id(0), step, i, pt_ref) % 8, 1), :]
        bias = x[:, :ncol] + carry
        carry = carry + x[:, ncol:]
        s = lax.dot_general(q, k_refs[i][...].astype(BF16), NT_DIMS, preferred_element_type=F32) * scale + bias
        s = jnp.where(same_head, s, -jnp.inf)
        m_new = jnp.maximum(m_new, jnp.max(s, axis=1, keepdims=True))
        scores.append(s)
    carry_sc[...] = carry
    alpha = jnp.exp(m_prev - m_new)
    l_new = alpha * l_sc[...]
    acc = alpha * acc_sc[...]
    for i in range(npg):
        p = jnp.exp(scores[i] - m_new)
        l_new += jnp.sum(p, axis=1, keepdims=True)
        acc += jnp.dot(p.astype(BF16), v_refs[i][...].astype(BF16), preferred_element_type=F32)
    m_sc[...] = m_new
    l_sc[...] = l_new
    acc_sc[...] = acc

    @pl.when(step == pl.num_programs(1) - 1)
    def _():
        nc = kn_ref.shape[0]
        rows, cols, same = equal_heads(nc)
        causal = (cols >> head_bits) <= (rows >> head_bits)
        r2 = lax.broadcasted_iota(jnp.int32, (nc, nc), 0)
        c2 = lax.broadcasted_iota(jnp.int32, (nc, nc), 1)
        prefix = jnp.where((r2 & head_mask) == (c2 & head_mask), jnp.where(r2 <= c2, 1.0, 0.0), 0.0)
        cn = jnp.dot(jnp.broadcast_to(lfn_ref[...], (8, nc)), prefix, precision=HI,
                     preferred_element_type=F32)[0:1]
        s = lax.dot_general(q, kn_ref[...].astype(BF16), NT_DIMS, preferred_element_type=F32) * scale - cn
        s = jnp.where(same, jnp.where(causal, s, -jnp.inf), -jnp.inf)
        m_fin = jnp.maximum(m_new, jnp.max(s, axis=1, keepdims=True))
        a_fin = jnp.exp(m_new - m_fin)
        p = jnp.exp(s - m_fin)
        l_fin = a_fin * l_new + jnp.sum(p, axis=1, keepdims=True)
        out = (a_fin * acc + jnp.dot(p.astype(BF16), vn_ref[...].astype(BF16),
                                     preferred_element_type=F32)) / l_fin
        r3 = lax.broadcasted_iota(jnp.int32, (nr, nr), 0)
        c3 = lax.broadcasted_iota(jnp.int32, (nr, nr), 1)
        same_token = jnp.where((r3 >> head_bits) == (c3 >> head_bits), 1.0, 0.0)
        ssq = jnp.sum(jnp.dot(same_token, out * out, precision=HI, preferred_element_type=F32),
                      axis=1, keepdims=True)
        o_ref[...] = out * lax.rsqrt(ssq / (H_FOX * DH_FOX) + EPS) * g_ref[...]


def _fox_sample(q, kn, vn, lfn, cache_k, cache_v, lf_sums, page_table, layer, norm_g):
    ns, nr, _ = q.shape
    nc = kn.shape[1]
    n_pages = page_table.shape[1]
    n_pool, cols = cache_k.shape[1], cache_k.shape[2]
    assert n_pool % 8 == 0 and lf_sums.shape == (cache_k.shape[0] * n_pool, 2 * cols)
    npg = max(d for d in range(1, 9) if n_pages % d == 0)
    g = jnp.tile(norm_g.reshape(H_FOX, DH_FOX), (nr // H_FOX, 1))

    def page_of(b, s, i, pt):
        return pt[b * n_pages + (n_pages - 1 - (s * npg + i))]

    kern = functools.partial(_fox_sample_kernel, scale=DH_FOX ** -0.5, npg=npg, page_of=page_of)

    def paged(i):
        return pl.BlockSpec((None, None, cols, DH_FOX), lambda b, s, pt: (layer, page_of(b, s, i, pt), 0, 0))

    def paged_sums(i):
        return pl.BlockSpec((8, 2 * cols), lambda b, s, pt: ((layer * n_pool + page_of(b, s, i, pt)) // 8, 0))

    grid_spec = pltpu.PrefetchScalarGridSpec(
        num_scalar_prefetch=1,
        grid=(ns, n_pages // npg),
        in_specs=[
            pl.BlockSpec((None, nr, DH_FOX), lambda b, s, pt: (b, 0, 0)),
            pl.BlockSpec((None, nc, DH_FOX), lambda b, s, pt: (b, 0, 0)),
            pl.BlockSpec((None, nc, DH_FOX), lambda b, s, pt: (b, 0, 0)),
            pl.BlockSpec((None, 1, nc), lambda b, s, pt: (b, 0, 0)),
            pl.BlockSpec((nr, DH_FOX), lambda b, s, pt: (0, 0)),
        ] + [paged(i) for i in range(npg)] * 2 + [paged_sums(i) for i in range(npg)],
        out_specs=pl.BlockSpec((None, nr, DH_FOX), lambda b, s, pt: (b, 0, 0)),
        scratch_shapes=[
            pltpu.VMEM((nr, 1), F32),
            pltpu.VMEM((nr, 1), F32),
            pltpu.VMEM((nr, DH_FOX), F32),
            pltpu.VMEM((1, cols), F32),
        ],
    )
    return pl.pallas_call(
        kern,
        out_shape=jax.ShapeDtypeStruct((ns, nr, DH_FOX), F32),
        grid_spec=grid_spec,
        compiler_params=_params("parallel", "arbitrary"),
        name="fox_sample",
    )(page_table.reshape(-1), q, kn, vn, lfn, g,
      *([cache_k] * npg), *([cache_v] * npg), *([lf_sums] * npg))


def _ret_kernel(*refs, has_s0):
    if has_s0:
        (q_ref, k_ref, v_ref, gt_ref, cos_ref, sin_ref, dm_ref, xi_ref, zt_ref, gc_ref, gn_ref, s0_ref,
         o_ref, s_ref) = refs
    else:
        (q_ref, k_ref, v_ref, gt_ref, cos_ref, sin_ref, dm_ref, xi_ref, zt_ref, gc_ref, gn_ref,
         o_ref, s_ref) = refs

    @pl.when(pl.program_id(1) == 0)
    def _():
        if has_s0:
            s_ref[...] = s0_ref[...]
        else:
            s_ref[...] = jnp.zeros_like(s_ref)

    cos = cos_ref[...]
    sin = sin_ref[...]

    def rot(x):
        return x * cos + pltpu.roll(x, DH_RET // 2, 1) * sin

    for h in range(H_RET):
        sl = slice(h * DH_RET, (h + 1) * DH_RET)
        q = rot(q_ref[:, sl])
        k = rot(k_ref[:, sl]) * DH_RET ** -0.5
        v = v_ref[:, sl]
        s = s_ref[h]
        att = lax.dot_general(q, k, NT_DIMS, preferred_element_type=F32) * dm_ref[h]
        o = jnp.dot(att, v, preferred_element_type=F32) + jnp.dot(q * xi_ref[h], s, preferred_element_type=F32)
        s_ref[h] = gc_ref[h] * s + lax.dot_general(k * zt_ref[h], v, TN_DIMS, preferred_element_type=F32)
        mu = jnp.mean(o, -1, keepdims=True)
        var = jnp.mean(jnp.square(o - mu), -1, keepdims=True)
        on = (o - mu) * lax.rsqrt(var + EPS)
        gt = gt_ref[:, sl]
        o_ref[:, sl] = on * gn_ref[:, sl] * (gt * _sigmoid(gt))


def _ret_consts(c, c_eff):
    lg = jnp.log1p(-jnp.exp2(-5.0 - jnp.arange(H_RET, dtype=F32)))
    idx = jnp.arange(c, dtype=F32)
    rel = idx[:, None] - idx[None, :]
    dmask = jnp.where(rel >= 0, jnp.exp(lg[:, None, None] * jnp.maximum(rel, 0.0)), 0.0)
    xi = jnp.exp(lg[:, None] * (idx + 1.0))[..., None]
    zeta = jnp.exp(lg[:, None] * (c_eff - 1.0 - idx))[..., None]
    g_c = jnp.broadcast_to(jnp.exp(lg * c_eff)[:, None, None], (H_RET, 1, DH_RET))
    return dmask, xi, zeta, g_c


def _rope_tables(pos):
    inv = ROPE_BASE ** (-jnp.arange(0, DH_RET, 2, dtype=F32) / DH_RET)
    ang = pos.astype(F32)[:, None] * inv[None, :]
    cos, sin = jnp.cos(ang), jnp.sin(ang)
    return jnp.concatenate([cos, cos], -1), jnp.concatenate([-sin, sin], -1)


def _retention_prompt(z, gn_g, nb, seq, t_all):
    c = BLOCK
    nc = seq // c
    dmask, xi, zeta, g_c = _ret_consts(c, c)
    cos, sin = _rope_tables(jnp.arange(seq))

    def zspec(col):
        return pl.BlockSpec((c, RET_W), lambda b, i: (b * nc + i, col))

    full = lambda shape: pl.BlockSpec(shape, lambda b, i: (0,) * len(shape))
    return pl.pallas_call(
        functools.partial(_ret_kernel, has_s0=False),
        out_shape=(jax.ShapeDtypeStruct((t_all, RET_W), F32),
                   jax.ShapeDtypeStruct((nb, H_RET, DH_RET, DH_RET), F32)),
        grid=(nb, nc),
        in_specs=[zspec(0), zspec(1), zspec(2), zspec(3),
                  pl.BlockSpec((c, DH_RET), lambda b, i: (i, 0)),
                  pl.BlockSpec((c, DH_RET), lambda b, i: (i, 0)),
                  full((H_RET, c, c)), full((H_RET, c, 1)), full((H_RET, c, 1)), full((H_RET, 1, DH_RET)),
                  full((1, RET_W))],
        out_specs=(pl.BlockSpec((c, RET_W), lambda b, i: (b * nc + i, 0)),
                   pl.BlockSpec((None, H_RET, DH_RET, DH_RET), lambda b, i: (b, 0, 0, 0))),
        compiler_params=_params("parallel", "arbitrary"),
        name="retention_prompt",
    )(z, z, z, z, cos, sin, dmask, xi, zeta, g_c, gn_g.reshape(1, RET_W))


def _retention_sample(zs, state_ret, layer, gn_g, pos0, n_new):
    ns, r, _ = zs.shape
    dmask, xi, zeta, g_c = _ret_consts(r, n_new)
    cos, sin = _rope_tables(pos0 + jnp.arange(r))

    def zspec(col):
        return pl.BlockSpec((None, r, RET_W), lambda b, i: (b, 0, col))

    full = lambda shape: pl.BlockSpec(shape, lambda b, i: (0,) * len(shape))
    return pl.pallas_call(
        functools.partial(_ret_kernel, has_s0=True),
        out_shape=(jax.ShapeDtypeStruct((ns, r, RET_W), F32),
                   jax.ShapeDtypeStruct((ns, H_RET, DH_RET, DH_RET), F32)),
        grid=(ns, 1),
        in_specs=[zspec(0), zspec(1), zspec(2), zspec(3),
                  full((r, DH_RET)), full((r, DH_RET)),
                  full((H_RET, r, r)), full((H_RET, r, 1)), full((H_RET, r, 1)), full((H_RET, 1, DH_RET)),
                  full((1, RET_W)),
                  pl.BlockSpec((None, None, H_RET, DH_RET, DH_RET), lambda b, i: (layer, b, 0, 0, 0))],
        out_specs=(pl.BlockSpec((None, r, RET_W), lambda b, i: (b, 0, 0)),
                   pl.BlockSpec((None, H_RET, DH_RET, DH_RET), lambda b, i: (b, 0, 0, 0))),
        compiler_params=_params("parallel", "arbitrary"),
        name="retention_sample",
    )(zs, zs, zs, zs, cos, sin, dmask, xi, zeta, g_c, gn_g.reshape(1, RET_W), state_ret)


def _s5_mats(a_re, a_im, b_re, b_im, c_re, c_im, d_skip, log_dt, c, c_eff):
    lam = lax.complex(a_re, a_im)
    lam_dt = lam * jnp.exp(log_dt)[:, None]
    lam_bar = jnp.exp(lam_dt)
    b_bar = ((lam_bar - 1.0) / lam)[..., None] * lax.complex(b_re, b_im)
    c_mat = lax.complex(c_re, c_im)
    g, p, cg = b_bar.shape

    def power(n):
        return jnp.exp(lam_dt[None] * jnp.asarray(n, F32)[:, None, None])

    lags = jnp.real(jnp.einsum('gop,jgp,gpi->jgoi', c_mat, power(jnp.arange(c)), b_bar, precision=HI))
    sig = jnp.arange(c)[:, None, None]
    tau = jnp.arange(c)[None, :, None]
    lag_of = (tau - sig == jnp.arange(c)[None, None, :]).astype(F32)
    m = jnp.einsum('stj,jgoi->gsito', lag_of, lags, precision=HI)
    m = m + (jnp.eye(c)[None, :, None, :, None] * jnp.eye(cg)[None, None, :, None, :]
             * d_skip.reshape(g, 1, cg, 1, 1))
    m = m.reshape(g, c * cg, c * cg)
    w = power(c_eff - 1.0 - jnp.arange(c, dtype=F32))[:, :, :, None] * b_bar[None]
    w = jnp.where((jnp.arange(c) < c_eff)[:, None, None, None], w, 0.0)
    w = w.transpose(1, 0, 3, 2).reshape(g, c * cg, p)
    v = power(jnp.arange(c, dtype=F32) + 1.0)[:, :, None, :] * c_mat[None]
    v = v.transpose(1, 3, 0, 2).reshape(g, p, c * cg)
    a = power(jnp.asarray([c_eff], F32))[0]
    return (m.astype(F32), jnp.real(w), jnp.imag(w), jnp.real(v), -jnp.imag(v), jnp.real(a), jnp.imag(a))


def _s5_prompt_kernel(u_ref, m_ref, wre_ref, wim_ref, vre_ref, vim_ref, are_ref, aim_ref,
                      y_ref, hre_ref, him_ref, ere_sc, eim_sc, hsre_sc, hsim_sc):
    ns, nc, _ = u_ref.shape
    p = wre_ref.shape[1]
    ere_sc[...] = jnp.concatenate(
        [jnp.dot(u_ref[s], wre_ref[...], precision=HI, preferred_element_type=F32) for s in range(ns)], axis=1)
    eim_sc[...] = jnp.concatenate(
        [jnp.dot(u_ref[s], wim_ref[...], precision=HI, preferred_element_type=F32) for s in range(ns)], axis=1)
    ar = are_ref[...]
    ai = aim_ref[...]

    def body(k, carry):
        hr, hi = carry
        hsre_sc[pl.ds(k, 1), :] = hr
        hsim_sc[pl.ds(k, 1), :] = hi
        er = ere_sc[pl.ds(k, 1), :]
        ei = eim_sc[pl.ds(k, 1), :]
        return ar * hr - ai * hi + er, ar * hi + ai * hr + ei

    zero = jnp.zeros((1, 2 * SSM_P), F32)
    hr, hi = lax.fori_loop(0, nc, body, (zero, zero))
    hre_ref[...] = hr
    him_ref[...] = hi
    hsre = hsre_sc[...]
    hsim = hsim_sc[...]
    for s in range(ns):
        y_ref[s] = (jnp.dot(u_ref[s], m_ref[...], preferred_element_type=F32)
                    + jnp.dot(hsre[:, s * p:(s + 1) * p], vre_ref[...], preferred_element_type=F32)
                    + jnp.dot(hsim[:, s * p:(s + 1) * p], vim_ref[...], preferred_element_type=F32))


def _s5_pack_kernel(su_ref, u_ref):
    ng, kb = u_ref.shape[0], u_ref.shape[1]
    rows = [su_ref[pl.ds(t, kb, stride=SSM_GROUP), :] for t in range(SSM_GROUP)]
    for g in range(ng):
        u_ref[g] = jnp.concatenate([r[:, g * SSM_GROUP:(g + 1) * SSM_GROUP] for r in rows], axis=1)


def _s5_unpack_kernel(y_ref, o_ref):
    ng, kb = y_ref.shape[0], y_ref.shape[1]
    groups = [y_ref[g] for g in range(ng)]
    for t in range(SSM_GROUP):
        o_ref[pl.ds(t, kb, stride=SSM_GROUP), :] = jnp.concatenate(
            [y[:, t * SSM_GROUP:(t + 1) * SSM_GROUP] for y in groups], axis=1)


def _s5_relayout(x, nb, seq, to_groups):
    c = SSM_GROUP
    nc = seq // c
    cw = c * SSM_GROUP
    ng = 128 // SSM_GROUP
    kb = max(d for d in (64, 32, 16, 8) if nc % d == 0)
    tok = pl.BlockSpec((kb * c, 128), lambda s, i, j: (s * (nc // kb) + i, j))
    grp = pl.BlockSpec((ng, None, kb, cw), lambda s, i, j: (j, s, i, 0))
    return pl.pallas_call(
        _s5_pack_kernel if to_groups else _s5_unpack_kernel,
        out_shape=jax.ShapeDtypeStruct((SSM_G, nb, nc, cw) if to_groups else (nb * seq, SSM_W), F32),
        grid=(nb, nc // kb, SSM_G // ng),
        in_specs=[tok if to_groups else grp],
        out_specs=grp if to_groups else tok,
        compiler_params=_params("parallel", "parallel", "parallel"),
        name="s5_pack" if to_groups else "s5_unpack",
    )(x)


def _s5_prompt(su, mats, nb, seq):
    assert nb == 2
    c = SSM_GROUP
    nc = seq // c
    cw = c * SSM_GROUP
    m, w_re, w_im, v_re, v_im, a_re, a_im = mats
    u = _s5_relayout(su, nb, seq, True)

    gspec = lambda shape: pl.BlockSpec((None,) + shape, lambda g: (g,) + (0,) * len(shape))
    y, hre, him = pl.pallas_call(
        _s5_prompt_kernel,
        out_shape=(jax.ShapeDtypeStruct((SSM_G, nb, nc, cw), F32),
                   jax.ShapeDtypeStruct((SSM_G, 1, nb * SSM_P), F32),
                   jax.ShapeDtypeStruct((SSM_G, 1, nb * SSM_P), F32)),
        grid=(SSM_G,),
        in_specs=[gspec((nb, nc, cw)), gspec((cw, cw)),
                  gspec((cw, SSM_P)), gspec((cw, SSM_P)), gspec((SSM_P, cw)), gspec((SSM_P, cw)),
                  gspec((1, nb * SSM_P)), gspec((1, nb * SSM_P))],
        out_specs=(gspec((nb, nc, cw)), gspec((1, nb * SSM_P)), gspec((1, nb * SSM_P))),
        scratch_shapes=[pltpu.VMEM((nc, nb * SSM_P), F32) for _ in range(4)],
        compiler_params=_params("parallel"),
        name="s5_prompt",
    )(u, m, w_re, w_im, v_re, v_im, jnp.tile(a_re, (1, nb))[:, None, :], jnp.tile(a_im, (1, nb))[:, None, :])
    y = _s5_relayout(y, nb, seq, False)
    unpack = lambda h: h.reshape(SSM_G, nb, SSM_P).transpose(1, 0, 2)
    return y, unpack(hre), unpack(him)


def _s5_sample_kernel(u_ref, h0re_ref, h0im_ref, m_ref, wre_ref, wim_ref, vre_ref, vim_ref, are_ref, aim_ref,
                      y_ref, hre_ref, him_ref):
    u = u_ref[...]
    hr = h0re_ref[...]
    hi = h0im_ref[...]
    ar = are_ref[...]
    ai = aim_ref[...]
    hre_ref[...] = ar * hr - ai * hi + jnp.dot(u, wre_ref[...], precision=HI, preferred_element_type=F32)
    him_ref[...] = ar * hi + ai * hr + jnp.dot(u, wim_ref[...], precision=HI, preferred_element_type=F32)
    y_ref[...] = (jnp.dot(u, m_ref[...], precision=HI, preferred_element_type=F32)
                  + jnp.dot(hr, vre_ref[...], precision=HI, preferred_element_type=F32)
                  + jnp.dot(hi, vim_ref[...], precision=HI, preferred_element_type=F32))


def _s5_sample(su, h0_re, h0_im, mats, n_new):
    ns, r, _ = su.shape
    cw = r * SSM_GROUP
    m, w_re, w_im, v_re, v_im, a_re, a_im = mats
    u = su.reshape(ns, r, SSM_G, SSM_GROUP).transpose(2, 0, 1, 3).reshape(SSM_G, ns, cw)
    gspec = lambda shape: pl.BlockSpec((None,) + shape, lambda g: (g,) + (0,) * len(shape))
    y, hre, him = pl.pallas_call(
        _s5_sample_kernel,
        out_shape=(jax.ShapeDtypeStruct((SSM_G, ns, cw), F32),
                   jax.ShapeDtypeStruct((SSM_G, ns, SSM_P), F32),
                   jax.ShapeDtypeStruct((SSM_G, ns, SSM_P), F32)),
        grid=(SSM_G,),
        in_specs=[gspec((ns, cw)), gspec((ns, SSM_P)), gspec((ns, SSM_P)), gspec((cw, cw)),
                  gspec((cw, SSM_P)), gspec((cw, SSM_P)), gspec((SSM_P, cw)), gspec((SSM_P, cw)),
                  gspec((1, SSM_P)), gspec((1, SSM_P))],
        out_specs=(gspec((ns, cw)), gspec((ns, SSM_P)), gspec((ns, SSM_P))),
        compiler_params=_params("parallel"),
        name="s5_sample",
    )(u, h0_re.transpose(1, 0, 2), h0_im.transpose(1, 0, 2), m, w_re, w_im, v_re, v_im,
      a_re[:, None, :], a_im[:, None, :])
    y = y.reshape(SSM_G, ns, r, SSM_GROUP)[:, :, :n_new].transpose(1, 2, 0, 3).reshape(ns * n_new, SSM_W)
    return y, hre.transpose(1, 0, 2), him.transpose(1, 0, 2)


def _glu_kernel(y_ref, w_ref, b_ref, g_ref, o_ref):
    sy = _gelu(y_ref[...])
    gate = _sigmoid(jnp.dot(sy.astype(BF16), w_ref[...], preferred_element_type=F32) + b_ref[...])
    o_ref[...] = _rms_norm(sy * gate, g_ref[...])


def _s5_glu(y, w_glu, b_glu, norm_g, tm):
    t = y.shape[0]
    row = lambda: pl.BlockSpec((1, SSM_W), lambda i: (0, 0))
    return pl.pallas_call(
        _glu_kernel,
        out_shape=jax.ShapeDtypeStruct((t, SSM_W), F32),
        grid=(t // tm,),
        in_specs=[pl.BlockSpec((tm, SSM_W), lambda i: (i, 0)),
                  pl.BlockSpec((SSM_W, SSM_W), lambda i: (0, 0)), row(), row()],
        out_specs=pl.BlockSpec((tm, SSM_W), lambda i: (i, 0)),
        compiler_params=_params("parallel"),
        name="s5_glu",
    )(y, w_glu.astype(BF16), b_glu.reshape(1, SSM_W), norm_g.reshape(1, SSM_W))


def _outproj_kernel(ro_ref, fo_ref, so_ref, x_ref, wr_ref, wf_ref, ws_ref, g_ref, b_ref, o_ref):
    mix = jnp.dot(ro_ref[...].astype(BF16), wr_ref[...], preferred_element_type=F32)
    mix += jnp.dot(fo_ref[...].astype(BF16), wf_ref[...], preferred_element_type=F32)
    mix += jnp.dot(so_ref[...].astype(BF16), ws_ref[...], preferred_element_type=F32)
    o_ref[...] = _layer_norm(DEEPNORM_ALPHA * x_ref[...] + mix, g_ref[...], b_ref[...])


def _outproj_ln(ro, fo, so, x, w_out, ln_g, ln_b, tm):
    t = x.shape[0]
    w = w_out.astype(BF16)
    tile = lambda n: pl.BlockSpec((tm, n), lambda i: (i, 0))
    full = lambda r, n: pl.BlockSpec((r, n), lambda i: (0, 0))
    return pl.pallas_call(
        _outproj_kernel,
        out_shape=jax.ShapeDtypeStruct((t, D_MODEL), F32),
        grid=(t // tm,),
        in_specs=[tile(RET_W), tile(FOX_W), tile(SSM_W), tile(D_MODEL),
                  full(RET_W, D_MODEL), full(FOX_W, D_MODEL), full(SSM_W, D_MODEL),
                  full(1, D_MODEL), full(1, D_MODEL)],
        out_specs=tile(D_MODEL),
        compiler_params=_params("parallel"),
        name="outproj_ln",
    )(ro, fo, so, x, w[:RET_W], w[RET_W:RET_W + FOX_W], w[RET_W + FOX_W:],
      ln_g.reshape(1, D_MODEL), ln_b.reshape(1, D_MODEL))


def _topk_rows(problems, k):
    def body(it, _):
        for s_ref, val_ref, idx_ref in problems:
            n = s_ref.shape[0]
            iota = lax.broadcasted_iota(jnp.int32, s_ref.shape, 0)
            s = s_ref[...]
            m = jnp.max(s, axis=0, keepdims=True)
            idx = jnp.min(jnp.where(s == m, iota, n), axis=0, keepdims=True)
            val_ref[pl.ds(it, 1), :] = m
            idx_ref[pl.ds(it, 1), :] = idx
            s_ref[...] = jnp.where(iota == idx, -jnp.inf, s)
        return 0

    lax.fori_loop(0, k, body, 0)


CAND_COUNTS = [PEER_TOPK // (a + 1) for a in range(PEER_TOPK)]
CAND_STARTS = [sum(CAND_COUNTS[:a]) for a in range(PEER_TOPK)]
N_CAND = sum(CAND_COUNTS)
N_CAND_PAD = -(-N_CAND // 8) * 8


def _peer_topk_kernel(q_ref, skh_ref, skl_ref, i_ref, j_ref, gate_ref, s0_sc, s1_sc, cand_sc, v0_sc, v1_sc,
                      x0_sc, x1_sc, top_sc, sel_sc):
    kk = PEER_TOPK
    q = q_ref[...]
    qh = q.astype(BF16)
    ql = (q - qh.astype(F32)).astype(BF16)
    skh = skh_ref[...]
    s = lax.dot_general(skh, ql, NT_DIMS, preferred_element_type=F32)
    s += lax.dot_general(skl_ref[...], qh, NT_DIMS, preferred_element_type=F32)
    s += lax.dot_general(skh, qh, NT_DIMS, preferred_element_type=F32)
    s0_sc[...] = s[:PEER_NKEYS]
    s1_sc[...] = s[PEER_NKEYS:]
    _topk_rows([(s0_sc, v0_sc, x0_sc), (s1_sc, v1_sc, x1_sc)], kk)
    v1 = v1_sc[...]
    for a in range(kk):
        cand_sc[CAND_STARTS[a]:CAND_STARTS[a] + CAND_COUNTS[a], :] = v0_sc[a:a + 1, :] + v1[:CAND_COUNTS[a]]
    if N_CAND_PAD > N_CAND:
        cand_sc[N_CAND:, :] = jnp.full((N_CAND_PAD - N_CAND, cand_sc.shape[1]), -jnp.inf, F32)
    _topk_rows([(cand_sc, top_sc, sel_sc)], kk)
    sel = sel_sc[...]
    a_idx = jnp.zeros_like(sel)
    start = jnp.zeros_like(sel)
    for a in range(1, kk):
        past = sel >= CAND_STARTS[a]
        a_idx = jnp.where(past, a, a_idx)
        start = jnp.where(past, CAND_STARTS[a], start)
    b_idx = sel - start
    ii = jnp.zeros_like(sel)
    jj = jnp.zeros_like(sel)
    for a in range(kk):
        ii = jnp.where(a_idx == a, x0_sc[a:a + 1, :], ii)
        jj = jnp.where(b_idx == a, x1_sc[a:a + 1, :], jj)
    top = top_sc[...]
    e = jnp.exp(top - top[0:1, :])
    i_ref[...] = ii
    j_ref[...] = jj
    gate_ref[...] = e / jnp.sum(e, axis=0, keepdims=True)


def _peer_topk(q, sub_keys, tm):
    t = q.shape[0]
    half = PEER_DK // 2
    sk = jnp.zeros((PEER_HEADS, 2 * PEER_NKEYS, PEER_DK), F32)
    sk = sk.at[:, :PEER_NKEYS, :half].set(sub_keys[:, 0]).at[:, PEER_NKEYS:, half:].set(sub_keys[:, 1])
    kk = PEER_TOPK
    slot = lambda: pl.BlockSpec((kk, tm), lambda i, h: (h, i))
    return pl.pallas_call(
        _peer_topk_kernel,
        out_shape=(jax.ShapeDtypeStruct((PEER_HEADS * kk, t), jnp.int32),
                   jax.ShapeDtypeStruct((PEER_HEADS * kk, t), jnp.int32),
                   jax.ShapeDtypeStruct((PEER_HEADS * kk, t), F32)),
        grid=(t // tm, PEER_HEADS),
        in_specs=[pl.BlockSpec((tm, PEER_DK), lambda i, h: (i, h)),
                  pl.BlockSpec((None, 2 * PEER_NKEYS, PEER_DK), lambda i, h: (h, 0, 0)),
                  pl.BlockSpec((None, 2 * PEER_NKEYS, PEER_DK), lambda i, h: (h, 0, 0))],
        out_specs=(slot(), slot(), slot()),
        scratch_shapes=[pltpu.VMEM((PEER_NKEYS, tm), F32), pltpu.VMEM((PEER_NKEYS, tm), F32),
                        pltpu.VMEM((N_CAND_PAD, tm), F32),
                        pltpu.VMEM((kk, tm), F32), pltpu.VMEM((kk, tm), F32),
                        pltpu.VMEM((kk, tm), jnp.int32), pltpu.VMEM((kk, tm), jnp.int32),
                        pltpu.VMEM((kk, tm), F32), pltpu.VMEM((kk, tm), jnp.int32)],
        compiler_params=_params("parallel", "arbitrary"),
        name="peer_topk",
    )(q, *_split_bf16(sk))


G_TOKENS = 128
G_ROW_STRIDE = G_TOKENS + 8
G_UNROLL = 64


def _peer_gates_kernel(i_ref, j_ref, gate_ref, o_ref, g_sc):
    tb, ns = i_ref.shape
    nk = PEER_NKEYS
    io = lax.broadcasted_iota(jnp.int32, (nk, ns), 0)

    def body(tt, carry):
        for u in range(G_UNROLL):
            t = tt * G_UNROLL + u
            irow = i_ref[pl.ds(t, 1), :]
            jrow = j_ref[pl.ds(t, 1), :]
            grow = gate_ref[pl.ds(t, 1), :]
            pt = jnp.where(io == irow, grow, 0.0).astype(BF16)
            qt = jnp.where(io == jrow, 1.0, 0.0).astype(BF16)
            g = lax.dot_general(pt, qt, NT_DIMS, preferred_element_type=F32)
            g_sc[pl.ds(t, nk, stride=G_ROW_STRIDE), :] = g
        return carry

    lax.fori_loop(0, tb // G_UNROLL, body, 0)
    for i in range(nk):
        o_ref[:, i * nk:(i + 1) * nk] = g_sc[i * G_ROW_STRIDE:i * G_ROW_STRIDE + tb, :].astype(BF16)


def _peer_gates(ii, jj, gate):
    t = ii.shape[0]
    tb = G_TOKENS
    ns = PEER_HEADS * PEER_TOPK
    tok = lambda: pl.BlockSpec((tb, ns), lambda i: (i, 0))
    return pl.pallas_call(
        _peer_gates_kernel,
        out_shape=jax.ShapeDtypeStruct((t, PEER_N), BF16),
        grid=(t // tb,),
        in_specs=[tok(), tok(), tok()],
        out_specs=pl.BlockSpec((tb, PEER_N), lambda i: (i, 0)),
        scratch_shapes=[pltpu.VMEM((PEER_NKEYS * G_ROW_STRIDE, PEER_NKEYS), F32)],
        compiler_params=_params("parallel"),
        name="peer_gates",
    )(ii, jj, gate)


def _peer_experts_kernel(x_ref, g_ref, u_ref, v_ref, lg_ref, lb_ref, o_ref, xb_sc, acc_sc):
    e = pl.program_id(1)

    @pl.when(e == 0)
    def _():
        xb_sc[...] = x_ref[...].astype(BF16)
        acc_sc[...] = jnp.zeros_like(acc_sc)

    h = lax.dot_general(xb_sc[...], u_ref[...], NT_DIMS, preferred_element_type=F32)
    w = (g_ref[...].astype(F32) * _gelu(h)).astype(BF16)
    acc_sc[...] += jnp.dot(w, v_ref[...], preferred_element_type=F32)

    @pl.when(e == pl.num_programs(1) - 1)
    def _():
        o_ref[...] = _layer_norm(DEEPNORM_ALPHA * x_ref[...] + acc_sc[...], lg_ref[...], lb_ref[...])


def _peer_experts_ln(x, g, u_tabs, v_tabs, layer, ln_g, ln_b, tm, nb):
    t = x.shape[0]
    return pl.pallas_call(
        _peer_experts_kernel,
        out_shape=jax.ShapeDtypeStruct((t, D_MODEL), F32),
        grid=(t // tm, PEER_N // nb),
        in_specs=[pl.BlockSpec((tm, D_MODEL), lambda i, e: (i, 0)),
                  pl.BlockSpec((tm, nb), lambda i, e: (i, e)),
                  pl.BlockSpec((None, nb, D_MODEL), lambda i, e: (layer, e, 0)),
                  pl.BlockSpec((None, nb, D_MODEL), lambda i, e: (layer, e, 0)),
                  pl.BlockSpec((1, D_MODEL), lambda i, e: (0, 0)),
                  pl.BlockSpec((1, D_MODEL), lambda i, e: (0, 0))],
        out_specs=pl.BlockSpec((tm, D_MODEL), lambda i, e: (i, 0)),
        scratch_shapes=[pltpu.VMEM((tm, D_MODEL), BF16), pltpu.VMEM((tm, D_MODEL), F32)],
        compiler_params=_params("parallel", "arbitrary"),
        name="peer_experts_ln",
    )(x, g, u_tabs, v_tabs, ln_g.reshape(1, D_MODEL), ln_b.reshape(1, D_MODEL))


def _ple_kernel(x_ref, xn_ref, p_ref, wg_ref, bg_ref, we_ref, o_ref):
    gate = _sigmoid(jnp.dot(x_ref[...].astype(BF16), wg_ref[...], preferred_element_type=F32) + bg_ref[...])
    emb = jnp.dot(p_ref[...].astype(BF16), we_ref[...], preferred_element_type=F32)
    o_ref[...] = xn_ref[...] + gate * emb


def _ple(x, p, w_pg, b_pg, w_pe, tm, tn):
    t = x.shape[0]
    return pl.pallas_call(
        _ple_kernel,
        out_shape=jax.ShapeDtypeStruct((t, D_MODEL), F32),
        grid=(t // tm, D_MODEL // tn),
        in_specs=[pl.BlockSpec((tm, D_MODEL), lambda i, j: (i, 0)),
                  pl.BlockSpec((tm, tn), lambda i, j: (i, j)),
                  pl.BlockSpec((tm, PLE_DIM), lambda i, j: (i, 0)),
                  pl.BlockSpec((D_MODEL, tn), lambda i, j: (0, j)),
                  pl.BlockSpec((1, tn), lambda i, j: (0, j)),
                  pl.BlockSpec((PLE_DIM, tn), lambda i, j: (0, j))],
        out_specs=pl.BlockSpec((tm, tn), lambda i, j: (i, j)),
        compiler_params=_params("parallel", "arbitrary"),
        name="ple_gate",
    )(x, x, p, w_pg.astype(BF16), b_pg.reshape(1, D_MODEL), w_pe.astype(BF16))


def _split_bf16(w):
    hi = w.astype(BF16)
    return hi, (w - hi.astype(F32)).astype(BF16)


def _layer(x, p, lw, caches, stacked, page_table, layer, nb, seq, ns, n_new, tiles):
    t_all = x.shape[0]
    tp = nb * seq
    r = 8
    cache_k, cache_v, lf_sums, state_ret, state_re, state_im = caches
    past_len = page_table.shape[1] * (cache_k.shape[2] // H_FOX)

    u_tabs, v_tabs, w_in_bf16, w_in = stacked
    z = _in_proj(x, w_in_bf16, layer, tiles['tm_in'], 512)
    lf = _forget_gates(x, w_in[layer, :, Z_FF:Z_SU], lw['b_fox_f'], tiles['tm'])
    zs = jnp.pad(z[tp:].reshape(ns, n_new, Z_W), ((0, 0), (0, r - n_new), (0, 0)))

    ro, ret_p = _retention_prompt(z, lw['ret_gn_g'], nb, seq, t_all)
    ro_s, ret_s = _retention_sample(zs, state_ret, layer, lw['ret_gn_g'], past_len, n_new)
    ro = ro.at[tp:].set(ro_s[:, :n_new].reshape(ns * n_new, RET_W))

    fo = _fox_prompt(z, _forget_cumsum(lf, nb, seq), lw['fox_norm_g'], nb, seq, t_all, tiles['attn'])
    lf_s = lf[:, tp:].T.reshape(ns, n_new, H_FOX)
    lfn = jnp.pad(lf_s, ((0, 0), (0, r - n_new), (0, 0))).reshape(ns, 1, r * H_FOX)
    fo_s = _fox_sample(z[tp:, Z_FQ:Z_FQ + FOX_W].reshape(ns, n_new * H_FOX, DH_FOX),
                       zs[:, :, Z_FK:Z_FK + FOX_W].reshape(ns, r * H_FOX, DH_FOX),
                       zs[:, :, Z_FV:Z_FV + FOX_W].reshape(ns, r * H_FOX, DH_FOX),
                       lfn, cache_k, cache_v, lf_sums, page_table, layer, lw['fox_norm_g'])
    fo = fo.at[tp:].set(fo_s.reshape(ns * n_new, FOX_W))

    ssm = (lw['ssm_a_re'], lw['ssm_a_im'], lw['ssm_b_re'], lw['ssm_b_im'], lw['ssm_c_re'], lw['ssm_c_im'],
           lw['ssm_d'], lw['ssm_log_dt'])
    y_p, hre_p, him_p = _s5_prompt(z[:tp, Z_SU:], _s5_mats(*ssm, SSM_GROUP, SSM_GROUP), nb, seq)
    y_s, hre_s, him_s = _s5_sample(zs[:, :, Z_SU:], state_re[layer], state_im[layer], _s5_mats(*ssm, r, n_new), n_new)
    so = _s5_glu(jnp.concatenate([y_p, y_s], 0), lw['w_glu'], lw['b_glu'], lw['ssm_norm_g'], tiles['tm'])

    x1 = _outproj_ln(ro, fo, so, x, lw['w_out'], lw['ln1_g'], lw['ln1_b'], tiles['tm_ln'])

    q = _matmul(x1, _split_bf16(lw['peer_w_q']), tiles['tm'], 512, "peer_query")
    ii, jj, gate = _peer_topk(q, lw['peer_sub_keys'], tiles['tm'])
    g = _peer_gates(ii.T, jj.T, gate.T)
    x2 = _peer_experts_ln(x1, g, u_tabs, v_tabs, layer, lw['ln2_g'], lw['ln2_b'], tiles['tm'], 512)

    x3 = _ple(x2, p, lw['w_pg'], lw['b_pg'], lw['w_pe'], tiles['tm'], 512)

    st_p = (z[:tp, Z_FK:Z_FK + FOX_W].reshape(nb, seq, H_FOX, DH_FOX),
            z[:tp, Z_FV:Z_FV + FOX_W].reshape(nb, seq, H_FOX, DH_FOX),
            lf[:, :tp].T.reshape(nb, seq, H_FOX), ret_p, hre_p, him_p)
    st_s = (z[tp:, Z_FK:Z_FK + FOX_W].reshape(ns, n_new, H_FOX, DH_FOX),
            z[tp:, Z_FV:Z_FV + FOX_W].reshape(ns, n_new, H_FOX, DH_FOX),
            lf_s, ret_s, hre_s, him_s)
    return x3, st_p, st_s


def kernel(x_prompt, x_sample, p_prompt, p_sample, cache_fox_k, cache_fox_v, cache_fox_lf, state_ret, state_ssm_re, state_ssm_im, page_table, w_in, w_out, ret_gn_g, b_fox_f, fox_norm_g, ssm_a_re, ssm_a_im, ssm_b_re, ssm_b_im, ssm_c_re, ssm_c_im, ssm_d, ssm_log_dt, w_glu, b_glu, ssm_norm_g, ln1_g, ln1_b, ln2_g, ln2_b, peer_w_q, peer_sub_keys, peer_u, peer_v, w_pe, w_pg, b_pg):
    nb, seq, d = x_prompt.shape
    ns, n_new, _ = x_sample.shape
    tp, ts = nb * seq, ns * n_new
    t_all = tp + ts
    tm = 640 if t_all % 640 == 0 else 128
    tiles = dict(tm=tm, tm_ln=tm // 2 if tm == 640 else tm, tm_in=832 if t_all % 832 == 0 else tm,
                 attn=512 if seq % 512 == 0 else (256 if seq % 256 == 0 else 128))
    x = jnp.concatenate([x_prompt.reshape(tp, d), x_sample.reshape(ts, d)], 0)
    depth, n_pool, pg, nh, dh = cache_fox_k.shape
    caches = (cache_fox_k.reshape(depth, n_pool, pg * nh, dh), cache_fox_v.reshape(depth, n_pool, pg * nh, dh),
              _lf_prep(cache_fox_lf), state_ret, state_ssm_re, state_ssm_im)
    weights = dict(w_out=w_out, ret_gn_g=ret_gn_g, b_fox_f=b_fox_f, fox_norm_g=fox_norm_g,
                   ssm_a_re=ssm_a_re, ssm_a_im=ssm_a_im, ssm_b_re=ssm_b_re, ssm_b_im=ssm_b_im,
                   ssm_c_re=ssm_c_re, ssm_c_im=ssm_c_im, ssm_d=ssm_d, ssm_log_dt=ssm_log_dt,
                   w_glu=w_glu, b_glu=b_glu, ssm_norm_g=ssm_norm_g, ln1_g=ln1_g, ln1_b=ln1_b,
                   ln2_g=ln2_g, ln2_b=ln2_b, peer_w_q=peer_w_q, peer_sub_keys=peer_sub_keys,
                   w_pe=w_pe, w_pg=w_pg, b_pg=b_pg)
    stacked = (peer_u.astype(BF16), peer_v.astype(BF16), w_in.astype(BF16), w_in)
    sts_p, sts_s = [], []
    for l in range(DEPTH):
        lw = {k: v[l] for k, v in weights.items()}
        p = jnp.concatenate([p_prompt[l].reshape(tp, PLE_DIM), p_sample[l].reshape(ts, PLE_DIM)], 0)
        x, st_p, st_s = _layer(x, p, lw, caches, stacked, page_table, l, nb, seq, ns, n_new, tiles)
        sts_p.append(st_p)
        sts_s.append(st_s)
    stack = lambda sts, i: jnp.stack([s[i] for s in sts])
    return ((x[:tp].reshape(nb, seq, d), x[tp:].reshape(ns, n_new, d))
            + tuple(stack(sts_p, i) for i in range(6)) + tuple(stack(sts_s, i) for i in range(6)))
```
